```python
import jax, jax.numpy as jnp
from jax import lax
import numpy as np

D_MODEL = 1024
BATCH = 4
SEQ = 4096
DEPTH = 2

MIX_WIDTH = D_MODEL
N_MIXERS = 4
GROUP_WIDTH = MIX_WIDTH // N_MIXERS
N_GROUP_HEADS = 4
HEAD_DIM = GROUP_WIDTH // N_GROUP_HEADS
SHORT_CONV_WIDTH = 3
CHUNK = 128
IDX_HEADS = 4
IDX_DIM = 64
TOPK_MAX = 256
CONF_CONV_WIDTH = 31
Q_BLOCK = 128
ROPE_THETA = 10000.0
FFN_HIDDEN = -(-8 * D_MODEL // (3 * 256)) * 256
NORM_EPS = 1e-6
LN_EPS = 1e-5
NEG = -1e30
SPLIT_SIZES = (GROUP_WIDTH, GROUP_WIDTH, GROUP_WIDTH,
               GROUP_WIDTH, GROUP_WIDTH,
               GROUP_WIDTH, GROUP_WIDTH, GROUP_WIDTH,
               IDX_HEADS * IDX_DIM, IDX_DIM, IDX_HEADS,
               GROUP_WIDTH, GROUP_WIDTH)
IN_COLS = sum(SPLIT_SIZES)

kernel_name = "hymba_style_four_mixer_hybrid"


def rmsnorm(x, g):
    xf = x.astype(jnp.float32)
    y = xf * lax.rsqrt(jnp.mean(xf * xf, axis=-1, keepdims=True) + NORM_EPS)
    return (y * g.astype(jnp.float32)).astype(x.dtype)


def layernorm(x, g, b):
    xf = x.astype(jnp.float32)
    mu = jnp.mean(xf, axis=-1, keepdims=True)
    var = jnp.mean(jnp.square(xf - mu), axis=-1, keepdims=True)
    y = (xf - mu) * lax.rsqrt(var + LN_EPS)
    return (y * g.astype(jnp.float32) + b.astype(jnp.float32)).astype(x.dtype)


def causal_dwconv(x, w):
    width, chans = w.shape
    return lax.conv_general_dilated(
        x, w[:, None, :].astype(x.dtype), window_strides=(1,),
        padding=[(width - 1, 0)], dimension_numbers=('NWC', 'WIO', 'NWC'),
        feature_group_count=chans)


def rope_tables(seq, dim):
    inv_freq = ROPE_THETA ** (-jnp.arange(0, dim, 2, dtype=jnp.float32) / dim)
    ang = jnp.arange(seq, dtype=jnp.float32)[:, None] * inv_freq[None, :]
    return jnp.cos(ang), jnp.sin(ang)


def apply_rope(x, cos, sin):
    half = x.shape[-1] // 2
    c = cos[None, :, None, :].astype(x.dtype)
    s = sin[None, :, None, :].astype(x.dtype)
    x1, x2 = x[..., :half], x[..., half:]
    return jnp.concatenate([x1 * c - x2 * s, x2 * c + x1 * s], axis=-1)


def short_conv_mixer(b, c, h, w_conv):
    return b * causal_dwconv(c * h, w_conv)


def gmlp_mixer(u, v, ln_g, ln_b, w_s, b_s):
    bsz, seq, _ = v.shape
    vn = layernorm(v, ln_g, ln_b)
    vc = vn.reshape(bsz, seq // CHUNK, CHUNK, N_GROUP_HEADS, HEAD_DIM)
    mask = jnp.tril(jnp.ones((CHUNK, CHUNK), dtype=bool))
    ws = jnp.where(mask[None], w_s, 0).astype(v.dtype)
    mixed = jnp.einsum('hts,bnshd->bnthd', ws, vc)
    mixed = mixed + b_s.T[None, None, :, :, None].astype(v.dtype)
    return u * mixed.reshape(bsz, seq, GROUP_WIDTH)


def dsa_mixer(q, k, v, q_idx, k_idx, w_idx):
    bsz, seq, _ = q.shape
    topk = min(TOPK_MAX, seq // 4)
    cos_a, sin_a = rope_tables(seq, HEAD_DIM)
    cos_i, sin_i = rope_tables(seq, IDX_DIM)
    q = apply_rope(q.reshape(bsz, seq, N_GROUP_HEADS, HEAD_DIM), cos_a, sin_a)
    k = apply_rope(k.reshape(bsz, seq, N_GROUP_HEADS, HEAD_DIM), cos_a, sin_a)
    v = v.reshape(bsz, seq, N_GROUP_HEADS, HEAD_DIM)
    qi = apply_rope(q_idx.reshape(bsz, seq, IDX_HEADS, IDX_DIM), cos_i, sin_i).astype(jnp.float32)
    ki = apply_rope(k_idx[:, :, None, :], cos_i, sin_i)[:, :, 0, :].astype(jnp.float32)
    wi = w_idx.astype(jnp.float32) * (IDX_HEADS ** -0.5) * (IDX_DIM ** -0.5)
    scale = HEAD_DIM ** -0.5
    nblk = seq // Q_BLOCK

    def to_blocks(t):
        return jnp.swapaxes(t.reshape((bsz, nblk, Q_BLOCK) + t.shape[2:]), 0, 1)

    kpos = jnp.arange(seq)
    tpos = kpos.reshape(nblk, Q_BLOCK)

    def block(args):
        qb, qib, wib, tb = args
        rel = jax.nn.relu(jnp.einsum('bqjd,bsd->bqjs', qib, ki))
        score = jnp.einsum('bqjs,bqj->bqs', rel, wib)
        admissible = kpos[None, :] <= tb[:, None]
        score = jnp.where(admissible[None], score, NEG)
        _, idx = lax.top_k(score, topk)
        valid = idx <= tb[None, :, None]
        kg = jax.vmap(lambda kk, ii: kk[ii])(k, idx)
        vg = jax.vmap(lambda vv, ii: vv[ii])(v, idx)
        logits = jnp.einsum('bqhd,bqkhd->bqhk', qb, kg).astype(jnp.float32) * scale
        logits = jnp.where(valid[:, :, None, :], logits, NEG)
        p = jax.nn.softmax(logits, axis=-1).astype(v.dtype)
        return jnp.einsum('bqhk,bqkhd->bqhd', p, vg)

    out = lax.map(block, (to_blocks(q), to_blocks(qi), to_blocks(wi), tpos))
    return jnp.swapaxes(out, 0, 1).reshape(bsz, seq, GROUP_WIDTH)


def conformer_conv_mixer(a, gate, w_dw, b_dw, ln_g, ln_b):
    y = a * jax.nn.sigmoid(gate)
    y = causal_dwconv(y, w_dw) + b_dw.astype(y.dtype)
    y = layernorm(y, ln_g, ln_b)
    return jax.nn.silu(y)


def hybrid_layer(x, g_mix, w_in, w_conv_a, gmlp_ln_g, gmlp_ln_b, w_s, b_s,
                 w_conf, b_conf, conf_ln_g, conf_ln_b, w_out,
                 g_ffn, w_gate, w_up, w_down):
    h = rmsnorm(x, g_mix)
    z = h @ w_in
    cuts, acc = [], 0
    for sz in SPLIT_SIZES[:-1]:
        acc += sz
        cuts.append(acc)
    (a_b, a_c, a_h, g_u, g_v, q, k, v, q_idx, k_idx, w_idx, d_a, d_g) = jnp.split(z, cuts, axis=-1)
    y_a = short_conv_mixer(a_b, a_c, a_h, w_conv_a)
    y_b = gmlp_mixer(g_u, g_v, gmlp_ln_g, gmlp_ln_b, w_s, b_s)
    y_c = dsa_mixer(q, k, v, q_idx, k_idx, w_idx)
    y_d = conformer_conv_mixer(d_a, d_g, w_conf, b_conf, conf_ln_g, conf_ln_b)
    x = x + jnp.concatenate([y_a, y_b, y_c, y_d], axis=-1) @ w_out
    hf = rmsnorm(x, g_ffn)
    return x + (jax.nn.silu(hf @ w_gate) * (hf @ w_up)) @ w_down


def setup_inputs(seed: int = 0) -> dict:
    key = jax.random.key(seed)
    ks = jax.random.split(key, 20)
    f32 = jnp.float32
    G, H = GROUP_WIDTH, N_GROUP_HEADS

    def nrm(k, shape, scale):
        return jax.random.normal(k, shape, f32) * scale

    def gain(k, shape):
        return 1.0 + 0.05 * jax.random.normal(k, shape, f32)

    return {
        "x": jax.random.normal(ks[0], (BATCH, SEQ, D_MODEL), f32),
        "g_mix": gain(ks[1], (DEPTH, D_MODEL)),
        "w_in": nrm(ks[2], (DEPTH, D_MODEL, IN_COLS), D_MODEL ** -0.5),
        "w_conv_a": nrm(ks[3], (DEPTH, SHORT_CONV_WIDTH, G), SHORT_CONV_WIDTH ** -0.5),
        "gmlp_ln_g": gain(ks[4], (DEPTH, G)),
        "gmlp_ln_b": nrm(ks[5], (DEPTH, G), 0.02),
        "w_s": nrm(ks[6], (DEPTH, H, CHUNK, CHUNK), 0.5 * CHUNK ** -0.5),
        "b_s": 1.0 + nrm(ks[7], (DEPTH, H, CHUNK), 0.1),
        "w_conf": nrm(ks[8], (DEPTH, CONF_CONV_WIDTH, G), CONF_CONV_WIDTH ** -0.5),
        "b_conf": nrm(ks[9], (DEPTH, G), 0.02),
        "conf_ln_g": gain(ks[10], (DEPTH, G)),
        "conf_ln_b": nrm(ks[11], (DEPTH, G), 0.02),
        "w_out": nrm(ks[12], (DEPTH, MIX_WIDTH, D_MODEL), 0.5 * MIX_WIDTH ** -0.5),
        "g_ffn": gain(ks[13], (DEPTH, D_MODEL)),
        "w_gate": nrm(ks[14], (DEPTH, D_MODEL, FFN_HIDDEN), D_MODEL ** -0.5),
        "w_up": nrm(ks[15], (DEPTH, D_MODEL, FFN_HIDDEN), D_MODEL ** -0.5),
        "w_down": nrm(ks[16], (DEPTH, FFN_HIDDEN, D_MODEL), 0.5 * FFN_HIDDEN ** -0.5),
        "g_final": gain(ks[17], (D_MODEL,)),
    }


def reference(x, g_mix, w_in, w_conv_a, gmlp_ln_g, gmlp_ln_b, w_s, b_s,
              w_conf, b_conf, conf_ln_g, conf_ln_b, w_out,
              g_ffn, w_gate, w_up, w_down, g_final):
    for l in range(DEPTH):
        x = hybrid_layer(x, g_mix[l], w_in[l], w_conv_a[l], gmlp_ln_g[l], gmlp_ln_b[l],
                         w_s[l], b_s[l], w_conf[l], b_conf[l], conf_ln_g[l], conf_ln_b[l],
                         w_out[l], g_ffn[l], w_gate[l], w_up[l], w_down[l])
    return rmsnorm(x, g_final)
```

```python
import functools

import jax
import jax.numpy as jnp
from jax import lax
from jax.experimental import pallas as pl
from jax.experimental.pallas import tpu as pltpu

D_MODEL = 1024
GROUP = 256
HEADS = 4
HEAD_DIM = 64
CHUNK = 128
IDX_HEADS = 4
IDX_DIM = 64
TOPK = 256
SHORT_W = 3
CONF_W = 31
FFN_HIDDEN = 2816
ROPE_THETA = 10000.0
NORM_EPS = 1e-6
LN_EPS = 1e-5
NEG = -1e30

LANES = 128
A_HALO = 8
D_HALO = 32

COL_A = 0
COL_B = COL_A + 3 * GROUP
COL_C = COL_B + 2 * GROUP
COL_I = COL_C + 3 * GROUP
COL_D = COL_I + 4 * LANES
IN_COLS_PACKED = COL_D + 2 * GROUP

MIX_TM = 512
DSA_QB = 256
FFN_TM = 512
FFN_TH = 1408
VMEM_LIMIT = 56 * 1024 * 1024

_NT = (((1,), (1,)), ((), ()))


def _rmsnorm(xf, g):
    ms = jnp.mean(xf * xf, axis=-1, keepdims=True)
    return xf * lax.rsqrt(ms + NORM_EPS) * g


def _layernorm(xf, g, b):
    mu = jnp.mean(xf, axis=-1, keepdims=True)
    xc = xf - mu
    var = jnp.mean(xc * xc, axis=-1, keepdims=True)
    return xc * lax.rsqrt(var + LN_EPS) * g + b


def _rope128(x, cos, sin_signed):
    half = HEAD_DIM // 2
    lane = lax.broadcasted_iota(jnp.int32, x.shape, 1) % HEAD_DIM
    up = pltpu.roll(x, LANES - half, axis=1)
    dn = pltpu.roll(x, half, axis=1)
    return x * cos + jnp.where(lane < half, up, dn) * sin_signed


def _mix_kernel(x_ref, g_ref, w_ref, cos_ref, sin_ref, wca_ref, lngb_ref, lnbb_ref,
                ws_ref, bsb_ref, wcf_ref, bcf_ref, lngd_ref, lnbd_ref,
                yab_ref, yd_ref, q4_ref, k4_ref, v4_ref, qi4_ref, ki_ref, wi_ref,
                abuf, dbuf):
    tm = x_ref.shape[0]
    f32, bf16 = jnp.float32, jnp.bfloat16

    @pl.when(pl.program_id(1) == 0)
    def _():
        abuf[0:A_HALO, :] = jnp.zeros((A_HALO, GROUP), f32)
        dbuf[0:D_HALO, :] = jnp.zeros((D_HALO, GROUP), f32)

    h = _rmsnorm(x_ref[...], g_ref[...]).astype(bf16)

    def proj(lo, hi):
        return jnp.dot(h, w_ref[:, lo:hi], preferred_element_type=f32)

    za = proj(COL_A, COL_B)
    abuf[A_HALO:A_HALO + tm, :] = za[:, GROUP:2 * GROUP] * za[:, 2 * GROUP:3 * GROUP]
    conv = abuf[pl.ds(A_HALO - 2, tm), :] * wca_ref[0:1, :]
    conv = conv + abuf[pl.ds(A_HALO - 1, tm), :] * wca_ref[1:2, :]
    conv = conv + abuf[pl.ds(A_HALO, tm), :] * wca_ref[2:3, :]
    yab_ref[:, 0:GROUP] = (za[:, 0:GROUP] * conv).astype(bf16)
    abuf[0:A_HALO, :] = abuf[tm:tm + A_HALO, :]

    zb = proj(COL_B, COL_C)
    vn = _layernorm(zb[:, GROUP:2 * GROUP], lngb_ref[...], lnbb_ref[...]).astype(bf16)
    r_i = lax.broadcasted_iota(jnp.int32, (CHUNK, CHUNK), 0)
    c_i = lax.broadcasted_iota(jnp.int32, (CHUNK, CHUNK), 1)
    wsm = [jnp.where(r_i >= c_i, ws_ref[hd], 0.0).astype(bf16) for hd in range(HEADS)]
    lane_head = lax.broadcasted_iota(jnp.int32, (CHUNK, GROUP), 1) // HEAD_DIM
    for c in range(tm // CHUNK):
        rows = slice(c * CHUNK, (c + 1) * CHUNK)
        vc = vn[rows, :]
        mixed = bsb_ref[...]
        for hd in range(HEADS):
            full = jnp.dot(wsm[hd], vc, preferred_element_type=f32)
            mixed = mixed + jnp.where(lane_head == hd, full, 0.0)
        yab_ref[rows, GROUP:2 * GROUP] = (zb[rows, 0:GROUP] * mixed).astype(bf16)

    cos = cos_ref[...]
    sin = sin_ref[...]
    zc = proj(COL_C, COL_I)
    scale = HEAD_DIM ** -0.5
    for pair in range(2):
        lo = pair * LANES
        qp = _rope128(zc[:, lo:lo + LANES], cos, sin) * scale
        kp = _rope128(zc[:, GROUP + lo:GROUP + lo + LANES], cos, sin)
        vp = zc[:, 2 * GROUP + lo:2 * GROUP + lo + LANES]
        for sub in range(2):
            hd = 2 * pair + sub
            sl = slice(sub * HEAD_DIM, (sub + 1) * HEAD_DIM)
            q4_ref[hd] = qp[:, sl].astype(bf16)
            k4_ref[hd] = kp[:, sl].astype(bf16)
            v4_ref[hd] = vp[:, sl].astype(bf16)

    zi = proj(COL_I, COL_D)
    for pair in range(2):
        lo = pair * LANES
        qp = _rope128(zi[:, lo:lo + LANES], cos, sin)
        for sub in range(2):
            sl = slice(sub * IDX_DIM, (sub + 1) * IDX_DIM)
            qi4_ref[2 * pair + sub] = qp[:, sl].astype(bf16)
    kp = _rope128(zi[:, 2 * LANES:3 * LANES], cos, sin)
    ki_ref[...] = kp[:, 0:IDX_DIM].astype(bf16)
    wi_ref[...] = zi[:, 3 * LANES:4 * LANES] * (IDX_HEADS ** -0.5 * IDX_DIM ** -0.5)

    zd = proj(COL_D, IN_COLS_PACKED)
    dbuf[D_HALO:D_HALO + tm, :] = zd[:, 0:GROUP] * jax.nn.sigmoid(zd[:, GROUP:2 * GROUP])
    acc = jnp.broadcast_to(bcf_ref[...], (tm, GROUP))
    for t in range(CONF_W):
        acc = acc + dbuf[pl.ds(D_HALO - (CONF_W - 1) + t, tm), :] * wcf_ref[t:t + 1, :]
    y = _layernorm(acc, lngd_ref[...], lnbd_ref[...])
    yd_ref[...] = (y * jax.nn.sigmoid(y)).astype(bf16)
    dbuf[0:D_HALO, :] = dbuf[tm:tm + D_HALO, :]


def _mix_call(x, g, w_packed, cos, sin, wca, lngb, lnbb, ws, bsb, wcf, bcf, lngd, lnbd):
    bsz, seq, _ = x.shape
    n = bsz * seq
    tm = MIX_TM
    nt = seq // tm
    bf16 = jnp.bfloat16
    row = lambda b, i: (b * nt + i, 0)
    row4 = lambda b, i: (0, b * nt + i, 0)
    const2 = lambda b, i: (0, 0)
    const3 = lambda b, i: (0, 0, 0)
    in_specs = [
        pl.BlockSpec((None, tm, D_MODEL), lambda b, i: (b, i, 0)),
        pl.BlockSpec((1, D_MODEL), const2),
        pl.BlockSpec((D_MODEL, IN_COLS_PACKED), const2),
        pl.BlockSpec((tm, LANES), lambda b, i: (i, 0)),
        pl.BlockSpec((tm, LANES), lambda b, i: (i, 0)),
        pl.BlockSpec((SHORT_W, GROUP), const2),
        pl.BlockSpec((1, GROUP), const2),
        pl.BlockSpec((1, GROUP), const2),
        pl.BlockSpec((HEADS, CHUNK, CHUNK), const3),
        pl.BlockSpec((CHUNK, GROUP), const2),
        pl.BlockSpec((CONF_W, GROUP), const2),
        pl.BlockSpec((1, GROUP), const2),
        pl.BlockSpec((1, GROUP), const2),
        pl.BlockSpec((1, GROUP), const2),
    ]
    out_shape = [
        jax.ShapeDtypeStruct((n, 2 * GROUP), bf16),
        jax.ShapeDtypeStruct((n, GROUP), bf16),
        jax.ShapeDtypeStruct((HEADS, n, HEAD_DIM), bf16),
        jax.ShapeDtypeStruct((HEADS, n, HEAD_DIM), bf16),
        jax.ShapeDtypeStruct((HEADS, n, HEAD_DIM), bf16),
        jax.ShapeDtypeStruct((IDX_HEADS, n, IDX_DIM), bf16),
        jax.ShapeDtypeStruct((n, IDX_DIM), bf16),
        jax.ShapeDtypeStruct((n, LANES), jnp.float32),
    ]
    out_specs = [
        pl.BlockSpec((tm, 2 * GROUP), row),
        pl.BlockSpec((tm, GROUP), row),
        pl.BlockSpec((HEADS, tm, HEAD_DIM), row4),
        pl.BlockSpec((HEADS, tm, HEAD_DIM), row4),
        pl.BlockSpec((HEADS, tm, HEAD_DIM), row4),
        pl.BlockSpec((IDX_HEADS, tm, IDX_DIM), row4),
        pl.BlockSpec((tm, IDX_DIM), row),
        pl.BlockSpec((tm, LANES), row),
    ]
    return pl.pallas_call(
        _mix_kernel,
        grid=(bsz, nt),
        in_specs=in_specs,
        out_specs=out_specs,
        out_shape=out_shape,
        scratch_shapes=[pltpu.VMEM((tm + A_HALO, GROUP), jnp.float32),
                        pltpu.VMEM((tm + D_HALO, GROUP), jnp.float32)],
        compiler_params=pltpu.CompilerParams(
            dimension_semantics=("arbitrary", "arbitrary"),
            vmem_limit_bytes=VMEM_LIMIT),
        name="mix",
    )(x, g, w_packed, cos, sin, wca, lngb, lnbb, ws, bsb, wcf, bcf, lngd, lnbd)


def _threshold_bits(trial):
    bits = jnp.where(trial < 0, trial & jnp.int32(0x7FFFFFFF), ~trial)
    return pltpu.bitcast(bits, jnp.float32)


def _dsa_kernel(qi4_ref, wi_ref, q4_ref, ki_ref, k4_ref, v4_ref, o_ref,
                sc_ref, vs_ref, j_ref, m_ref, l_ref, acc_ref, *, seq):
    f32, bf16, i32 = jnp.float32, jnp.bfloat16, jnp.int32
    qb = wi_ref.shape[0]
    kc = qb
    rg = qb // 2
    i = pl.program_id(1)
    nchunks = i + 1
    n_outside = seq - nchunks * kc
    row = lax.broadcasted_iota(i32, (qb, kc), 0)
    col = lax.broadcasted_iota(i32, (qb, kc), 1)

    def score_chunk(c, diagonal):
        off = pl.multiple_of(c * kc, kc)
        kic = ki_ref[pl.ds(off, kc), :]
        sc = jnp.zeros((qb, kc), f32)
        for j in range(IDX_HEADS):
            rel = lax.dot_general(qi4_ref[j], kic, _NT, preferred_element_type=f32)
            sc = sc + jnp.maximum(rel, 0.0) * wi_ref[:, j:j + 1]
        if diagonal:
            sc = jnp.where(col <= row, sc, NEG)
        sc_ref[:, pl.ds(off, kc)] = sc

    def score_body(c, carry):
        score_chunk(c, False)
        return carry

    lax.fori_loop(0, i, score_body, 0)
    score_chunk(i, True)

    for g in range(qb // rg):
        rows = slice(g * rg, (g + 1) * rg)

        def count(indicator):
            def body(c, acc):
                off = pl.multiple_of(c * kc, kc)
                ind = indicator(sc_ref[rows, pl.ds(off, kc)], off)
                for t in range(kc // LANES):
                    acc = acc + ind[:, t * LANES:(t + 1) * LANES]
                return acc
            acc = lax.fori_loop(0, nchunks, body, jnp.zeros((rg, LANES), i32))
            return jnp.sum(acc, axis=1, keepdims=True)

        def count_ge(thr):
            inside = count(lambda s, off: jnp.where(s >= thr, 1, 0))
            return inside + jnp.where(thr <= NEG, n_outside, 0)

        def bit_step(step, prefix):
            trial = prefix | jnp.left_shift(jnp.int32(1), 31 - step)
            cnt = count_ge(_threshold_bits(trial))
            return jnp.where(cnt >= TOPK, trial, prefix)

        prefix = lax.fori_loop(0, 32, bit_step, jnp.zeros((rg, 1), i32))
        vstar = _threshold_bits(prefix)
        c_ge = count_ge(vstar)
        vs_ref[rows, :] = jnp.broadcast_to(vstar, (rg, LANES))
        j_ref[rows, :] = jnp.full((rg, LANES), seq, i32)

        @pl.when(jnp.max(c_ge) > TOPK)
        def _():
            c_gt = (count(lambda s, off: jnp.where(s > vstar, 1, 0))
                    + jnp.where(vstar < NEG, n_outside, 0))
            want = TOPK - c_gt

            def idx_step(step, bound):
                trial = bound | jnp.left_shift(jnp.int32(1), 11 - step)
                lane_lim = trial - lax.broadcasted_iota(i32, (rg, kc), 1)
                cnt = count(lambda s, off: jnp.where(
                    s == vstar, jnp.where(off < lane_lim, 1, 0), 0))
                return jnp.where(cnt < want, trial, bound)

            bound = lax.fori_loop(0, 12, idx_step, jnp.zeros((rg, 1), i32))
            j_ref[rows, :] = jnp.broadcast_to(bound, (rg, LANES))

    m_ref[...] = jnp.full(m_ref.shape, NEG, f32)
    l_ref[...] = jnp.zeros(l_ref.shape, f32)
    acc_ref[...] = jnp.zeros(acc_ref.shape, f32)
    reps = kc // LANES

    def attend_chunk(c, diagonal):
        off = pl.multiple_of(c * kc, kc)
        sc = sc_ref[:, pl.ds(off, kc)]
        vs = jnp.tile(vs_ref[...], (1, reps))
        jb = jnp.tile(j_ref[...], (1, reps))
        sel = (sc > vs) | ((sc == vs) & (col + off <= jb))
        if diagonal:
            sel = sel & (col <= row)
        for hd in range(HEADS):
            s = lax.dot_general(q4_ref[hd], k4_ref[hd, pl.ds(off, kc), :], _NT,
                                preferred_element_type=f32)
            s = jnp.where(sel, s, NEG)
            m_prev = m_ref[hd]
            m_next = jnp.maximum(m_prev, jnp.max(s, axis=1, keepdims=True))
            alpha = jnp.exp(m_prev - m_next)
            p = jnp.exp(s - jnp.tile(m_next, (1, reps)))
            l_ref[hd] = alpha * l_ref[hd] + jnp.sum(p, axis=1, keepdims=True)
            pv = jnp.dot(p.astype(bf16), v4_ref[hd, pl.ds(off, kc), :],
                         preferred_element_type=f32)
            acc_ref[hd] = acc_ref[hd] * alpha[:, 0:HEAD_DIM] + pv
            m_ref[hd] = m_next

    def attend_body(c, carry):
        attend_chunk(c, False)
        return carry

    lax.fori_loop(0, i, attend_body, 0)
    attend_chunk(i, True)

    outs = [acc_ref[hd] / l_ref[hd][:, 0:HEAD_DIM] for hd in range(HEADS)]
    o_ref[...] = jnp.concatenate(outs, axis=1).astype(o_ref.dtype)


def _dsa_call(qi4, wi, q4, ki, k4, v4, bsz, seq):
    n = bsz * seq
    qb = DSA_QB
    nq = seq // qb
    qrow4 = lambda b, i: (0, b * nq + i, 0)
    return pl.pallas_call(
        functools.partial(_dsa_kernel, seq=seq),
        grid=(bsz, nq),
        in_specs=[
            pl.BlockSpec((IDX_HEADS, qb, IDX_DIM), qrow4),
            pl.BlockSpec((qb, LANES), lambda b, i: (b * nq + i, 0)),
            pl.BlockSpec((HEADS, qb, HEAD_DIM), qrow4),
            pl.BlockSpec((seq, IDX_DIM), lambda b, i: (b, 0)),
            pl.BlockSpec((HEADS, seq, HEAD_DIM), lambda b, i: (0, b, 0)),
            pl.BlockSpec((HEADS, seq, HEAD_DIM), lambda b, i: (0, b, 0)),
        ],
        out_specs=pl.BlockSpec((qb, GROUP), lambda b, i: (b * nq + i, 0)),
        out_shape=jax.ShapeDtypeStruct((n, GROUP), jnp.bfloat16),
        scratch_shapes=[
            pltpu.VMEM((qb, seq), jnp.float32),
            pltpu.VMEM((qb, LANES), jnp.float32),
            pltpu.VMEM((qb, LANES), jnp.int32),
            pltpu.VMEM((HEADS, qb, LANES), jnp.float32),
            pltpu.VMEM((HEADS, qb, LANES), jnp.float32),
            pltpu.VMEM((HEADS, qb, HEAD_DIM), jnp.float32),
        ],
        compiler_params=pltpu.CompilerParams(
            dimension_semantics=("arbitrary", "arbitrary"),
            vmem_limit_bytes=VMEM_LIMIT),
        name="dsa",
    )(qi4, wi, q4, ki, k4, v4)


def _out_ffn_kernel(x_ref, yab_ref, yc_ref, yd_ref, wout_ref, gffn_ref, wg_ref, wu_ref,
                    wd_ref, gfin_ref, o_ref, acc_sc, hf_sc, *, final):
    f32, bf16 = jnp.float32, jnp.bfloat16
    j = pl.program_id(1)

    @pl.when(j == 0)
    def _():
        y = jnp.concatenate([yab_ref[...], yc_ref[...], yd_ref[...]], axis=1)
        x1 = x_ref[...] + jnp.dot(y, wout_ref[...], preferred_element_type=f32)
        acc_sc[...] = x1
        hf_sc[...] = _rmsnorm(x1, gffn_ref[...]).astype(bf16)

    hf = hf_sc[...]
    gate = jnp.dot(hf, wg_ref[...], preferred_element_type=f32)
    up = jnp.dot(hf, wu_ref[...], preferred_element_type=f32)
    act = (gate * jax.nn.sigmoid(gate) * up).astype(bf16)
    acc_sc[...] += jnp.dot(act, wd_ref[...], preferred_element_type=f32)

    @pl.when(j == pl.num_programs(1) - 1)
    def _():
        out = acc_sc[...]
        if final:
            out = _rmsnorm(out, gfin_ref[...])
        o_ref[...] = out


def _out_ffn_call(x2d, yab, yc, yd, wout, gffn, wg, wu, wd, gfin, final):
    n = x2d.shape[0]
    tm, th = FFN_TM, FFN_TH
    row = lambda i, j: (i, 0)
    const = lambda i, j: (0, 0)
    return pl.pallas_call(
        functools.partial(_out_ffn_kernel, final=final),
        grid=(n // tm, FFN_HIDDEN // th),
        in_specs=[
            pl.BlockSpec((tm, D_MODEL), row),
            pl.BlockSpec((tm, 2 * GROUP), row),
            pl.BlockSpec((tm, GROUP), row),
            pl.BlockSpec((tm, GROUP), row),
            pl.BlockSpec((D_MODEL, D_MODEL), const),
            pl.BlockSpec((1, D_MODEL), const),
            pl.BlockSpec((D_MODEL, th), lambda i, j: (0, j)),
            pl.BlockSpec((D_MODEL, th), lambda i, j: (0, j)),
            pl.BlockSpec((th, D_MODEL), lambda i, j: (j, 0)),
            pl.BlockSpec((1, D_MODEL), const),
        ],
        out_specs=pl.BlockSpec((tm, D_MODEL), row),
        out_shape=jax.ShapeDtypeStruct((n, D_MODEL), jnp.float32),
        scratch_shapes=[pltpu.VMEM((tm, D_MODEL), jnp.float32),
                        pltpu.VMEM((tm, D_MODEL), jnp.bfloat16)],
        compiler_params=pltpu.CompilerParams(
            dimension_semantics=("arbitrary", "arbitrary"),
            vmem_limit_bytes=VMEM_LIMIT),
        name="out_ffn",
    )(x2d, yab, yc, yd, wout, gffn, wg, wu, wd, gfin)


def _pack_w_in(w):
    g = GROUP
    cuts = [0, 3 * g, 5 * g, 8 * g, 9 * g, 9 * g + IDX_DIM, 9 * g + IDX_DIM + IDX_HEADS,
            11 * g + IDX_DIM + IDX_HEADS]
    a, b, c, qi, ki, wi, d = [w[:, cuts[t]:cuts[t + 1]] for t in range(7)]
    z = lambda k: jnp.zeros((w.shape[0], k), w.dtype)
    packed = jnp.concatenate(
        [a, b, c, qi, ki, z(LANES - IDX_DIM), wi, z(LANES - IDX_HEADS), d], axis=1)
    return packed.astype(jnp.bfloat16)


def _rope_tables(seq):
    inv_freq = ROPE_THETA ** (-jnp.arange(0, HEAD_DIM, 2, dtype=jnp.float32) / HEAD_DIM)
    ang = jnp.arange(seq, dtype=jnp.float32)[:, None] * inv_freq[None, :]
    c, s = jnp.cos(ang), jnp.sin(ang)
    reps = LANES // HEAD_DIM
    cos = jnp.tile(jnp.concatenate([c, c], axis=1), (1, reps))
    sin = jnp.tile(jnp.concatenate([-s, s], axis=1), (1, reps))
    return cos, sin


def kernel(x, g_mix, w_in, w_conv_a, gmlp_ln_g, gmlp_ln_b, w_s, b_s, w_conf, b_conf,
           conf_ln_g, conf_ln_b, w_out, g_ffn, w_gate, w_up, w_down, g_final):
    bsz, seq, d = x.shape
    depth = w_in.shape[0]
    bf16 = jnp.bfloat16
    cos, sin = _rope_tables(seq)
    xs = x
    for l in range(depth):
        yab, yd, q4, k4, v4, qi4, ki, wi = _mix_call(
            xs.reshape(bsz, seq, d), g_mix[l][None, :], _pack_w_in(w_in[l]), cos, sin,
            w_conv_a[l], gmlp_ln_g[l][None, :], gmlp_ln_b[l][None, :], w_s[l],
            jnp.repeat(b_s[l].T, HEAD_DIM, axis=1), w_conf[l], b_conf[l][None, :],
            conf_ln_g[l][None, :], conf_ln_b[l][None, :])
        yc = _dsa_call(qi4, wi, q4, ki, k4, v4, bsz, seq)
        xs = _out_ffn_call(
            xs.reshape(bsz * seq, d), yab, yc, yd, w_out[l].astype(bf16), g_ffn[l][None, :],
            w_gate[l].astype(bf16), w_up[l].astype(bf16), w_down[l].astype(bf16),
            g_final[None, :], final=(l == depth - 1))
    return xs.reshape(bsz, seq, d)
```

```python
import functools
import math

import jax
import jax.numpy as jnp
from jax import lax
from jax.experimental import pallas as pl
from jax.experimental.pallas import tpu as pltpu

D_MODEL = 1024
GROUP = 256
HEADS = 4
HEAD_DIM = 64
CHUNK = 128
IDX_HEADS = 4
IDX_DIM = 64
TOPK = 256
SHORT_W = 3
CONF_W = 31
FFN_HIDDEN = 2816
ROPE_THETA = 10000.0
NORM_EPS = 1e-6
LN_EPS = 1e-5
NEG = -1e30
BELOW_NEG = -3.0e38

LANES = 128
SUBLANES = 8
A_HALO = 8
D_HALO = 32

COL_A = 0
COL_B = COL_A + 3 * GROUP
COL_K = COL_B + 2 * GROUP
COL_KI = COL_K + GROUP
COL_D = COL_KI + LANES
ROW_COLS = COL_D + 2 * GROUP
TROW_Q = 0
TROW_V = TROW_Q + GROUP
TROW_QI = TROW_V + GROUP
TROW_WI = TROW_QI + IDX_HEADS * IDX_DIM
WI_ROWS = 16
TROWS = TROW_WI + WI_ROWS

MIX_TM = 512
DSA_QB = 256
DSA_SUB = 128
CNT_ROWS = 32
FFN_TM = 512
FFN_TH = 1408
VMEM_LIMIT = 56 * 1024 * 1024

_NT = (((1,), (1,)), ((), ()))


def _rmsnorm(xf, g):
    ms = jnp.mean(xf * xf, axis=-1, keepdims=True)
    return xf * lax.rsqrt(ms + NORM_EPS) * g


def _layernorm(xf, g, b):
    mu = jnp.mean(xf, axis=-1, keepdims=True)
    xc = xf - mu
    var = jnp.mean(xc * xc, axis=-1, keepdims=True)
    return xc * lax.rsqrt(var + LN_EPS) * g + b


def _rope128(x, cos, sin_signed):
    half = HEAD_DIM // 2
    lane = lax.broadcasted_iota(jnp.int32, x.shape, 1) % HEAD_DIM
    up = pltpu.roll(x, LANES - half, axis=1)
    dn = pltpu.roll(x, half, axis=1)
    return x * cos + jnp.where(lane < half, up, dn) * sin_signed


def _rope_t(x, cos_t, sin_t):
    half = HEAD_DIM // 2
    x1, x2 = x[0:half, :], x[half:HEAD_DIM, :]
    return x1 * cos_t - x2 * sin_t, x2 * cos_t + x1 * sin_t


def _mix_kernel(x_ref, g_ref, w_ref, wt_ref, cos_ref, sin_ref, cost_ref, sint_ref,
                wca_ref, lngb_ref, lnbb_ref, ws_ref, bsb_ref, wcf_ref, bcf_ref,
                lngd_ref, lnbd_ref,
                yab_ref, yd_ref, k4_ref, ki_ref, qt_ref, vt_ref, qit_ref, wit_ref,
                abuf, dbuf):
    tm = x_ref.shape[0]
    f32, bf16 = jnp.float32, jnp.bfloat16
    half = HEAD_DIM // 2

    @pl.when(pl.program_id(1) == 0)
    def _():
        abuf[0:A_HALO, :] = jnp.zeros((A_HALO, GROUP), f32)
        dbuf[0:D_HALO, :] = jnp.zeros((D_HALO, GROUP), f32)

    h = _rmsnorm(x_ref[...], g_ref[...]).astype(bf16)

    def proj(lo, hi):
        return jnp.dot(h, w_ref[:, lo:hi], preferred_element_type=f32)

    def proj_t(lo, hi):
        return lax.dot_general(wt_ref[lo:hi, :], h, _NT, preferred_element_type=f32)

    za = proj(COL_A, COL_B)
    abuf[A_HALO:A_HALO + tm, :] = za[:, GROUP:2 * GROUP] * za[:, 2 * GROUP:3 * GROUP]
    conv = abuf[pl.ds(A_HALO - 2, tm), :] * wca_ref[0:1, :]
    conv = conv + abuf[pl.ds(A_HALO - 1, tm), :] * wca_ref[1:2, :]
    conv = conv + abuf[pl.ds(A_HALO, tm), :] * wca_ref[2:3, :]
    yab_ref[:, 0:GROUP] = (za[:, 0:GROUP] * conv).astype(bf16)
    abuf[0:A_HALO, :] = abuf[tm:tm + A_HALO, :]

    zb = proj(COL_B, COL_K)
    vn = _layernorm(zb[:, GROUP:2 * GROUP], lngb_ref[...], lnbb_ref[...]).astype(bf16)
    r_i = lax.broadcasted_iota(jnp.int32, (CHUNK, CHUNK), 0)
    c_i = lax.broadcasted_iota(jnp.int32, (CHUNK, CHUNK), 1)
    wsm = [jnp.where(r_i >= c_i, ws_ref[hd], 0.0).astype(bf16) for hd in range(HEADS)]
    lane_head = lax.broadcasted_iota(jnp.int32, (CHUNK, GROUP), 1) // HEAD_DIM
    for c in range(tm // CHUNK):
        rows = slice(c * CHUNK, (c + 1) * CHUNK)
        vc = vn[rows, :]
        mixed = bsb_ref[...]
        for hd in range(HEADS):
            full = jnp.dot(wsm[hd], vc, preferred_element_type=f32)
            mixed = mixed + jnp.where(lane_head == hd, full, 0.0)
        yab_ref[rows, GROUP:2 * GROUP] = (zb[rows, 0:GROUP] * mixed).astype(bf16)

    cos = cos_ref[...]
    sin = sin_ref[...]
    zk = proj(COL_K, COL_D)
    for pair in range(2):
        kp = _rope128(zk[:, pair * LANES:(pair + 1) * LANES], cos, sin)
        for sub in range(2):
            k4_ref[2 * pair + sub] = kp[:, sub * HEAD_DIM:(sub + 1) * HEAD_DIM].astype(bf16)
    kip = _rope128(zk[:, GROUP:GROUP + LANES], cos, sin)
    ki_ref[...] = kip[:, 0:IDX_DIM].astype(bf16)

    cos_t = cost_ref[...]
    sin_t = sint_ref[...]
    q_scale = HEAD_DIM ** -0.5 * math.log2(math.e)
    zq = proj_t(TROW_Q, TROW_V)
    zqi = proj_t(TROW_QI, TROW_WI)
    for hd in range(HEADS):
        rows = slice(hd * HEAD_DIM, (hd + 1) * HEAD_DIM)
        o1, o2 = _rope_t(zq[rows, :], cos_t, sin_t)
        qt_ref[hd, 0:half, :] = (o1 * q_scale).astype(bf16)
        qt_ref[hd, half:HEAD_DIM, :] = (o2 * q_scale).astype(bf16)
        o1, o2 = _rope_t(zqi[rows, :], cos_t, sin_t)
        qit_ref[hd, 0:half, :] = o1.astype(bf16)
        qit_ref[hd, half:HEAD_DIM, :] = o2.astype(bf16)
    zv = proj_t(TROW_V, TROW_QI)
    for hd in range(HEADS):
        vt_ref[hd] = zv[hd * HEAD_DIM:(hd + 1) * HEAD_DIM, :].astype(bf16)
    zw = proj_t(TROW_WI, TROWS)
    wit_ref[...] = zw[0:SUBLANES, :] * (IDX_HEADS ** -0.5 * IDX_DIM ** -0.5)

    zd = proj(COL_D, ROW_COLS)
    dbuf[D_HALO:D_HALO + tm, :] = zd[:, 0:GROUP] * jax.nn.sigmoid(zd[:, GROUP:2 * GROUP])
    acc = jnp.broadcast_to(bcf_ref[...], (tm, GROUP))
    for t in range(CONF_W):
        acc = acc + dbuf[pl.ds(D_HALO - (CONF_W - 1) + t, tm), :] * wcf_ref[t:t + 1, :]
    y = _layernorm(acc, lngd_ref[...], lnbd_ref[...])
    yd_ref[...] = (y * jax.nn.sigmoid(y)).astype(bf16)
    dbuf[0:D_HALO, :] = dbuf[tm:tm + D_HALO, :]


def _mix_call(x, g, w_rows, w_t, tables, wca, lngb, lnbb, ws, bsb, wcf, bcf, lngd, lnbd):
    bsz, seq, _ = x.shape
    n = bsz * seq
    tm = MIX_TM
    nt = seq // tm
    bf16 = jnp.bfloat16
    half = HEAD_DIM // 2
    cos, sin, cos_t, sin_t = tables
    row = lambda b, i: (b * nt + i, 0)
    row4 = lambda b, i: (0, b * nt + i, 0)
    col = lambda b, i: (0, b * nt + i)
    col4 = lambda b, i: (0, 0, b * nt + i)
    const2 = lambda b, i: (0, 0)
    const3 = lambda b, i: (0, 0, 0)
    in_specs = [
        pl.BlockSpec((None, tm, D_MODEL), lambda b, i: (b, i, 0)),
        pl.BlockSpec((1, D_MODEL), const2),
        pl.BlockSpec((D_MODEL, ROW_COLS), const2),
        pl.BlockSpec((TROWS, D_MODEL), const2),
        pl.BlockSpec((tm, LANES), lambda b, i: (i, 0)),
        pl.BlockSpec((tm, LANES), lambda b, i: (i, 0)),
        pl.BlockSpec((half, tm), lambda b, i: (0, i)),
        pl.BlockSpec((half, tm), lambda b, i: (0, i)),
        pl.BlockSpec((SHORT_W, GROUP), const2),
        pl.BlockSpec((1, GROUP), const2),
        pl.BlockSpec((1, GROUP), const2),
        pl.BlockSpec((HEADS, CHUNK, CHUNK), const3),
        pl.BlockSpec((CHUNK, GROUP), const2),
        pl.BlockSpec((CONF_W, GROUP), const2),
        pl.BlockSpec((1, GROUP), const2),
        pl.BlockSpec((1, GROUP), const2),
        pl.BlockSpec((1, GROUP), const2),
    ]
    out_shape = [
        jax.ShapeDtypeStruct((n, 2 * GROUP), bf16),
        jax.ShapeDtypeStruct((n, GROUP), bf16),
        jax.ShapeDtypeStruct((HEADS, n, HEAD_DIM), bf16),
        jax.ShapeDtypeStruct((n, IDX_DIM), bf16),
        jax.ShapeDtypeStruct((HEADS, HEAD_DIM, n), bf16),
        jax.ShapeDtypeStruct((HEADS, HEAD_DIM, n), bf16),
        jax.ShapeDtypeStruct((IDX_HEADS, IDX_DIM, n), bf16),
        jax.ShapeDtypeStruct((SUBLANES, n), jnp.float32),
    ]
    out_specs = [
        pl.BlockSpec((tm, 2 * GROUP), row),
        pl.BlockSpec((tm, GROUP), row),
        pl.BlockSpec((HEADS, tm, HEAD_DIM), row4),
        pl.BlockSpec((tm, IDX_DIM), row),
        pl.BlockSpec((HEADS, HEAD_DIM, tm), col4),
        pl.BlockSpec((HEADS, HEAD_DIM, tm), col4),
        pl.BlockSpec((IDX_HEADS, IDX_DIM, tm), col4),
        pl.BlockSpec((SUBLANES, tm), col),
    ]
    return pl.pallas_call(
        _mix_kernel,
        grid=(bsz, nt),
        in_specs=in_specs,
        out_specs=out_specs,
        out_shape=out_shape,
        scratch_shapes=[pltpu.VMEM((tm + A_HALO, GROUP), jnp.float32),
                        pltpu.VMEM((tm + D_HALO, GROUP), jnp.float32)],
        compiler_params=pltpu.CompilerParams(
            dimension_semantics=("arbitrary", "arbitrary"),
            vmem_limit_bytes=VMEM_LIMIT),
        name="mix",
    )(x, g, w_rows, w_t, cos, sin, cos_t, sin_t, wca, lngb, lnbb, ws, bsb, wcf, bcf,
      lngd, lnbd)


def _threshold_bits(trial):
    bits = jnp.where(trial < 0, trial & jnp.int32(0x7FFFFFFF), ~trial)
    return pltpu.bitcast(bits, jnp.float32)


def _dsa_kernel(qit_ref, wit_ref, qt_ref, ki_ref, k4_ref, vt_ref, o_ref,
                sc_ref, acc_ref, *, seq):
    f32, bf16, i32 = jnp.float32, jnp.bfloat16, jnp.int32
    qb = wit_ref.shape[1]
    kc = qb
    i = pl.program_id(1)
    nchunks = i + 1
    n_outside = seq - nchunks * kc
    sub = DSA_SUB
    key_i = lax.broadcasted_iota(i32, (sub, qb), 0)
    qry_i = lax.broadcasted_iota(i32, (sub, qb), 1)
    cnt_i = lax.broadcasted_iota(i32, (CNT_ROWS, qb), 0)

    def score_chunk(c, diagonal):
        off = pl.multiple_of(c * kc, kc)
        for t in range(kc // sub):
            r0 = off + t * sub
            kis = ki_ref[pl.ds(r0, sub), :]
            sc = None
            for j in range(IDX_HEADS):
                rel = jnp.dot(kis, qit_ref[j], preferred_element_type=f32)
                term = jnp.maximum(rel, 0.0) * wit_ref[j:j + 1, :]
                sc = term if sc is None else sc + term
            if diagonal:
                sc = jnp.where(key_i + t * sub <= qry_i, sc, NEG)
            sc_ref[pl.ds(r0, sub), :] = sc

    def score_body(c, carry):
        score_chunk(c, False)
        return carry

    lax.fori_loop(0, i, score_body, 0)
    score_chunk(i, True)

    def count(indicator):
        def body(c, acc):
            off = pl.multiple_of(c * kc, kc)
            for t in range(kc // CNT_ROWS):
                r0 = off + t * CNT_ROWS
                acc = acc + indicator(sc_ref[pl.ds(r0, CNT_ROWS), :], r0)
            return acc
        acc = lax.fori_loop(0, nchunks, body, jnp.zeros((CNT_ROWS, qb), i32))
        return jnp.sum(acc, axis=0, keepdims=True)

    def count_ge(thr):
        inside = count(lambda s, r0: jnp.where(s >= thr, 1, 0))
        return inside + jnp.where(thr <= NEG, n_outside, 0)

    def bit_step(step, prefix):
        trial = prefix | jnp.left_shift(jnp.int32(1), 31 - step)
        cnt = count_ge(_threshold_bits(trial))
        return jnp.where(cnt >= TOPK, trial, prefix)

    prefix = lax.fori_loop(0, 32, bit_step, jnp.zeros((1, qb), i32))
    vstar = _threshold_bits(prefix)
    c_ge = count_ge(vstar)

    @pl.when(jnp.max(c_ge) > TOPK)
    def _():
        c_gt = (count(lambda s, r0: jnp.where(s > vstar, 1, 0))
                + jnp.where(vstar < NEG, n_outside, 0))
        want = TOPK - c_gt

        def idx_step(step, bound):
            trial = bound | jnp.left_shift(jnp.int32(1), 11 - step)
            cnt = count(lambda s, r0: jnp.where(
                s == vstar, jnp.where(cnt_i + r0 < trial, 1, 0), 0))
            return jnp.where(cnt < want, trial, bound)

        bound = lax.fori_loop(0, 12, idx_step, jnp.zeros((1, qb), i32))

        def drop_body(c, carry):
            off = pl.multiple_of(c * sub, sub)
            s = sc_ref[pl.ds(off, sub), :]
            drop = jnp.where(s == vstar, jnp.where(key_i + off > bound, 1, 0), 0)
            sc_ref[pl.ds(off, sub), :] = jnp.where(drop > 0, BELOW_NEG, s)
            return carry

        lax.fori_loop(0, nchunks * (kc // sub), drop_body, 0)

    def masked_logits(hd, r0, t, diagonal):
        s = jnp.dot(k4_ref[hd, pl.ds(r0, sub), :], qt_ref[hd], preferred_element_type=f32)
        s = jnp.where(sc_ref[pl.ds(r0, sub), :] >= vstar, s, NEG)
        if diagonal:
            s = jnp.where(key_i + t * sub <= qry_i, s, NEG)
        return s

    def fold8(x, op):
        return op(x.reshape(sub // SUBLANES, SUBLANES, qb), axis=0)

    def max_chunk(c, diagonal, mparts):
        off = pl.multiple_of(c * kc, kc)
        mparts = list(mparts)
        for t in range(kc // sub):
            for hd in range(HEADS):
                s = masked_logits(hd, off + t * sub, t, diagonal)
                mparts[hd] = jnp.maximum(mparts[hd], fold8(s, jnp.max))
        return tuple(mparts)

    mparts = tuple(jnp.full((SUBLANES, qb), NEG, f32) for _ in range(HEADS))
    mparts = lax.fori_loop(0, i, lambda c, cr: max_chunk(c, False, cr), mparts)
    mparts = max_chunk(i, True, mparts)
    ms = [jnp.max(mp, axis=0, keepdims=True) for mp in mparts]

    acc_ref[...] = jnp.zeros(acc_ref.shape, f32)

    def pv_chunk(c, diagonal, lparts):
        off = pl.multiple_of(c * kc, kc)
        lparts = list(lparts)
        pvs = [None] * HEADS
        for t in range(kc // sub):
            r0 = off + t * sub
            for hd in range(HEADS):
                p = jnp.exp2(masked_logits(hd, r0, t, diagonal) - ms[hd])
                lparts[hd] = lparts[hd] + fold8(p, jnp.sum)
                pv = jnp.dot(vt_ref[hd, :, pl.ds(r0, sub)], p.astype(bf16),
                             preferred_element_type=f32)
                pvs[hd] = pv if pvs[hd] is None else pvs[hd] + pv
        for hd in range(HEADS):
            acc_ref[hd] += pvs[hd]
        return tuple(lparts)

    lparts = tuple(jnp.zeros((SUBLANES, qb), f32) for _ in range(HEADS))
    lparts = lax.fori_loop(0, i, lambda c, cr: pv_chunk(c, False, cr), lparts)
    lparts = pv_chunk(i, True, lparts)

    out_t = jnp.concatenate(
        [acc_ref[hd] / jnp.sum(lparts[hd], axis=0, keepdims=True) for hd in range(HEADS)],
        axis=0)
    o_ref[...] = out_t.T.astype(o_ref.dtype)


def _dsa_call(qit, wit, qt, ki, k4, vt, bsz, seq):
    n = bsz * seq
    qb = DSA_QB
    nq = seq // qb
    qcol4 = lambda b, i: (0, 0, b * nq + i)
    return pl.pallas_call(
        functools.partial(_dsa_kernel, seq=seq),
        grid=(bsz, nq),
        in_specs=[
            pl.BlockSpec((IDX_HEADS, IDX_DIM, qb), qcol4),
            pl.BlockSpec((SUBLANES, qb), lambda b, i: (0, b * nq + i)),
            pl.BlockSpec((HEADS, HEAD_DIM, qb), qcol4),
            pl.BlockSpec((seq, IDX_DIM), lambda b, i: (b, 0)),
            pl.BlockSpec((HEADS, seq, HEAD_DIM), lambda b, i: (0, b, 0)),
            pl.BlockSpec((HEADS, HEAD_DIM, seq), lambda b, i: (0, 0, b)),
        ],
        out_specs=pl.BlockSpec((qb, GROUP), lambda b, i: (b * nq + i, 0)),
        out_shape=jax.ShapeDtypeStruct((n, GROUP), jnp.bfloat16),
        scratch_shapes=[
            pltpu.VMEM((seq, qb), jnp.float32),
            pltpu.VMEM((HEADS, HEAD_DIM, qb), jnp.float32),
        ],
        compiler_params=pltpu.CompilerParams(
            dimension_semantics=("arbitrary", "arbitrary"),
            vmem_limit_bytes=VMEM_LIMIT),
        name="dsa",
    )(qit, wit, qt, ki, k4, vt)


def _out_ffn_kernel(x_ref, yab_ref, yc_ref, yd_ref, wout_ref, gffn_ref, wg_ref, wu_ref,
                    wd_ref, gfin_ref, o_ref, acc_sc, hf_sc, *, final):
    f32, bf16 = jnp.float32, jnp.bfloat16
    j = pl.program_id(1)

    @pl.when(j == 0)
    def _():
        y = jnp.concatenate([yab_ref[...], yc_ref[...], yd_ref[...]], axis=1)
        x1 = x_ref[...] + jnp.dot(y, wout_ref[...], preferred_element_type=f32)
        acc_sc[...] = x1
        hf_sc[...] = _rmsnorm(x1, gffn_ref[...]).astype(bf16)

    hf = hf_sc[...]
    gate = jnp.dot(hf, wg_ref[...], preferred_element_type=f32)
    up = jnp.dot(hf, wu_ref[...], preferred_element_type=f32)
    act = (gate * jax.nn.sigmoid(gate) * up).astype(bf16)
    acc_sc[...] += jnp.dot(act, wd_ref[...], preferred_element_type=f32)

    @pl.when(j == pl.num_programs(1) - 1)
    def _():
        out = acc_sc[...]
        if final:
            out = _rmsnorm(out, gfin_ref[...])
        o_ref[...] = out


def _out_ffn_call(x2d, yab, yc, yd, wout, gffn, wg, wu, wd, gfin, final):
    n = x2d.shape[0]
    tm, th = FFN_TM, FFN_TH
    row = lambda i, j: (i, 0)
    const = lambda i, j: (0, 0)
    return pl.pallas_call(
        functools.partial(_out_ffn_kernel, final=final),
        grid=(n // tm, FFN_HIDDEN // th),
        in_specs=[
            pl.BlockSpec((tm, D_MODEL), row),
            pl.BlockSpec((tm, 2 * GROUP), row),
            pl.BlockSpec((tm, GROUP), row),
            pl.BlockSpec((tm, GROUP), row),
            pl.BlockSpec((D_MODEL, D_MODEL), const),
            pl.BlockSpec((1, D_MODEL), const),
            pl.BlockSpec((D_MODEL, th), lambda i, j: (0, j)),
            pl.BlockSpec((D_MODEL, th), lambda i, j: (0, j)),
            pl.BlockSpec((th, D_MODEL), lambda i, j: (j, 0)),
            pl.BlockSpec((1, D_MODEL), const),
        ],
        out_specs=pl.BlockSpec((tm, D_MODEL), row),
        out_shape=jax.ShapeDtypeStruct((n, D_MODEL), jnp.float32),
        scratch_shapes=[pltpu.VMEM((tm, D_MODEL), jnp.float32),
                        pltpu.VMEM((tm, D_MODEL), jnp.bfloat16)],
        compiler_params=pltpu.CompilerParams(
            dimension_semantics=("arbitrary", "arbitrary"),
            vmem_limit_bytes=VMEM_LIMIT),
        name="out_ffn",
    )(x2d, yab, yc, yd, wout, gffn, wg, wu, wd, gfin)


def _pack_w_in(w):
    g = GROUP
    cuts = [0, 3 * g, 5 * g, 6 * g, 7 * g, 8 * g, 9 * g, 9 * g + IDX_DIM,
            9 * g + IDX_DIM + IDX_HEADS, 11 * g + IDX_DIM + IDX_HEADS]
    a, b, q, k, v, qi, ki, wi, d = [w[:, cuts[t]:cuts[t + 1]] for t in range(9)]
    z = lambda cols: jnp.zeros((w.shape[0], cols), w.dtype)
    rows = jnp.concatenate([a, b, k, ki, z(LANES - IDX_DIM), d], axis=1)
    trans = jnp.concatenate([q, v, qi, wi, z(WI_ROWS - IDX_HEADS)], axis=1).T
    return rows.astype(jnp.bfloat16), trans.astype(jnp.bfloat16)


def _rope_tables(seq):
    inv_freq = ROPE_THETA ** (-jnp.arange(0, HEAD_DIM, 2, dtype=jnp.float32) / HEAD_DIM)
    ang = jnp.arange(seq, dtype=jnp.float32)[:, None] * inv_freq[None, :]
    c, s = jnp.cos(ang), jnp.sin(ang)
    reps = LANES // HEAD_DIM
    cos = jnp.tile(jnp.concatenate([c, c], axis=1), (1, reps))
    sin = jnp.tile(jnp.concatenate([-s, s], axis=1), (1, reps))
    return cos, sin, c.T, s.T


def kernel(x, g_mix, w_in, w_conv_a, gmlp_ln_g, gmlp_ln_b, w_s, b_s, w_conf, b_conf,
           conf_ln_g, conf_ln_b, w_out, g_ffn, w_gate, w_up, w_down, g_final):
    bsz, seq, d = x.shape
    depth = w_in.shape[0]
    bf16 = jnp.bfloat16
    tables = _rope_tables(seq)
    xs = x
    for l in range(depth):
        w_rows, w_t = _pack_w_in(w_in[l])
        yab, yd, k4, ki, qt, vt, qit, wit = _mix_call(
            xs.reshape(bsz, seq, d), g_mix[l][None, :], w_rows, w_t, tables,
            w_conv_a[l], gmlp_ln_g[l][None, :], gmlp_ln_b[l][None, :], w_s[l],
            jnp.repeat(b_s[l].T, HEAD_DIM, axis=1), w_conf[l], b_conf[l][None, :],
            conf_ln_g[l][None, :], conf_ln_b[l][None, :])
        yc = _dsa_call(qit, wit, qt, ki, k4, vt, bsz, seq)
        xs = _out_ffn_call(
            xs.reshape(bsz * seq, d), yab, yc, yd, w_out[l].astype(bf16), g_ffn[l][None, :],
            w_gate[l].astype(bf16), w_up[l].astype(bf16), w_down[l].astype(bf16),
            g_final[None, :], final=(l == depth - 1))
    return xs.reshape(bsz, seq, d)
```

```python
import functools
import math

import jax
import jax.numpy as jnp
from jax import lax
from jax.experimental import pallas as pl
from jax.experimental.pallas import tpu as pltpu

D_MODEL = 1024
GROUP = 256
HEADS = 4
HEAD_DIM = 64
CHUNK = 128
IDX_HEADS = 4
IDX_DIM = 64
TOPK = 256
SHORT_W = 3
CONF_W = 31
FFN_HIDDEN = 2816
ROPE_THETA = 10000.0
NORM_EPS = 1e-6
LN_EPS = 1e-5
NEG = -1e30
BELOW_NEG = -3.0e38

LANES = 128
SUBLANES = 8
A_HALO = 8
D_HALO = 32

COL_A = 0
COL_B = COL_A + 3 * GROUP
COL_K = COL_B + 2 * GROUP
COL_KI = COL_K + GROUP
COL_D = COL_KI + LANES
ROW_COLS = COL_D + 2 * GROUP
TROW_Q = 0
TROW_V = TROW_Q + GROUP
TROW_QI = TROW_V + GROUP
TROW_WI = TROW_QI + IDX_HEADS * IDX_DIM
WI_ROWS = 16
TROWS = TROW_WI + WI_ROWS

MIX_TM = 512
DSA_QB = 512
DSA_KC = 512
DSA_SUB = 64
CNT_ROWS = 16
FFN_TM = 512
FFN_TH = 1408
VMEM_LIMIT = 56 * 1024 * 1024

_NT = (((1,), (1,)), ((), ()))


def _rmsnorm(xf, g):
    ms = jnp.mean(xf * xf, axis=-1, keepdims=True)
    return xf * lax.rsqrt(ms + NORM_EPS) * g


def _layernorm(xf, g, b):
    mu = jnp.mean(xf, axis=-1, keepdims=True)
    xc = xf - mu
    var = jnp.mean(xc * xc, axis=-1, keepdims=True)
    return xc * lax.rsqrt(var + LN_EPS) * g + b


def _rope128(x, cos, sin_signed):
    half = HEAD_DIM // 2
    lane = lax.broadcasted_iota(jnp.int32, x.shape, 1) % HEAD_DIM
    up = pltpu.roll(x, LANES - half, axis=1)
    dn = pltpu.roll(x, half, axis=1)
    return x * cos + jnp.where(lane < half, up, dn) * sin_signed


def _rope_t(x, cos_t, sin_t):
    half = HEAD_DIM // 2
    x1, x2 = x[0:half, :], x[half:HEAD_DIM, :]
    return x1 * cos_t - x2 * sin_t, x2 * cos_t + x1 * sin_t


def _mix_kernel(x_ref, g_ref, w_ref, wt_ref, cos_ref, sin_ref, cost_ref, sint_ref,
                wca_ref, lngb_ref, lnbb_ref, ws_ref, bsb_ref, wcf_ref, bcf_ref,
                lngd_ref, lnbd_ref,
                yab_ref, yd_ref, k4_ref, ki_ref, qt_ref, vt_ref, qit_ref, wit_ref,
                abuf, dbuf):
    tm = x_ref.shape[0]
    f32, bf16 = jnp.float32, jnp.bfloat16
    half = HEAD_DIM // 2

    @pl.when(pl.program_id(1) == 0)
    def _():
        abuf[0:A_HALO, :] = jnp.zeros((A_HALO, GROUP), f32)
        dbuf[0:D_HALO, :] = jnp.zeros((D_HALO, GROUP), f32)

    h = _rmsnorm(x_ref[...], g_ref[...]).astype(bf16)

    def proj(lo, hi):
        return jnp.dot(h, w_ref[:, lo:hi], preferred_element_type=f32)

    def proj_t(lo, hi):
        return lax.dot_general(wt_ref[lo:hi, :], h, _NT, preferred_element_type=f32)

    za = proj(COL_A, COL_B)
    abuf[A_HALO:A_HALO + tm, :] = za[:, GROUP:2 * GROUP] * za[:, 2 * GROUP:3 * GROUP]
    conv = abuf[pl.ds(A_HALO - 2, tm), :] * wca_ref[0:1, :]
    conv = conv + abuf[pl.ds(A_HALO - 1, tm), :] * wca_ref[1:2, :]
    conv = conv + abuf[pl.ds(A_HALO, tm), :] * wca_ref[2:3, :]
    yab_ref[:, 0:GROUP] = (za[:, 0:GROUP] * conv).astype(bf16)
    abuf[0:A_HALO, :] = abuf[tm:tm + A_HALO, :]

    zb = proj(COL_B, COL_K)
    vn = _layernorm(zb[:, GROUP:2 * GROUP], lngb_ref[...], lnbb_ref[...]).astype(bf16)
    r_i = lax.broadcasted_iota(jnp.int32, (CHUNK, CHUNK), 0)
    c_i = lax.broadcasted_iota(jnp.int32, (CHUNK, CHUNK), 1)
    wsm = [jnp.where(r_i >= c_i, ws_ref[hd], 0.0).astype(bf16) for hd in range(HEADS)]
    lane_head = lax.broadcasted_iota(jnp.int32, (CHUNK, GROUP), 1) // HEAD_DIM
    for c in range(tm // CHUNK):
        rows = slice(c * CHUNK, (c + 1) * CHUNK)
        vc = vn[rows, :]
        mixed = bsb_ref[...]
        for hd in range(HEADS):
            full = jnp.dot(wsm[hd], vc, preferred_element_type=f32)
            mixed = mixed + jnp.where(lane_head == hd, full, 0.0)
        yab_ref[rows, GROUP:2 * GROUP] = (zb[rows, 0:GROUP] * mixed).astype(bf16)

    cos = cos_ref[...]
    sin = sin_ref[...]
    zk = proj(COL_K, COL_D)
    for pair in range(2):
        kp = _rope128(zk[:, pair * LANES:(pair + 1) * LANES], cos, sin)
        for sub in range(2):
            k4_ref[2 * pair + sub] = kp[:, sub * HEAD_DIM:(sub + 1) * HEAD_DIM].astype(bf16)
    kip = _rope128(zk[:, GROUP:GROUP + LANES], cos, sin)
    ki_ref[...] = kip[:, 0:IDX_DIM].astype(bf16)

    cos_t = cost_ref[...]
    sin_t = sint_ref[...]
    q_scale = HEAD_DIM ** -0.5 * math.log2(math.e)
    zq = proj_t(TROW_Q, TROW_V)
    zqi = proj_t(TROW_QI, TROW_WI)
    for hd in range(HEADS):
        rows = slice(hd * HEAD_DIM, (hd + 1) * HEAD_DIM)
        o1, o2 = _rope_t(zq[rows, :], cos_t, sin_t)
        qt_ref[hd, 0:half, :] = (o1 * q_scale).astype(bf16)
        qt_ref[hd, half:HEAD_DIM, :] = (o2 * q_scale).astype(bf16)
        o1, o2 = _rope_t(zqi[rows, :], cos_t, sin_t)
        qit_ref[hd, 0:half, :] = o1.astype(bf16)
        qit_ref[hd, half:HEAD_DIM, :] = o2.astype(bf16)
    zv = proj_t(TROW_V, TROW_QI)
    for hd in range(HEADS):
        vt_ref[hd] = zv[hd * HEAD_DIM:(hd + 1) * HEAD_DIM, :].astype(bf16)
    zw = proj_t(TROW_WI, TROWS)
    wit_ref[...] = zw[0:SUBLANES, :] * (IDX_HEADS ** -0.5 * IDX_DIM ** -0.5)

    zd = proj(COL_D, ROW_COLS)
    dbuf[D_HALO:D_HALO + tm, :] = zd[:, 0:GROUP] * jax.nn.sigmoid(zd[:, GROUP:2 * GROUP])
    acc = jnp.broadcast_to(bcf_ref[...], (tm, GROUP))
    for t in range(CONF_W):
        acc = acc + dbuf[pl.ds(D_HALO - (CONF_W - 1) + t, tm), :] * wcf_ref[t:t + 1, :]
    y = _layernorm(acc, lngd_ref[...], lnbd_ref[...])
    yd_ref[...] = (y * jax.nn.sigmoid(y)).astype(bf16)
    dbuf[0:D_HALO, :] = dbuf[tm:tm + D_HALO, :]


def _mix_call(x, g, w_rows, w_t, tables, wca, lngb, lnbb, ws, bsb, wcf, bcf, lngd, lnbd):
    bsz, seq, _ = x.shape
    n = bsz * seq
    tm = MIX_TM
    nt = seq // tm
    bf16 = jnp.bfloat16
    half = HEAD_DIM // 2
    cos, sin, cos_t, sin_t = tables
    row = lambda b, i: (b * nt + i, 0)
    row4 = lambda b, i: (0, b * nt + i, 0)
    col = lambda b, i: (0, b * nt + i)
    col4 = lambda b, i: (0, 0, b * nt + i)
    const2 = lambda b, i: (0, 0)
    const3 = lambda b, i: (0, 0, 0)
    in_specs = [
        pl.BlockSpec((None, tm, D_MODEL), lambda b, i: (b, i, 0)),
        pl.BlockSpec((1, D_MODEL), const2),
        pl.BlockSpec((D_MODEL, ROW_COLS), const2),
        pl.BlockSpec((TROWS, D_MODEL), const2),
        pl.BlockSpec((tm, LANES), lambda b, i: (i, 0)),
        pl.BlockSpec((tm, LANES), lambda b, i: (i, 0)),
        pl.BlockSpec((half, tm), lambda b, i: (0, i)),
        pl.BlockSpec((half, tm), lambda b, i: (0, i)),
        pl.BlockSpec((SHORT_W, GROUP), const2),
        pl.BlockSpec((1, GROUP), const2),
        pl.BlockSpec((1, GROUP), const2),
        pl.BlockSpec((HEADS, CHUNK, CHUNK), const3),
        pl.BlockSpec((CHUNK, GROUP), const2),
        pl.BlockSpec((CONF_W, GROUP), const2),
        pl.BlockSpec((1, GROUP), const2),
        pl.BlockSpec((1, GROUP), const2),
        pl.BlockSpec((1, GROUP), const2),
    ]
    out_shape = [
        jax.ShapeDtypeStruct((n, 2 * GROUP), bf16),
        jax.ShapeDtypeStruct((n, GROUP), bf16),
        jax.ShapeDtypeStruct((HEADS, n, HEAD_DIM), bf16),
        jax.ShapeDtypeStruct((n, IDX_DIM), bf16),
        jax.ShapeDtypeStruct((HEADS, HEAD_DIM, n), bf16),
        jax.ShapeDtypeStruct((HEADS, HEAD_DIM, n), bf16),
        jax.ShapeDtypeStruct((IDX_HEADS, IDX_DIM, n), bf16),
        jax.ShapeDtypeStruct((SUBLANES, n), jnp.float32),
    ]
    out_specs = [
        pl.BlockSpec((tm, 2 * GROUP), row),
        pl.BlockSpec((tm, GROUP), row),
        pl.BlockSpec((HEADS, tm, HEAD_DIM), row4),
        pl.BlockSpec((tm, IDX_DIM), row),
        pl.BlockSpec((HEADS, HEAD_DIM, tm), col4),
        pl.BlockSpec((HEADS, HEAD_DIM, tm), col4),
        pl.BlockSpec((IDX_HEADS, IDX_DIM, tm), col4),
        pl.BlockSpec((SUBLANES, tm), col),
    ]
    return pl.pallas_call(
        _mix_kernel,
        grid=(bsz, nt),
        in_specs=in_specs,
        out_specs=out_specs,
        out_shape=out_shape,
        scratch_shapes=[pltpu.VMEM((tm + A_HALO, GROUP), jnp.float32),
                        pltpu.VMEM((tm + D_HALO, GROUP), jnp.float32)],
        compiler_params=pltpu.CompilerParams(
            dimension_semantics=("arbitrary", "arbitrary"),
            vmem_limit_bytes=VMEM_LIMIT),
        name="mix",
    )(x, g, w_rows, w_t, cos, sin, cos_t, sin_t, wca, lngb, lnbb, ws, bsb, wcf, bcf,
      lngd, lnbd)


def _threshold_bits(trial):
    bits = jnp.where(trial < 0, trial & jnp.int32(0x7FFFFFFF), ~trial)
    return pltpu.bitcast(bits, jnp.float32)


def _dsa_kernel(qit_ref, wit_ref, qt_ref, ki_ref, k4_ref, vt_ref, o_ref,
                sc_ref, acc_ref, s_ref, p_ref, *, seq):
    f32, bf16, i32 = jnp.float32, jnp.bfloat16, jnp.int32
    qb = wit_ref.shape[1]
    kc, sub = DSA_KC, DSA_SUB
    ndiag = qb // kc
    i = pl.program_id(1)
    nfull = i * ndiag
    nchunks = nfull + ndiag
    n_outside = seq - nchunks * kc
    key_i = lax.broadcasted_iota(i32, (sub, qb), 0)
    qry_i = lax.broadcasted_iota(i32, (sub, qb), 1)
    cnt_i = lax.broadcasted_iota(i32, (CNT_ROWS, qb), 0)

    def causal(x, d, r):
        return jnp.where(key_i + (d * kc + r) <= qry_i, x, NEG)

    def run_chunks(chunk_fn, carry):
        carry = lax.fori_loop(0, nfull, lambda c, cr: chunk_fn(c, None, cr), carry)
        for d in range(ndiag):
            carry = chunk_fn(nfull + d, d, carry)
        return carry

    def score_chunk(c, d, carry):
        off = pl.multiple_of(c * kc, kc)
        for t in range(kc // sub):
            r0 = off + t * sub
            kis = ki_ref[pl.ds(r0, sub), :]
            sc = None
            for j in range(IDX_HEADS):
                rel = jnp.dot(kis, qit_ref[j], preferred_element_type=f32)
                term = jnp.maximum(rel, 0.0) * wit_ref[j:j + 1, :]
                sc = term if sc is None else sc + term
            if d is not None:
                sc = causal(sc, d, t * sub)
            sc_ref[pl.ds(r0, sub), :] = sc
        return carry

    run_chunks(score_chunk, 0)

    def count(indicator):
        def body(c, acc):
            off = pl.multiple_of(c * kc, kc)
            for t in range(kc // CNT_ROWS):
                r0 = off + t * CNT_ROWS
                acc = acc + indicator(sc_ref[pl.ds(r0, CNT_ROWS), :], r0)
            return acc
        acc = lax.fori_loop(0, nchunks, body, jnp.zeros((CNT_ROWS, qb), i32))
        return jnp.sum(acc, axis=0, keepdims=True)

    def count_ge(thr):
        inside = count(lambda s, r0: jnp.where(s >= thr, 1, 0))
        return inside + jnp.where(thr <= NEG, n_outside, 0)

    def bit_step(step, prefix):
        trial = prefix | jnp.left_shift(jnp.int32(1), 31 - step)
        cnt = count_ge(_threshold_bits(trial))
        return jnp.where(cnt >= TOPK, trial, prefix)

    prefix = lax.fori_loop(0, 32, bit_step, jnp.zeros((1, qb), i32))
    vstar = _threshold_bits(prefix)
    c_ge = count_ge(vstar)

    @pl.when(jnp.max(c_ge) > TOPK)
    def _():
        c_gt = (count(lambda s, r0: jnp.where(s > vstar, 1, 0))
                + jnp.where(vstar < NEG, n_outside, 0))
        want = TOPK - c_gt

        def idx_step(step, bound):
            trial = bound | jnp.left_shift(jnp.int32(1), 11 - step)
            cnt = count(lambda s, r0: jnp.where(
                s == vstar, jnp.where(cnt_i + r0 < trial, 1, 0), 0))
            return jnp.where(cnt < want, trial, bound)

        bound = lax.fori_loop(0, 12, idx_step, jnp.zeros((1, qb), i32))

        def drop_body(c, carry):
            off = pl.multiple_of(c * sub, sub)
            s = sc_ref[pl.ds(off, sub), :]
            drop = jnp.where(s == vstar, jnp.where(key_i + off > bound, 1, 0), 0)
            sc_ref[pl.ds(off, sub), :] = jnp.where(drop > 0, BELOW_NEG, s)
            return carry

        lax.fori_loop(0, nchunks * (kc // sub), drop_body, 0)

    def fold8(x, op):
        return op(x.reshape(sub // SUBLANES, SUBLANES, qb), axis=0)

    def cap_chunk(c, d, carry):
        off = pl.multiple_of(c * kc, kc)
        for t in range(kc // sub):
            r = t * sub
            cap = jnp.where(sc_ref[pl.ds(off + r, sub), :] >= vstar, jnp.inf, NEG)
            if d is not None:
                cap = causal(cap, d, r)
            sc_ref[pl.ds(off + r, sub), :] = cap
        return carry

    run_chunks(cap_chunk, 0)
    acc_ref[...] = jnp.zeros(acc_ref.shape, f32)

    def attend_chunk(c, d, carry):
        ms, ls = carry
        off = pl.multiple_of(c * kc, kc)
        new_ms, new_ls = list(ms), list(ls)

        def logits_stage(hd):
            mpart = None
            for t in range(kc // sub):
                r = t * sub
                s = jnp.dot(k4_ref[hd, pl.ds(off + r, sub), :], qt_ref[hd],
                            preferred_element_type=f32)
                s = jnp.minimum(s, sc_ref[pl.ds(off + r, sub), :])
                s_ref[hd, r:r + sub, :] = s
                m8 = fold8(s, jnp.max)
                mpart = m8 if mpart is None else jnp.maximum(mpart, m8)
            new_ms[hd] = jnp.maximum(ms[hd], jnp.max(mpart, axis=0, keepdims=True))

        def value_stage(hd):
            lpart = None
            for t in range(kc // sub):
                r = t * sub
                p = jnp.exp2(s_ref[hd, r:r + sub, :] - new_ms[hd])
                l8 = fold8(p, jnp.sum)
                lpart = l8 if lpart is None else lpart + l8
                p_ref[hd, r:r + sub, :] = p.astype(bf16)
            alpha = jnp.exp2(ms[hd] - new_ms[hd])
            new_ls[hd] = alpha * ls[hd] + jnp.sum(lpart, axis=0, keepdims=True)
            pv = jnp.dot(vt_ref[hd, :, pl.ds(off, kc)], p_ref[hd], preferred_element_type=f32)
            acc_ref[hd] = acc_ref[hd] * alpha + pv

        logits_stage(0)
        for hd in range(HEADS):
            if hd + 1 < HEADS:
                logits_stage(hd + 1)
            value_stage(hd)
        return tuple(new_ms), tuple(new_ls)

    init = (tuple(jnp.full((1, qb), NEG, f32) for _ in range(HEADS)),
            tuple(jnp.zeros((1, qb), f32) for _ in range(HEADS)))
    _, ls = run_chunks(attend_chunk, init)

    out_t = jnp.concatenate([acc_ref[hd] / ls[hd] for hd in range(HEADS)], axis=0)
    o_ref[...] = out_t.T.astype(o_ref.dtype)


def _dsa_call(qit, wit, qt, ki, k4, vt, bsz, seq):
    n = bsz * seq
    qb = DSA_QB
    nq = seq // qb
    qcol4 = lambda b, i: (0, 0, b * nq + i)
    return pl.pallas_call(
        functools.partial(_dsa_kernel, seq=seq),
        grid=(bsz, nq),
        in_specs=[
            pl.BlockSpec((IDX_HEADS, IDX_DIM, qb), qcol4),
            pl.BlockSpec((SUBLANES, qb), lambda b, i: (0, b * nq + i)),
            pl.BlockSpec((HEADS, HEAD_DIM, qb), qcol4),
            pl.BlockSpec((seq, IDX_DIM), lambda b, i: (b, 0)),
            pl.BlockSpec((HEADS, seq, HEAD_DIM), lambda b, i: (0, b, 0)),
            pl.BlockSpec((HEADS, HEAD_DIM, seq), lambda b, i: (0, 0, b)),
        ],
        out_specs=pl.BlockSpec((qb, GROUP), lambda b, i: (b * nq + i, 0)),
        out_shape=jax.ShapeDtypeStruct((n, GROUP), jnp.bfloat16),
        scratch_shapes=[
            pltpu.VMEM((seq, qb), jnp.float32),
            pltpu.VMEM((HEADS, HEAD_DIM, qb), jnp.float32),
            pltpu.VMEM((HEADS, DSA_KC, qb), jnp.float32),
            pltpu.VMEM((HEADS, DSA_KC, qb), jnp.bfloat16),
        ],
        compiler_params=pltpu.CompilerParams(
            dimension_semantics=("arbitrary", "arbitrary"),
            vmem_limit_bytes=VMEM_LIMIT),
        name="dsa",
    )(qit, wit, qt, ki, k4, vt)


def _out_ffn_kernel(x_ref, yab_ref, yc_ref, yd_ref, wout_ref, gffn_ref, wg_ref, wu_ref,
                    wd_ref, gfin_ref, o_ref, acc_sc, hf_sc, *, final):
    f32, bf16 = jnp.float32, jnp.bfloat16
    j = pl.program_id(1)

    @pl.when(j == 0)
    def _():
        y = jnp.concatenate([yab_ref[...], yc_ref[...], yd_ref[...]], axis=1)
        x1 = x_ref[...] + jnp.dot(y, wout_ref[...], preferred_element_type=f32)
        acc_sc[...] = x1
        hf_sc[...] = _rmsnorm(x1, gffn_ref[...]).astype(bf16)

    hf = hf_sc[...]
    gate = jnp.dot(hf, wg_ref[...], preferred_element_type=f32)
    up = jnp.dot(hf, wu_ref[...], preferred_element_type=f32)
    act = (gate * jax.nn.sigmoid(gate) * up).astype(bf16)
    acc_sc[...] += jnp.dot(act, wd_ref[...], preferred_element_type=f32)

    @pl.when(j == pl.num_programs(1) - 1)
    def _():
        out = acc_sc[...]
        if final:
            out = _rmsnorm(out, gfin_ref[...])
        o_ref[...] = out


def _out_ffn_call(x2d, yab, yc, yd, wout, gffn, wg, wu, wd, gfin, final):
    n = x2d.shape[0]
    tm, th = FFN_TM, FFN_TH
    row = lambda i, j: (i, 0)
    const = lambda i, j: (0, 0)
    return pl.pallas_call(
        functools.partial(_out_ffn_kernel, final=final),
        grid=(n // tm, FFN_HIDDEN // th),
        in_specs=[
            pl.BlockSpec((tm, D_MODEL), row),
            pl.BlockSpec((tm, 2 * GROUP), row),
            pl.BlockSpec((tm, GROUP), row),
            pl.BlockSpec((tm, GROUP), row),
            pl.BlockSpec((D_MODEL, D_MODEL), const),
            pl.BlockSpec((1, D_MODEL), const),
            pl.BlockSpec((D_MODEL, th), lambda i, j: (0, j)),
            pl.BlockSpec((D_MODEL, th), lambda i, j: (0, j)),
            pl.BlockSpec((th, D_MODEL), lambda i, j: (j, 0)),
            pl.BlockSpec((1, D_MODEL), const),
        ],
        out_specs=pl.BlockSpec((tm, D_MODEL), row),
        out_shape=jax.ShapeDtypeStruct((n, D_MODEL), jnp.float32),
        scratch_shapes=[pltpu.VMEM((tm, D_MODEL), jnp.float32),
                        pltpu.VMEM((tm, D_MODEL), jnp.bfloat16)],
        compiler_params=pltpu.CompilerParams(
            dimension_semantics=("arbitrary", "arbitrary"),
            vmem_limit_bytes=VMEM_LIMIT),
        name="out_ffn",
    )(x2d, yab, yc, yd, wout, gffn, wg, wu, wd, gfin)


def _pack_w_in(w):
    g = GROUP
    cuts = [0, 3 * g, 5 * g, 6 * g, 7 * g, 8 * g, 9 * g, 9 * g + IDX_DIM,
            9 * g + IDX_DIM + IDX_HEADS, 11 * g + IDX_DIM + IDX_HEADS]
    a, b, q, k, v, qi, ki, wi, d = [w[:, cuts[t]:cuts[t + 1]] for t in range(9)]
    z = lambda cols: jnp.zeros((w.shape[0], cols), w.dtype)
    rows = jnp.concatenate([a, b, k, ki, z(LANES - IDX_DIM), d], axis=1)
    trans = jnp.concatenate([q, v, qi, wi, z(WI_ROWS - IDX_HEADS)], axis=1).T
    return rows.astype(jnp.bfloat16), trans.astype(jnp.bfloat16)


def _rope_tables(seq):
    inv_freq = ROPE_THETA ** (-jnp.arange(0, HEAD_DIM, 2, dtype=jnp.float32) / HEAD_DIM)
    ang = jnp.arange(seq, dtype=jnp.float32)[:, None] * inv_freq[None, :]
    c, s = jnp.cos(ang), jnp.sin(ang)
    reps = LANES // HEAD_DIM
    cos = jnp.tile(jnp.concatenate([c, c], axis=1), (1, reps))
    sin = jnp.tile(jnp.concatenate([-s, s], axis=1), (1, reps))
    return cos, sin, c.T, s.T


def kernel(x, g_mix, w_in, w_conv_a, gmlp_ln_g, gmlp_ln_b, w_s, b_s, w_conf, b_conf,
           conf_ln_g, conf_ln_b, w_out, g_ffn, w_gate, w_up, w_down, g_final):
    bsz, seq, d = x.shape
    depth = w_in.shape[0]
    bf16 = jnp.bfloat16
    tables = _rope_tables(seq)
    xs = x
    for l in range(depth):
        w_rows, w_t = _pack_w_in(w_in[l])
        yab, yd, k4, ki, qt, vt, qit, wit = _mix_call(
            xs.reshape(bsz, seq, d), g_mix[l][None, :], w_rows, w_t, tables,
            w_conv_a[l], gmlp_ln_g[l][None, :], gmlp_ln_b[l][None, :], w_s[l],
            jnp.repeat(b_s[l].T, HEAD_DIM, axis=1), w_conf[l], b_conf[l][None, :],
            conf_ln_g[l][None, :], conf_ln_b[l][None, :])
        yc = _dsa_call(qit, wit, qt, ki, k4, vt, bsz, seq)
        xs = _out_ffn_call(
            xs.reshape(bsz * seq, d), yab, yc, yd, w_out[l].astype(bf16), g_ffn[l][None, :],
            w_gate[l].astype(bf16), w_up[l].astype(bf16), w_down[l].astype(bf16),
            g_final[None, :], final=(l == depth - 1))
    return xs.reshape(bsz, seq, d)
```

```python
import functools
import math

import jax
import jax.numpy as jnp
from jax import lax
from jax.experimental import pallas as pl
from jax.experimental.pallas import tpu as pltpu

D_MODEL = 1024
GROUP = 256
HEADS = 4
HEAD_DIM = 64
CHUNK = 128
IDX_HEADS = 4
IDX_DIM = 64
TOPK = 256
SHORT_W = 3
CONF_W = 31
FFN_HIDDEN = 2816
ROPE_THETA = 10000.0
NORM_EPS = 1e-6
LN_EPS = 1e-5
NEG = -1e30
BELOW_NEG = -3.0e38

LANES = 128
SUBLANES = 8
A_HALO = 8
D_HALO = 32

COL_A = 0
COL_B = COL_A + 3 * GROUP
COL_K = COL_B + 2 * GROUP
COL_KI = COL_K + GROUP
COL_D = COL_KI + LANES
ROW_COLS = COL_D + 2 * GROUP
TROW_Q = 0
TROW_V = TROW_Q + GROUP
TROW_QI = TROW_V + GROUP
TROW_WI = TROW_QI + IDX_HEADS * IDX_DIM
WI_ROWS = 16
TROWS = TROW_WI + WI_ROWS

MIX_TM = 512
DSA_QB = 512
DSA_KC = 512
DSA_SUB = 64
CNT_ROWS = 16
FFN_TM = 512
FFN_TH = 1408
VMEM_LIMIT = 56 * 1024 * 1024

_NT = (((1,), (1,)), ((), ()))


def _rmsnorm(xf, g):
    ms = jnp.mean(xf * xf, axis=-1, keepdims=True)
    return xf * lax.rsqrt(ms + NORM_EPS) * g


def _layernorm(xf, g, b):
    mu = jnp.mean(xf, axis=-1, keepdims=True)
    xc = xf - mu
    var = jnp.mean(xc * xc, axis=-1, keepdims=True)
    return xc * lax.rsqrt(var + LN_EPS) * g + b


def _rope128(x, cos, sin_signed):
    half = HEAD_DIM // 2
    lane = lax.broadcasted_iota(jnp.int32, x.shape, 1) % HEAD_DIM
    up = pltpu.roll(x, LANES - half, axis=1)
    dn = pltpu.roll(x, half, axis=1)
    return x * cos + jnp.where(lane < half, up, dn) * sin_signed


def _rope_t(x, cos_t, sin_t):
    half = HEAD_DIM // 2
    x1, x2 = x[0:half, :], x[half:HEAD_DIM, :]
    return x1 * cos_t - x2 * sin_t, x2 * cos_t + x1 * sin_t


def _mix_kernel(x_ref, g_ref, w_ref, wt_ref, cos_ref, sin_ref, cost_ref, sint_ref,
                wca_ref, lngb_ref, lnbb_ref, ws_ref, bsb_ref, wcf_ref, bcf_ref,
                lngd_ref, lnbd_ref,
                yab_ref, yd_ref, k4_ref, ki_ref, qt_ref, vt_ref, qit_ref, wit_ref,
                abuf, dbuf, sbuf):
    tm = x_ref.shape[0]
    f32, bf16 = jnp.float32, jnp.bfloat16
    half = HEAD_DIM // 2

    @pl.when(pl.program_id(1) == 0)
    def _():
        abuf[0:A_HALO, :] = jnp.zeros((A_HALO, GROUP), f32)
        dbuf[0:D_HALO, :] = jnp.zeros((D_HALO, GROUP), f32)

    h = _rmsnorm(x_ref[...], g_ref[...]).astype(bf16)

    def proj(lo, hi):
        return jnp.dot(h, w_ref[:, lo:hi], preferred_element_type=f32)

    def proj_t(lo, hi):
        return lax.dot_general(wt_ref[lo:hi, :], h, _NT, preferred_element_type=f32)

    za = proj(COL_A, COL_B)
    abuf[A_HALO:A_HALO + tm, :] = za[:, GROUP:2 * GROUP] * za[:, 2 * GROUP:3 * GROUP]
    conv = abuf[pl.ds(A_HALO - 2, tm), :] * wca_ref[0:1, :]
    conv = conv + abuf[pl.ds(A_HALO - 1, tm), :] * wca_ref[1:2, :]
    conv = conv + abuf[pl.ds(A_HALO, tm), :] * wca_ref[2:3, :]
    yab_ref[:, 0:GROUP] = (za[:, 0:GROUP] * conv).astype(bf16)
    abuf[0:A_HALO, :] = abuf[tm:tm + A_HALO, :]

    zb = proj(COL_B, COL_K)
    vn = _layernorm(zb[:, GROUP:2 * GROUP], lngb_ref[...], lnbb_ref[...]).astype(bf16)
    r_i = lax.broadcasted_iota(jnp.int32, (CHUNK, CHUNK), 0)
    c_i = lax.broadcasted_iota(jnp.int32, (CHUNK, CHUNK), 1)
    wsm = [jnp.where(r_i >= c_i, ws_ref[hd], 0.0).astype(bf16) for hd in range(HEADS)]
    lane_head = lax.broadcasted_iota(jnp.int32, (CHUNK, GROUP), 1) // HEAD_DIM
    for c in range(tm // CHUNK):
        rows = slice(c * CHUNK, (c + 1) * CHUNK)
        vc = vn[rows, :]
        mixed = bsb_ref[...]
        for hd in range(HEADS):
            full = jnp.dot(wsm[hd], vc, preferred_element_type=f32)
            mixed = mixed + jnp.where(lane_head == hd, full, 0.0)
        yab_ref[rows, GROUP:2 * GROUP] = (zb[rows, 0:GROUP] * mixed).astype(bf16)

    cos = cos_ref[...]
    sin = sin_ref[...]
    zk = proj(COL_K, COL_D)
    for pair in range(2):
        kp = _rope128(zk[:, pair * LANES:(pair + 1) * LANES], cos, sin)
        for sub in range(2):
            k4_ref[2 * pair + sub] = kp[:, sub * HEAD_DIM:(sub + 1) * HEAD_DIM].astype(bf16)
    kip = _rope128(zk[:, GROUP:GROUP + LANES], cos, sin)
    ki_ref[...] = kip[:, 0:IDX_DIM].astype(bf16)

    cos_t = cost_ref[...]
    sin_t = sint_ref[...]
    q_scale = HEAD_DIM ** -0.5 * math.log2(math.e)
    zq = proj_t(TROW_Q, TROW_V)
    zqi = proj_t(TROW_QI, TROW_WI)
    for hd in range(HEADS):
        rows = slice(hd * HEAD_DIM, (hd + 1) * HEAD_DIM)
        o1, o2 = _rope_t(zq[rows, :], cos_t, sin_t)
        qt_ref[hd, 0:half, :] = (o1 * q_scale).astype(bf16)
        qt_ref[hd, half:HEAD_DIM, :] = (o2 * q_scale).astype(bf16)
        o1, o2 = _rope_t(zqi[rows, :], cos_t, sin_t)
        qit_ref[hd, 0:half, :] = o1.astype(bf16)
        qit_ref[hd, half:HEAD_DIM, :] = o2.astype(bf16)
    zv = proj_t(TROW_V, TROW_QI)
    for hd in range(HEADS):
        vt_ref[hd] = zv[hd * HEAD_DIM:(hd + 1) * HEAD_DIM, :].astype(bf16)
    zw = proj_t(TROW_WI, TROWS)
    wit_ref[...] = zw[0:SUBLANES, :] * (IDX_HEADS ** -0.5 * IDX_DIM ** -0.5)

    zd = proj(COL_D, ROW_COLS)
    dbuf[D_HALO:D_HALO + tm, :] = zd[:, 0:GROUP] * jax.nn.sigmoid(zd[:, GROUP:2 * GROUP])
    acc = jnp.broadcast_to(bcf_ref[...], (tm, GROUP))
    first = D_HALO - (CONF_W - 1)
    for r in range(SUBLANES):
        taps = [t for t in range(CONF_W) if (first + t) % SUBLANES == r]
        span = (first + taps[-1]) - (first + taps[0]) + tm
        sbuf[0:span, :] = dbuf[pl.ds(first + taps[0], span), :]
        for t in taps:
            lo = t - taps[0]
            acc = acc + sbuf[lo:lo + tm, :] * wcf_ref[t:t + 1, :]
    y = _layernorm(acc, lngd_ref[...], lnbd_ref[...])
    yd_ref[...] = (y * jax.nn.sigmoid(y)).astype(bf16)
    dbuf[0:D_HALO, :] = dbuf[tm:tm + D_HALO, :]


def _mix_call(x, g, w_rows, w_t, tables, wca, lngb, lnbb, ws, bsb, wcf, bcf, lngd, lnbd):
    bsz, seq, _ = x.shape
    n = bsz * seq
    tm = MIX_TM
    nt = seq // tm
    bf16 = jnp.bfloat16
    half = HEAD_DIM // 2
    cos, sin, cos_t, sin_t = tables
    row = lambda b, i: (b * nt + i, 0)
    row4 = lambda b, i: (0, b * nt + i, 0)
    col = lambda b, i: (0, b * nt + i)
    col4 = lambda b, i: (0, 0, b * nt + i)
    const2 = lambda b, i: (0, 0)
    const3 = lambda b, i: (0, 0, 0)
    in_specs = [
        pl.BlockSpec((None, tm, D_MODEL), lambda b, i: (b, i, 0)),
        pl.BlockSpec((1, D_MODEL), const2),
        pl.BlockSpec((D_MODEL, ROW_COLS), const2),
        pl.BlockSpec((TROWS, D_MODEL), const2),
        pl.BlockSpec((tm, LANES), lambda b, i: (i, 0)),
        pl.BlockSpec((tm, LANES), lambda b, i: (i, 0)),
        pl.BlockSpec((half, tm), lambda b, i: (0, i)),
        pl.BlockSpec((half, tm), lambda b, i: (0, i)),
        pl.BlockSpec((SHORT_W, GROUP), const2),
        pl.BlockSpec((1, GROUP), const2),
        pl.BlockSpec((1, GROUP), const2),
        pl.BlockSpec((HEADS, CHUNK, CHUNK), const3),
        pl.BlockSpec((CHUNK, GROUP), const2),
        pl.BlockSpec((CONF_W, GROUP), const2),
        pl.BlockSpec((1, GROUP), const2),
        pl.BlockSpec((1, GROUP), const2),
        pl.BlockSpec((1, GROUP), const2),
    ]
    out_shape = [
        jax.ShapeDtypeStruct((n, 2 * GROUP), bf16),
        jax.ShapeDtypeStruct((n, GROUP), bf16),
        jax.ShapeDtypeStruct((HEADS, n, HEAD_DIM), bf16),
        jax.ShapeDtypeStruct((n, IDX_DIM), bf16),
        jax.ShapeDtypeStruct((HEADS, HEAD_DIM, n), bf16),
        jax.ShapeDtypeStruct((HEADS, HEAD_DIM, n), bf16),
        jax.ShapeDtypeStruct((IDX_HEADS, IDX_DIM, n), bf16),
        jax.ShapeDtypeStruct((SUBLANES, n), jnp.float32),
    ]
    out_specs = [
        pl.BlockSpec((tm, 2 * GROUP), row),
        pl.BlockSpec((tm, GROUP), row),
        pl.BlockSpec((HEADS, tm, HEAD_DIM), row4),
        pl.BlockSpec((tm, IDX_DIM), row),
        pl.BlockSpec((HEADS, HEAD_DIM, tm), col4),
        pl.BlockSpec((HEADS, HEAD_DIM, tm), col4),
        pl.BlockSpec((IDX_HEADS, IDX_DIM, tm), col4),
        pl.BlockSpec((SUBLANES, tm), col),
    ]
    return pl.pallas_call(
        _mix_kernel,
        grid=(bsz, nt),
        in_specs=in_specs,
        out_specs=out_specs,
        out_shape=out_shape,
        scratch_shapes=[pltpu.VMEM((tm + A_HALO, GROUP), jnp.float32),
                        pltpu.VMEM((tm + D_HALO, GROUP), jnp.float32),
                        pltpu.VMEM((tm + D_HALO, GROUP), jnp.float32)],
        compiler_params=pltpu.CompilerParams(
            dimension_semantics=("arbitrary", "arbitrary"),
            vmem_limit_bytes=VMEM_LIMIT),
        name="mix",
    )(x, g, w_rows, w_t, cos, sin, cos_t, sin_t, wca, lngb, lnbb, ws, bsb, wcf, bcf,
      lngd, lnbd)


def _threshold_bits(trial):
    bits = jnp.where(trial < 0, trial & jnp.int32(0x7FFFFFFF), ~trial)
    return pltpu.bitcast(bits, jnp.float32)


def _dsa_kernel(qit_ref, wit_ref, qt_ref, ki_ref, k4_ref, vt_ref, o_ref,
                sc_ref, acc_ref, s_ref, p_ref, *, seq):
    f32, bf16, i32 = jnp.float32, jnp.bfloat16, jnp.int32
    qb = wit_ref.shape[1]
    kc, sub = DSA_KC, DSA_SUB
    ndiag = qb // kc
    i = pl.program_id(1)
    nfull = i * ndiag
    nchunks = nfull + ndiag
    n_outside = seq - nchunks * kc
    key_i = lax.broadcasted_iota(i32, (sub, qb), 0)
    qry_i = lax.broadcasted_iota(i32, (sub, qb), 1)
    cnt_i = lax.broadcasted_iota(i32, (CNT_ROWS, qb), 0)

    def causal(x, d, r):
        return jnp.where(key_i + (d * kc + r) <= qry_i, x, NEG)

    def run_chunks(chunk_fn, carry):
        carry = lax.fori_loop(0, nfull, lambda c, cr: chunk_fn(c, None, cr), carry)
        for d in range(ndiag):
            carry = chunk_fn(nfull + d, d, carry)
        return carry

    def score_chunk(c, d, carry):
        off = pl.multiple_of(c * kc, kc)
        for t in range(kc // sub):
            r0 = off + t * sub
            kis = ki_ref[pl.ds(r0, sub), :]
            sc = None
            for j in range(IDX_HEADS):
                rel = jnp.dot(kis, qit_ref[j], preferred_element_type=f32)
                term = jnp.maximum(rel, 0.0) * wit_ref[j:j + 1, :]
                sc = term if sc is None else sc + term
            if d is not None:
                sc = causal(sc, d, t * sub)
            sc_ref[pl.ds(r0, sub), :] = sc
        return carry

    run_chunks(score_chunk, 0)

    def count(indicator):
        def body(c, acc):
            off = pl.multiple_of(c * kc, kc)
            for t in range(kc // CNT_ROWS):
                r0 = off + t * CNT_ROWS
                acc = acc + indicator(sc_ref[pl.ds(r0, CNT_ROWS), :], r0)
            return acc
        acc = lax.fori_loop(0, nchunks, body, jnp.zeros((CNT_ROWS, qb), i32))
        return jnp.sum(acc, axis=0, keepdims=True)

    def count_ge(thr):
        inside = count(lambda s, r0: jnp.where(s >= thr, 1, 0))
        return inside + jnp.where(thr <= NEG, n_outside, 0)

    def bit_cond(state):
        step, worst, _, _ = state
        return jnp.logical_and(step < 32, worst > TOPK)

    def bit_step(state):
        step, _, prefix, c_prefix = state
        trial = prefix | jnp.left_shift(jnp.int32(1), 31 - step)
        cnt = count_ge(_threshold_bits(trial))
        accept = cnt >= TOPK
        c_prefix = jnp.where(accept, cnt, c_prefix)
        return step + 1, jnp.max(c_prefix), jnp.where(accept, trial, prefix), c_prefix

    _, _, prefix, c_ge = lax.while_loop(
        bit_cond, bit_step,
        (jnp.int32(0), jnp.int32(seq), jnp.zeros((1, qb), i32), jnp.full((1, qb), seq, i32)))
    vstar = _threshold_bits(prefix)

    @pl.when(jnp.max(c_ge) > TOPK)
    def _():
        c_gt = (count(lambda s, r0: jnp.where(s > vstar, 1, 0))
                + jnp.where(vstar < NEG, n_outside, 0))
        want = TOPK - c_gt

        def idx_step(step, bound):
            trial = bound | jnp.left_shift(jnp.int32(1), 11 - step)
            cnt = count(lambda s, r0: jnp.where(
                s == vstar, jnp.where(cnt_i + r0 < trial, 1, 0), 0))
            return jnp.where(cnt < want, trial, bound)

        bound = lax.fori_loop(0, 12, idx_step, jnp.zeros((1, qb), i32))

        def drop_body(c, carry):
            off = pl.multiple_of(c * sub, sub)
            s = sc_ref[pl.ds(off, sub), :]
            drop = jnp.where(s == vstar, jnp.where(key_i + off > bound, 1, 0), 0)
            sc_ref[pl.ds(off, sub), :] = jnp.where(drop > 0, BELOW_NEG, s)
            return carry

        lax.fori_loop(0, nchunks * (kc // sub), drop_body, 0)

    def fold8(x, op):
        return op(x.reshape(sub // SUBLANES, SUBLANES, qb), axis=0)

    def cap_chunk(c, d, carry):
        off = pl.multiple_of(c * kc, kc)
        for t in range(kc // sub):
            r = t * sub
            cap = jnp.where(sc_ref[pl.ds(off + r, sub), :] >= vstar, jnp.inf, NEG)
            if d is not None:
                cap = causal(cap, d, r)
            sc_ref[pl.ds(off + r, sub), :] = cap
        return carry

    run_chunks(cap_chunk, 0)
    acc_ref[...] = jnp.zeros(acc_ref.shape, f32)

    def attend_chunk(c, d, carry):
        ms, ls = carry
        off = pl.multiple_of(c * kc, kc)
        new_ms, new_ls = list(ms), list(ls)

        def logits_stage(hd):
            mpart = None
            for t in range(kc // sub):
                r = t * sub
                s = jnp.dot(k4_ref[hd, pl.ds(off + r, sub), :], qt_ref[hd],
                            preferred_element_type=f32)
                s = jnp.minimum(s, sc_ref[pl.ds(off + r, sub), :])
                s_ref[hd, r:r + sub, :] = s
                m8 = fold8(s, jnp.max)
                mpart = m8 if mpart is None else jnp.maximum(mpart, m8)
            new_ms[hd] = jnp.maximum(ms[hd], jnp.max(mpart, axis=0, keepdims=True))

        def value_stage(hd):
            lpart = None
            for t in range(kc // sub):
                r = t * sub
                p = jnp.exp2(s_ref[hd, r:r + sub, :] - new_ms[hd])
                l8 = fold8(p, jnp.sum)
                lpart = l8 if lpart is None else lpart + l8
                p_ref[hd, r:r + sub, :] = p.astype(bf16)
            alpha = jnp.exp2(ms[hd] - new_ms[hd])
            new_ls[hd] = alpha * ls[hd] + jnp.sum(lpart, axis=0, keepdims=True)
            pv = jnp.dot(vt_ref[hd, :, pl.ds(off, kc)], p_ref[hd], preferred_element_type=f32)
            acc_ref[hd] = acc_ref[hd] * alpha + pv

        logits_stage(0)
        for hd in range(HEADS):
            if hd + 1 < HEADS:
                logits_stage(hd + 1)
            value_stage(hd)
        return tuple(new_ms), tuple(new_ls)

    init = (tuple(jnp.full((1, qb), NEG, f32) for _ in range(HEADS)),
            tuple(jnp.zeros((1, qb), f32) for _ in range(HEADS)))
    _, ls = run_chunks(attend_chunk, init)

    out_t = jnp.concatenate([acc_ref[hd] / ls[hd] for hd in range(HEADS)], axis=0)
    o_ref[...] = out_t.T.astype(o_ref.dtype)


def _dsa_call(qit, wit, qt, ki, k4, vt, bsz, seq):
    n = bsz * seq
    qb = DSA_QB
    nq = seq // qb
    qcol4 = lambda b, i: (0, 0, b * nq + i)
    return pl.pallas_call(
        functools.partial(_dsa_kernel, seq=seq),
        grid=(bsz, nq),
        in_specs=[
            pl.BlockSpec((IDX_HEADS, IDX_DIM, qb), qcol4),
            pl.BlockSpec((SUBLANES, qb), lambda b, i: (0, b * nq + i)),
            pl.BlockSpec((HEADS, HEAD_DIM, qb), qcol4),
            pl.BlockSpec((seq, IDX_DIM), lambda b, i: (b, 0)),
            pl.BlockSpec((HEADS, seq, HEAD_DIM), lambda b, i: (0, b, 0)),
            pl.BlockSpec((HEADS, HEAD_DIM, seq), lambda b, i: (0, 0, b)),
        ],
        out_specs=pl.BlockSpec((qb, GROUP), lambda b, i: (b * nq + i, 0)),
        out_shape=jax.ShapeDtypeStruct((n, GROUP), jnp.bfloat16),
        scratch_shapes=[
            pltpu.VMEM((seq, qb), jnp.float32),
            pltpu.VMEM((HEADS, HEAD_DIM, qb), jnp.float32),
            pltpu.VMEM((HEADS, DSA_KC, qb), jnp.float32),
            pltpu.VMEM((HEADS, DSA_KC, qb), jnp.bfloat16),
        ],
        compiler_params=pltpu.CompilerParams(
            dimension_semantics=("arbitrary", "arbitrary"),
            vmem_limit_bytes=VMEM_LIMIT),
        name="dsa",
    )(qit, wit, qt, ki, k4, vt)


def _out_ffn_kernel(x_ref, yab_ref, yc_ref, yd_ref, wout_ref, gffn_ref, wg_ref, wu_ref,
                    wd_ref, gfin_ref, o_ref, acc_sc, hf_sc, *, final):
    f32, bf16 = jnp.float32, jnp.bfloat16
    j = pl.program_id(1)

    @pl.when(j == 0)
    def _():
        y = jnp.concatenate([yab_ref[...], yc_ref[...], yd_ref[...]], axis=1)
        x1 = x_ref[...] + jnp.dot(y, wout_ref[...], preferred_element_type=f32)
        acc_sc[...] = x1
        hf_sc[...] = _rmsnorm(x1, gffn_ref[...]).astype(bf16)

    hf = hf_sc[...]
    gate = jnp.dot(hf, wg_ref[...], preferred_element_type=f32)
    up = jnp.dot(hf, wu_ref[...], preferred_element_type=f32)
    act = (gate * jax.nn.sigmoid(gate) * up).astype(bf16)
    acc_sc[...] += jnp.dot(act, wd_ref[...], preferred_element_type=f32)

    @pl.when(j == pl.num_programs(1) - 1)
    def _():
        out = acc_sc[...]
        if final:
            out = _rmsnorm(out, gfin_ref[...])
        o_ref[...] = out


def _out_ffn_call(x2d, yab, yc, yd, wout, gffn, wg, wu, wd, gfin, final):
    n = x2d.shape[0]
    tm, th = FFN_TM, FFN_TH
    row = lambda i, j: (i, 0)
    const = lambda i, j: (0, 0)
    return pl.pallas_call(
        functools.partial(_out_ffn_kernel, final=final),
        grid=(n // tm, FFN_HIDDEN // th),
        in_specs=[
            pl.BlockSpec((tm, D_MODEL), row),
            pl.BlockSpec((tm, 2 * GROUP), row),
            pl.BlockSpec((tm, GROUP), row),
            pl.BlockSpec((tm, GROUP), row),
            pl.BlockSpec((D_MODEL, D_MODEL), const),
            pl.BlockSpec((1, D_MODEL), const),
            pl.BlockSpec((D_MODEL, th), lambda i, j: (0, j)),
            pl.BlockSpec((D_MODEL, th), lambda i, j: (0, j)),
            pl.BlockSpec((th, D_MODEL), lambda i, j: (j, 0)),
            pl.BlockSpec((1, D_MODEL), const),
        ],
        out_specs=pl.BlockSpec((tm, D_MODEL), row),
        out_shape=jax.ShapeDtypeStruct((n, D_MODEL), jnp.float32),
        scratch_shapes=[pltpu.VMEM((tm, D_MODEL), jnp.float32),
                        pltpu.VMEM((tm, D_MODEL), jnp.bfloat16)],
        compiler_params=pltpu.CompilerParams(
            dimension_semantics=("arbitrary", "arbitrary"),
            vmem_limit_bytes=VMEM_LIMIT),
        name="out_ffn",
    )(x2d, yab, yc, yd, wout, gffn, wg, wu, wd, gfin)


def _pack_w_in(w):
    g = GROUP
    cuts = [0, 3 * g, 5 * g, 6 * g, 7 * g, 8 * g, 9 * g, 9 * g + IDX_DIM,
            9 * g + IDX_DIM + IDX_HEADS, 11 * g + IDX_DIM + IDX_HEADS]
    a, b, q, k, v, qi, ki, wi, d = [w[:, cuts[t]:cuts[t + 1]] for t in range(9)]
    z = lambda cols: jnp.zeros((w.shape[0], cols), w.dtype)
    rows = jnp.concatenate([a, b, k, ki, z(LANES - IDX_DIM), d], axis=1)
    trans = jnp.concatenate([q, v, qi, wi, z(WI_ROWS - IDX_HEADS)], axis=1).T
    return rows.astype(jnp.bfloat16), trans.astype(jnp.bfloat16)


def _rope_tables(seq):
    inv_freq = ROPE_THETA ** (-jnp.arange(0, HEAD_DIM, 2, dtype=jnp.float32) / HEAD_DIM)
    ang = jnp.arange(seq, dtype=jnp.float32)[:, None] * inv_freq[None, :]
    c, s = jnp.cos(ang), jnp.sin(ang)
    reps = LANES // HEAD_DIM
    cos = jnp.tile(jnp.concatenate([c, c], axis=1), (1, reps))
    sin = jnp.tile(jnp.concatenate([-s, s], axis=1), (1, reps))
    return cos, sin, c.T, s.T


def kernel(x, g_mix, w_in, w_conv_a, gmlp_ln_g, gmlp_ln_b, w_s, b_s, w_conf, b_conf,
           conf_ln_g, conf_ln_b, w_out, g_ffn, w_gate, w_up, w_down, g_final):
    bsz, seq, d = x.shape
    depth = w_in.shape[0]
    bf16 = jnp.bfloat16
    tables = _rope_tables(seq)
    xs = x
    for l in range(depth):
        w_rows, w_t = _pack_w_in(w_in[l])
        yab, yd, k4, ki, qt, vt, qit, wit = _mix_call(
            xs.reshape(bsz, seq, d), g_mix[l][None, :], w_rows, w_t, tables,
            w_conv_a[l], gmlp_ln_g[l][None, :], gmlp_ln_b[l][None, :], w_s[l],
            jnp.repeat(b_s[l].T, HEAD_DIM, axis=1), w_conf[l], b_conf[l][None, :],
            conf_ln_g[l][None, :], conf_ln_b[l][None, :])
        yc = _dsa_call(qit, wit, qt, ki, k4, vt, bsz, seq)
        xs = _out_ffn_call(
            xs.reshape(bsz * seq, d), yab, yc, yd, w_out[l].astype(bf16), g_ffn[l][None, :],
            w_gate[l].astype(bf16), w_up[l].astype(bf16), w_down[l].astype(bf16),
            g_final[None, :], final=(l == depth - 1))
    return xs.reshape(bsz, seq, d)
```

```python
import functools
import math

import jax
import jax.numpy as jnp
from jax import lax
from jax.experimental import pallas as pl
from jax.experimental.pallas import tpu as pltpu

D_MODEL = 1024
GROUP = 256
HEADS = 4
HEAD_DIM = 64
CHUNK = 128
IDX_HEADS = 4
IDX_DIM = 64
TOPK = 256
SHORT_W = 3
CONF_W = 31
FFN_HIDDEN = 2816
ROPE_THETA = 10000.0
NORM_EPS = 1e-6
LN_EPS = 1e-5
NEG = -1e30
BELOW_NEG = -3.0e38

LANES = 128
SUBLANES = 8
A_HALO = 8
D_HALO = 32

COL_A = 0
COL_B = COL_A + 3 * GROUP
COL_K = COL_B + 2 * GROUP
COL_KI = COL_K + GROUP
COL_D = COL_KI + LANES
ROW_COLS = COL_D + 2 * GROUP
TROW_Q = 0
TROW_V = TROW_Q + GROUP
TROW_QI = TROW_V + GROUP
TROW_WI = TROW_QI + IDX_HEADS * IDX_DIM
WI_ROWS = 16
TROWS = TROW_WI + WI_ROWS

MIX_TM = 512
DSA_QB = 512
DSA_KC = 512
DSA_SUB = 64
CNT_ROWS = 16
TIE_TILE = 128
FFN_TM = 512
FFN_TH = 1408
VMEM_LIMIT = 56 * 1024 * 1024

_NT = (((1,), (1,)), ((), ()))


def _rmsnorm(xf, g):
    ms = jnp.mean(xf * xf, axis=-1, keepdims=True)
    return xf * lax.rsqrt(ms + NORM_EPS) * g


def _layernorm(xf, g, b):
    mu = jnp.mean(xf, axis=-1, keepdims=True)
    xc = xf - mu
    var = jnp.mean(xc * xc, axis=-1, keepdims=True)
    return xc * lax.rsqrt(var + LN_EPS) * g + b


def _rope128(x, cos, sin_signed):
    half = HEAD_DIM // 2
    lane = lax.broadcasted_iota(jnp.int32, x.shape, 1) % HEAD_DIM
    up = pltpu.roll(x, LANES - half, axis=1)
    dn = pltpu.roll(x, half, axis=1)
    return x * cos + jnp.where(lane < half, up, dn) * sin_signed


def _rope_t(x, cos_t, sin_t):
    half = HEAD_DIM // 2
    x1, x2 = x[0:half, :], x[half:HEAD_DIM, :]
    return x1 * cos_t - x2 * sin_t, x2 * cos_t + x1 * sin_t


def _mix_kernel(x_ref, g_ref, w_ref, wt_ref, cos_ref, sin_ref, cost_ref, sint_ref,
                wca_ref, lngb_ref, lnbb_ref, ws_ref, bsb_ref, wcf_ref, bcf_ref,
                lngd_ref, lnbd_ref,
                yab_ref, yd_ref, k4_ref, ki_ref, qt_ref, vt_ref, qit_ref, wit_ref,
                abuf, dbuf, sbuf):
    tm = x_ref.shape[0]
    f32, bf16 = jnp.float32, jnp.bfloat16
    half = HEAD_DIM // 2

    @pl.when(pl.program_id(1) == 0)
    def _():
        abuf[0:A_HALO, :] = jnp.zeros((A_HALO, GROUP), f32)
        dbuf[0:D_HALO, :] = jnp.zeros((D_HALO, GROUP), f32)

    h = _rmsnorm(x_ref[...], g_ref[...]).astype(bf16)

    def proj(lo, hi):
        return jnp.dot(h, w_ref[:, lo:hi], preferred_element_type=f32)

    def proj_t(lo, hi):
        return lax.dot_general(wt_ref[lo:hi, :], h, _NT, preferred_element_type=f32)

    za = proj(COL_A, COL_B)
    abuf[A_HALO:A_HALO + tm, :] = za[:, GROUP:2 * GROUP] * za[:, 2 * GROUP:3 * GROUP]
    conv = abuf[pl.ds(A_HALO - 2, tm), :] * wca_ref[0:1, :]
    conv = conv + abuf[pl.ds(A_HALO - 1, tm), :] * wca_ref[1:2, :]
    conv = conv + abuf[pl.ds(A_HALO, tm), :] * wca_ref[2:3, :]
    yab_ref[:, 0:GROUP] = (za[:, 0:GROUP] * conv).astype(bf16)
    abuf[0:A_HALO, :] = abuf[tm:tm + A_HALO, :]

    zb = proj(COL_B, COL_K)
    vn = _layernorm(zb[:, GROUP:2 * GROUP], lngb_ref[...], lnbb_ref[...]).astype(bf16)
    r_i = lax.broadcasted_iota(jnp.int32, (CHUNK, CHUNK), 0)
    c_i = lax.broadcasted_iota(jnp.int32, (CHUNK, CHUNK), 1)
    wsm = [jnp.where(r_i >= c_i, ws_ref[hd], 0.0).astype(bf16) for hd in range(HEADS)]
    lane_head = lax.broadcasted_iota(jnp.int32, (CHUNK, GROUP), 1) // HEAD_DIM
    for c in range(tm // CHUNK):
        rows = slice(c * CHUNK, (c + 1) * CHUNK)
        vc = vn[rows, :]
        mixed = bsb_ref[...]
        for hd in range(HEADS):
            full = jnp.dot(wsm[hd], vc, preferred_element_type=f32)
            mixed = mixed + jnp.where(lane_head == hd, full, 0.0)
        yab_ref[rows, GROUP:2 * GROUP] = (zb[rows, 0:GROUP] * mixed).astype(bf16)

    cos = cos_ref[...]
    sin = sin_ref[...]
    zk = proj(COL_K, COL_D)
    for pair in range(2):
        kp = _rope128(zk[:, pair * LANES:(pair + 1) * LANES], cos, sin)
        for sub in range(2):
            k4_ref[2 * pair + sub] = kp[:, sub * HEAD_DIM:(sub + 1) * HEAD_DIM].astype(bf16)
    kip = _rope128(zk[:, GROUP:GROUP + LANES], cos, sin)
    ki_ref[...] = kip[:, 0:IDX_DIM].astype(bf16)

    cos_t = cost_ref[...]
    sin_t = sint_ref[...]
    q_scale = HEAD_DIM ** -0.5 * math.log2(math.e)
    zq = proj_t(TROW_Q, TROW_V)
    zqi = proj_t(TROW_QI, TROW_WI)
    for hd in range(HEADS):
        rows = slice(hd * HEAD_DIM, (hd + 1) * HEAD_DIM)
        o1, o2 = _rope_t(zq[rows, :], cos_t, sin_t)
        qt_ref[hd, 0:half, :] = (o1 * q_scale).astype(bf16)
        qt_ref[hd, half:HEAD_DIM, :] = (o2 * q_scale).astype(bf16)
        o1, o2 = _rope_t(zqi[rows, :], cos_t, sin_t)
        qit_ref[hd, 0:half, :] = o1.astype(bf16)
        qit_ref[hd, half:HEAD_DIM, :] = o2.astype(bf16)
    zv = proj_t(TROW_V, TROW_QI)
    for hd in range(HEADS):
        vt_ref[hd] = zv[hd * HEAD_DIM:(hd + 1) * HEAD_DIM, :].astype(bf16)
    zw = proj_t(TROW_WI, TROWS)
    wit_ref[...] = zw[0:SUBLANES, :] * (IDX_HEADS ** -0.5 * IDX_DIM ** -0.5)

    zd = proj(COL_D, ROW_COLS)
    dbuf[D_HALO:D_HALO + tm, :] = zd[:, 0:GROUP] * jax.nn.sigmoid(zd[:, GROUP:2 * GROUP])
    acc = jnp.broadcast_to(bcf_ref[...], (tm, GROUP))
    first = D_HALO - (CONF_W - 1)
    for r in range(SUBLANES):
        taps = [t for t in range(CONF_W) if (first + t) % SUBLANES == r]
        span = (first + taps[-1]) - (first + taps[0]) + tm
        sbuf[0:span, :] = dbuf[pl.ds(first + taps[0], span), :]
        for t in taps:
            lo = t - taps[0]
            acc = acc + sbuf[lo:lo + tm, :] * wcf_ref[t:t + 1, :]
    y = _layernorm(acc, lngd_ref[...], lnbd_ref[...])
    yd_ref[...] = (y * jax.nn.sigmoid(y)).astype(bf16)
    dbuf[0:D_HALO, :] = dbuf[tm:tm + D_HALO, :]


def _mix_call(x, g, w_rows, w_t, tables, wca, lngb, lnbb, ws, bsb, wcf, bcf, lngd, lnbd):
    bsz, seq, _ = x.shape
    n = bsz * seq
    tm = MIX_TM
    nt = seq // tm
    bf16 = jnp.bfloat16
    half = HEAD_DIM // 2
    cos, sin, cos_t, sin_t = tables
    row = lambda b, i: (b * nt + i, 0)
    row4 = lambda b, i: (0, b * nt + i, 0)
    col = lambda b, i: (0, b * nt + i)
    col4 = lambda b, i: (0, 0, b * nt + i)
    const2 = lambda b, i: (0, 0)
    const3 = lambda b, i: (0, 0, 0)
    in_specs = [
        pl.BlockSpec((None, tm, D_MODEL), lambda b, i: (b, i, 0)),
        pl.BlockSpec((1, D_MODEL), const2),
        pl.BlockSpec((D_MODEL, ROW_COLS), const2),
        pl.BlockSpec((TROWS, D_MODEL), const2),
        pl.BlockSpec((tm, LANES), lambda b, i: (i, 0)),
        pl.BlockSpec((tm, LANES), lambda b, i: (i, 0)),
        pl.BlockSpec((half, tm), lambda b, i: (0, i)),
        pl.BlockSpec((half, tm), lambda b, i: (0, i)),
        pl.BlockSpec((SHORT_W, GROUP), const2),
        pl.BlockSpec((1, GROUP), const2),
        pl.BlockSpec((1, GROUP), const2),
        pl.BlockSpec((HEADS, CHUNK, CHUNK), const3),
        pl.BlockSpec((CHUNK, GROUP), const2),
        pl.BlockSpec((CONF_W, GROUP), const2),
        pl.BlockSpec((1, GROUP), const2),
        pl.BlockSpec((1, GROUP), const2),
        pl.BlockSpec((1, GROUP), const2),
    ]
    out_shape = [
        jax.ShapeDtypeStruct((n, 2 * GROUP), bf16),
        jax.ShapeDtypeStruct((n, GROUP), bf16),
        jax.ShapeDtypeStruct((HEADS, n, HEAD_DIM), bf16),
        jax.ShapeDtypeStruct((n, IDX_DIM), bf16),
        jax.ShapeDtypeStruct((HEADS, HEAD_DIM, n), bf16),
        jax.ShapeDtypeStruct((HEADS, HEAD_DIM, n), bf16),
        jax.ShapeDtypeStruct((IDX_HEADS, IDX_DIM, n), bf16),
        jax.ShapeDtypeStruct((SUBLANES, n), jnp.float32),
    ]
    out_specs = [
        pl.BlockSpec((tm, 2 * GROUP), row),
        pl.BlockSpec((tm, GROUP), row),
        pl.BlockSpec((HEADS, tm, HEAD_DIM), row4),
        pl.BlockSpec((tm, IDX_DIM), row),
        pl.BlockSpec((HEADS, HEAD_DIM, tm), col4),
        pl.BlockSpec((HEADS, HEAD_DIM, tm), col4),
        pl.BlockSpec((IDX_HEADS, IDX_DIM, tm), col4),
        pl.BlockSpec((SUBLANES, tm), col),
    ]
    return pl.pallas_call(
        _mix_kernel,
        grid=(bsz, nt),
        in_specs=in_specs,
        out_specs=out_specs,
        out_shape=out_shape,
        scratch_shapes=[pltpu.VMEM((tm + A_HALO, GROUP), jnp.float32),
                        pltpu.VMEM((tm + D_HALO, GROUP), jnp.float32),
                        pltpu.VMEM((tm + D_HALO, GROUP), jnp.float32)],
        compiler_params=pltpu.CompilerParams(
            dimension_semantics=("arbitrary", "arbitrary"),
            vmem_limit_bytes=VMEM_LIMIT),
        name="mix",
    )(x, g, w_rows, w_t, cos, sin, cos_t, sin_t, wca, lngb, lnbb, ws, bsb, wcf, bcf,
      lngd, lnbd)


def _threshold_bits(trial):
    bits = jnp.where(trial < 0, trial & jnp.int32(0x7FFFFFFF), ~trial)
    return pltpu.bitcast(bits, jnp.float32)


def _dsa_kernel(qit_ref, wit_ref, qt_ref, ki_ref, k4_ref, vt_ref, o_ref,
                sc_ref, acc_ref, s_ref, p_ref, *, seq):
    f32, bf16, i32 = jnp.float32, jnp.bfloat16, jnp.int32
    qb = wit_ref.shape[1]
    kc, sub = DSA_KC, DSA_SUB
    ndiag = qb // kc
    i = pl.program_id(1)
    nfull = i * ndiag
    nchunks = nfull + ndiag
    n_outside = seq - nchunks * kc
    key_i = lax.broadcasted_iota(i32, (sub, qb), 0)
    qry_i = lax.broadcasted_iota(i32, (sub, qb), 1)

    def causal(x, d, r):
        return jnp.where(key_i + (d * kc + r) <= qry_i, x, NEG)

    def run_chunks(chunk_fn, carry):
        carry = lax.fori_loop(0, nfull, lambda c, cr: chunk_fn(c, None, cr), carry)
        for d in range(ndiag):
            carry = chunk_fn(nfull + d, d, carry)
        return carry

    def score_chunk(c, d, carry):
        off = pl.multiple_of(c * kc, kc)
        for t in range(kc // sub):
            r0 = off + t * sub
            kis = ki_ref[pl.ds(r0, sub), :]
            sc = None
            for j in range(IDX_HEADS):
                rel = jnp.dot(kis, qit_ref[j], preferred_element_type=f32)
                term = jnp.maximum(rel, 0.0) * wit_ref[j:j + 1, :]
                sc = term if sc is None else sc + term
            if d is not None:
                sc = causal(sc, d, t * sub)
            sc_ref[pl.ds(r0, sub), :] = sc
        return carry

    run_chunks(score_chunk, 0)

    def count(indicator):
        def body(c, acc):
            off = pl.multiple_of(c * kc, kc)
            for t in range(kc // CNT_ROWS):
                r0 = off + t * CNT_ROWS
                acc = acc + indicator(sc_ref[pl.ds(r0, CNT_ROWS), :], r0)
            return acc
        acc = lax.fori_loop(0, nchunks, body, jnp.zeros((CNT_ROWS, qb), i32))
        return jnp.sum(acc, axis=0, keepdims=True)

    def count_ge(thr):
        inside = count(lambda s, r0: jnp.where(s >= thr, 1, 0))
        return inside + jnp.where(thr <= NEG, n_outside, 0)

    def bit_step(step, state):
        prefix, c_prefix = state
        trial = prefix | jnp.left_shift(jnp.int32(1), 31 - step)
        cnt = count_ge(_threshold_bits(trial))
        accept = cnt >= TOPK
        return jnp.where(accept, trial, prefix), jnp.where(accept, cnt, c_prefix)

    prefix, c_ge = lax.fori_loop(
        0, 32, bit_step, (jnp.zeros((1, qb), i32), jnp.full((1, qb), seq, i32)))
    vstar = _threshold_bits(prefix)

    @pl.when(jnp.max(c_ge) > TOPK)
    def _():
        c_gt = (count(lambda s, r0: jnp.where(s > vstar, 1, 0))
                + jnp.where(vstar < NEG, n_outside, 0))
        want = (TOPK - c_gt).astype(f32)
        tt = TIE_TILE
        tri = jnp.where(lax.broadcasted_iota(i32, (tt, tt), 0)
                        >= lax.broadcasted_iota(i32, (tt, tt), 1), 1.0, 0.0).astype(bf16)

        def tie_body(c, seen):
            off = pl.multiple_of(c * kc, kc)
            running = []
            for t in range(kc // tt):
                s = sc_ref[pl.ds(off + t * tt, tt), :]
                tie = jnp.where(s == vstar, 1.0, 0.0).astype(bf16)
                running.append(jnp.dot(tri, tie, preferred_element_type=f32))
            for t in range(kc // tt):
                s = sc_ref[pl.ds(off + t * tt, tt), :]
                rank = seen + running[t]
                sc_ref[pl.ds(off + t * tt, tt), :] = jnp.where(
                    s == vstar, jnp.where(rank <= want, s, BELOW_NEG), s)
                seen = seen + running[t][tt - 1:tt, :]
            return seen

        lax.fori_loop(0, nchunks, tie_body, jnp.zeros((1, qb), f32))

    def fold8(x, op):
        return op(x.reshape(sub // SUBLANES, SUBLANES, qb), axis=0)

    def cap_chunk(c, d, carry):
        off = pl.multiple_of(c * kc, kc)
        for t in range(kc // sub):
            r = t * sub
            cap = jnp.where(sc_ref[pl.ds(off + r, sub), :] >= vstar, jnp.inf, NEG)
            if d is not None:
                cap = causal(cap, d, r)
            sc_ref[pl.ds(off + r, sub), :] = cap
        return carry

    run_chunks(cap_chunk, 0)
    acc_ref[...] = jnp.zeros(acc_ref.shape, f32)

    def attend_chunk(c, d, carry):
        ms, ls = carry
        off = pl.multiple_of(c * kc, kc)
        new_ms, new_ls = list(ms), list(ls)

        def logits_stage(hd):
            mpart = None
            for t in range(kc // sub):
                r = t * sub
                s = jnp.dot(k4_ref[hd, pl.ds(off + r, sub), :], qt_ref[hd],
                            preferred_element_type=f32)
                s = jnp.minimum(s, sc_ref[pl.ds(off + r, sub), :])
                s_ref[hd, r:r + sub, :] = s
                m8 = fold8(s, jnp.max)
                mpart = m8 if mpart is None else jnp.maximum(mpart, m8)
            new_ms[hd] = jnp.maximum(ms[hd], jnp.max(mpart, axis=0, keepdims=True))

        def value_stage(hd):
            lpart = None
            for t in range(kc // sub):
                r = t * sub
                p = jnp.exp2(s_ref[hd, r:r + sub, :] - new_ms[hd])
                l8 = fold8(p, jnp.sum)
                lpart = l8 if lpart is None else lpart + l8
                p_ref[hd, r:r + sub, :] = p.astype(bf16)
            alpha = jnp.exp2(ms[hd] - new_ms[hd])
            new_ls[hd] = alpha * ls[hd] + jnp.sum(lpart, axis=0, keepdims=True)
            pv = jnp.dot(vt_ref[hd, :, pl.ds(off, kc)], p_ref[hd], preferred_element_type=f32)
            acc_ref[hd] = acc_ref[hd] * alpha + pv

        logits_stage(0)
        for hd in range(HEADS):
            if hd + 1 < HEADS:
                logits_stage(hd + 1)
            value_stage(hd)
        return tuple(new_ms), tuple(new_ls)

    init = (tuple(jnp.full((1, qb), NEG, f32) for _ in range(HEADS)),
            tuple(jnp.zeros((1, qb), f32) for _ in range(HEADS)))
    _, ls = run_chunks(attend_chunk, init)

    out_t = jnp.concatenate([acc_ref[hd] / ls[hd] for hd in range(HEADS)], axis=0)
    o_ref[...] = out_t.T.astype(o_ref.dtype)


def _dsa_call(qit, wit, qt, ki, k4, vt, bsz, seq):
    n = bsz * seq
    qb = DSA_QB
    nq = seq // qb
    qcol4 = lambda b, i: (0, 0, b * nq + i)
    return pl.pallas_call(
        functools.partial(_dsa_kernel, seq=seq),
        grid=(bsz, nq),
        in_specs=[
            pl.BlockSpec((IDX_HEADS, IDX_DIM, qb), qcol4),
            pl.BlockSpec((SUBLANES, qb), lambda b, i: (0, b * nq + i)),
            pl.BlockSpec((HEADS, HEAD_DIM, qb), qcol4),
            pl.BlockSpec((seq, IDX_DIM), lambda b, i: (b, 0)),
            pl.BlockSpec((HEADS, seq, HEAD_DIM), lambda b, i: (0, b, 0)),
            pl.BlockSpec((HEADS, HEAD_DIM, seq), lambda b, i: (0, 0, b)),
        ],
        out_specs=pl.BlockSpec((qb, GROUP), lambda b, i: (b * nq + i, 0)),
        out_shape=jax.ShapeDtypeStruct((n, GROUP), jnp.bfloat16),
        scratch_shapes=[
            pltpu.VMEM((seq, qb), jnp.float32),
            pltpu.VMEM((HEADS, HEAD_DIM, qb), jnp.float32),
            pltpu.VMEM((HEADS, DSA_KC, qb), jnp.float32),
            pltpu.VMEM((HEADS, DSA_KC, qb), jnp.bfloat16),
        ],
        compiler_params=pltpu.CompilerParams(
            dimension_semantics=("arbitrary", "arbitrary"),
            vmem_limit_bytes=VMEM_LIMIT),
        name="dsa",
    )(qit, wit, qt, ki, k4, vt)


def _out_ffn_kernel(x_ref, yab_ref, yc_ref, yd_ref, wout_ref, gffn_ref, wg_ref, wu_ref,
                    wd_ref, gfin_ref, o_ref, acc_sc, hf_sc, *, final):
    f32, bf16 = jnp.float32, jnp.bfloat16
    j = pl.program_id(1)

    @pl.when(j == 0)
    def _():
        y = jnp.concatenate([yab_ref[...], yc_ref[...], yd_ref[...]], axis=1)
        x1 = x_ref[...] + jnp.dot(y, wout_ref[...], preferred_element_type=f32)
        acc_sc[...] = x1
        hf_sc[...] = _rmsnorm(x1, gffn_ref[...]).astype(bf16)

    hf = hf_sc[...]
    gate = jnp.dot(hf, wg_ref[...], preferred_element_type=f32)
    up = jnp.dot(hf, wu_ref[...], preferred_element_type=f32)
    act = (gate * jax.nn.sigmoid(gate) * up).astype(bf16)
    acc_sc[...] += jnp.dot(act, wd_ref[...], preferred_element_type=f32)

    @pl.when(j == pl.num_programs(1) - 1)
    def _():
        out = acc_sc[...]
        if final:
            out = _rmsnorm(out, gfin_ref[...])
        o_ref[...] = out


def _out_ffn_call(x2d, yab, yc, yd, wout, gffn, wg, wu, wd, gfin, final):
    n = x2d.shape[0]
    tm, th = FFN_TM, FFN_TH
    row = lambda i, j: (i, 0)
    const = lambda i, j: (0, 0)
    return pl.pallas_call(
        functools.partial(_out_ffn_kernel, final=final),
        grid=(n // tm, FFN_HIDDEN // th),
        in_specs=[
            pl.BlockSpec((tm, D_MODEL), row),
            pl.BlockSpec((tm, 2 * GROUP), row),
            pl.BlockSpec((tm, GROUP), row),
            pl.BlockSpec((tm, GROUP), row),
            pl.BlockSpec((D_MODEL, D_MODEL), const),
            pl.BlockSpec((1, D_MODEL), const),
            pl.BlockSpec((D_MODEL, th), lambda i, j: (0, j)),
            pl.BlockSpec((D_MODEL, th), lambda i, j: (0, j)),
            pl.BlockSpec((th, D_MODEL), lambda i, j: (j, 0)),
            pl.BlockSpec((1, D_MODEL), const),
        ],
        out_specs=pl.BlockSpec((tm, D_MODEL), row),
        out_shape=jax.ShapeDtypeStruct((n, D_MODEL), jnp.float32),
        scratch_shapes=[pltpu.VMEM((tm, D_MODEL), jnp.float32),
                        pltpu.VMEM((tm, D_MODEL), jnp.bfloat16)],
        compiler_params=pltpu.CompilerParams(
            dimension_semantics=("arbitrary", "arbitrary"),
            vmem_limit_bytes=VMEM_LIMIT),
        name="out_ffn",
    )(x2d, yab, yc, yd, wout, gffn, wg, wu, wd, gfin)


def _pack_w_in(w):
    g = GROUP
    cuts = [0, 3 * g, 5 * g, 6 * g, 7 * g, 8 * g, 9 * g, 9 * g + IDX_DIM,
            9 * g + IDX_DIM + IDX_HEADS, 11 * g + IDX_DIM + IDX_HEADS]
    a, b, q, k, v, qi, ki, wi, d = [w[:, cuts[t]:cuts[t + 1]] for t in range(9)]
    z = lambda cols: jnp.zeros((w.shape[0], cols), w.dtype)
    rows = jnp.concatenate([a, b, k, ki, z(LANES - IDX_DIM), d], axis=1)
    trans = jnp.concatenate([q, v, qi, wi, z(WI_ROWS - IDX_HEADS)], axis=1).T
    return rows.astype(jnp.bfloat16), trans.astype(jnp.bfloat16)


def _rope_tables(seq):
    inv_freq = ROPE_THETA ** (-jnp.arange(0, HEAD_DIM, 2, dtype=jnp.float32) / HEAD_DIM)
    ang = jnp.arange(seq, dtype=jnp.float32)[:, None] * inv_freq[None, :]
    c, s = jnp.cos(ang), jnp.sin(ang)
    reps = LANES // HEAD_DIM
    cos = jnp.tile(jnp.concatenate([c, c], axis=1), (1, reps))
    sin = jnp.tile(jnp.concatenate([-s, s], axis=1), (1, reps))
    return cos, sin, c.T, s.T


def kernel(x, g_mix, w_in, w_conv_a, gmlp_ln_g, gmlp_ln_b, w_s, b_s, w_conf, b_conf,
           conf_ln_g, conf_ln_b, w_out, g_ffn, w_gate, w_up, w_down, g_final):
    bsz, seq, d = x.shape
    depth = w_in.shape[0]
    bf16 = jnp.bfloat16
    tables = _rope_tables(seq)
    xs = x
    for l in range(depth):
        w_rows, w_t = _pack_w_in(w_in[l])
        yab, yd, k4, ki, qt, vt, qit, wit = _mix_call(
            xs.reshape(bsz, seq, d), g_mix[l][None, :], w_rows, w_t, tables,
            w_conv_a[l], gmlp_ln_g[l][None, :], gmlp_ln_b[l][None, :], w_s[l],
            jnp.repeat(b_s[l].T, HEAD_DIM, axis=1), w_conf[l], b_conf[l][None, :],
            conf_ln_g[l][None, :], conf_ln_b[l][None, :])
        yc = _dsa_call(qit, wit, qt, ki, k4, vt, bsz, seq)
        xs = _out_ffn_call(
            xs.reshape(bsz * seq, d), yab, yc, yd, w_out[l].astype(bf16), g_ffn[l][None, :],
            w_gate[l].astype(bf16), w_up[l].astype(bf16), w_down[l].astype(bf16),
            g_final[None, :], final=(l == depth - 1))
    return xs.reshape(bsz, seq, d)
```

```python
import functools
import math

import jax
import jax.numpy as jnp
from jax import lax
from jax.experimental import pallas as pl
from jax.experimental.pallas import tpu as pltpu

D_MODEL = 1024
GROUP = 256
HEADS = 4
HEAD_DIM = 64
CHUNK = 128
IDX_HEADS = 4
IDX_DIM = 64
TOPK = 256
SHORT_W = 3
CONF_W = 31
FFN_HIDDEN = 2816
ROPE_THETA = 10000.0
NORM_EPS = 1e-6
LN_EPS = 1e-5
NEG = -1e30
BELOW_NEG = -3.0e38

LANES = 128
SUBLANES = 8
A_HALO = 8
D_HALO = 32

COL_A = 0
COL_B = COL_A + 3 * GROUP
COL_K = COL_B + 2 * GROUP
COL_KI = COL_K + GROUP
COL_D = COL_KI + LANES
ROW_COLS = COL_D + 2 * GROUP
TROW_Q = 0
TROW_V = TROW_Q + GROUP
TROW_QI = TROW_V + GROUP
TROW_WI = TROW_QI + IDX_HEADS * IDX_DIM
WI_ROWS = 16
TROWS = TROW_WI + WI_ROWS

MIX_TM = 512
DSA_QB = 512
DSA_KC = 512
DSA_SUB = 64
CNT_ROWS = 16
TIE_TILE = 128
BF16_ROWS = 16
SHIFT_LIMIT = 45.0
FFN_TM = 512
FFN_TH = 1408
VMEM_LIMIT = 56 * 1024 * 1024

_NT = (((1,), (1,)), ((), ()))


def _rmsnorm(xf, g):
    ms = jnp.mean(xf * xf, axis=-1, keepdims=True)
    return xf * lax.rsqrt(ms + NORM_EPS) * g


def _layernorm(xf, g, b):
    mu = jnp.mean(xf, axis=-1, keepdims=True)
    xc = xf - mu
    var = jnp.mean(xc * xc, axis=-1, keepdims=True)
    return xc * lax.rsqrt(var + LN_EPS) * g + b


def _rope128(x, cos, sin_signed):
    half = HEAD_DIM // 2
    lane = lax.broadcasted_iota(jnp.int32, x.shape, 1) % HEAD_DIM
    up = pltpu.roll(x, LANES - half, axis=1)
    dn = pltpu.roll(x, half, axis=1)
    return x * cos + jnp.where(lane < half, up, dn) * sin_signed


def _rope_t(x, cos_t, sin_t):
    half = HEAD_DIM // 2
    x1, x2 = x[0:half, :], x[half:HEAD_DIM, :]
    return x1 * cos_t - x2 * sin_t, x2 * cos_t + x1 * sin_t


def _mix_kernel(x_ref, g_ref, w_ref, wt_ref, cos_ref, sin_ref, cost_ref, sint_ref,
                wca_ref, lngb_ref, lnbb_ref, ws_ref, bsb_ref, wcf_ref, bcf_ref,
                lngd_ref, lnbd_ref,
                yab_ref, yd_ref, k4_ref, ki_ref, qt_ref, vt_ref, qit_ref, wit_ref,
                abuf, dbuf, sbuf):
    tm = x_ref.shape[0]
    f32, bf16 = jnp.float32, jnp.bfloat16
    half = HEAD_DIM // 2

    @pl.when(pl.program_id(1) == 0)
    def _():
        abuf[0:A_HALO, :] = jnp.zeros((A_HALO, GROUP), f32)
        dbuf[0:D_HALO, :] = jnp.zeros((D_HALO, GROUP), f32)

    h = _rmsnorm(x_ref[...], g_ref[...]).astype(bf16)

    def proj(lo, hi):
        return jnp.dot(h, w_ref[:, lo:hi], preferred_element_type=f32)

    def proj_t(lo, hi):
        return lax.dot_general(wt_ref[lo:hi, :], h, _NT, preferred_element_type=f32)

    za = proj(COL_A, COL_B)
    abuf[A_HALO:A_HALO + tm, :] = za[:, GROUP:2 * GROUP] * za[:, 2 * GROUP:3 * GROUP]
    conv = abuf[pl.ds(A_HALO - 2, tm), :] * wca_ref[0:1, :]
    conv = conv + abuf[pl.ds(A_HALO - 1, tm), :] * wca_ref[1:2, :]
    conv = conv + abuf[pl.ds(A_HALO, tm), :] * wca_ref[2:3, :]
    yab_ref[:, 0:GROUP] = (za[:, 0:GROUP] * conv).astype(bf16)
    abuf[0:A_HALO, :] = abuf[tm:tm + A_HALO, :]

    zb = proj(COL_B, COL_K)
    vn = _layernorm(zb[:, GROUP:2 * GROUP], lngb_ref[...], lnbb_ref[...]).astype(bf16)
    r_i = lax.broadcasted_iota(jnp.int32, (CHUNK, CHUNK), 0)
    c_i = lax.broadcasted_iota(jnp.int32, (CHUNK, CHUNK), 1)
    wsm = [jnp.where(r_i >= c_i, ws_ref[hd], 0.0).astype(bf16) for hd in range(HEADS)]
    lane_head = lax.broadcasted_iota(jnp.int32, (CHUNK, GROUP), 1) // HEAD_DIM
    for c in range(tm // CHUNK):
        rows = slice(c * CHUNK, (c + 1) * CHUNK)
        vc = vn[rows, :]
        mixed = bsb_ref[...]
        for hd in range(HEADS):
            full = jnp.dot(wsm[hd], vc, preferred_element_type=f32)
            mixed = mixed + jnp.where(lane_head == hd, full, 0.0)
        yab_ref[rows, GROUP:2 * GROUP] = (zb[rows, 0:GROUP] * mixed).astype(bf16)

    cos = cos_ref[...]
    sin = sin_ref[...]
    zk = proj(COL_K, COL_D)
    lane = lax.broadcasted_iota(jnp.int32, (tm, LANES), 1)
    bias_col = jnp.where(lane == HEAD_DIM, 1.0, 0.0)
    for pair in range(2):
        kp = _rope128(zk[:, pair * LANES:(pair + 1) * LANES], cos, sin)
        for sub in range(2):
            kh = kp if sub == 0 else pltpu.roll(kp, HEAD_DIM, axis=1)
            k4_ref[2 * pair + sub] = jnp.where(lane < HEAD_DIM, kh, bias_col).astype(bf16)
    kip = _rope128(zk[:, GROUP:GROUP + LANES], cos, sin)
    ki_ref[...] = kip[:, 0:IDX_DIM].astype(bf16)

    cos_t = cost_ref[...]
    sin_t = sint_ref[...]
    q_scale = HEAD_DIM ** -0.5 * math.log2(math.e)
    zq = proj_t(TROW_Q, TROW_V)
    zqi = proj_t(TROW_QI, TROW_WI)
    for hd in range(HEADS):
        rows = slice(hd * HEAD_DIM, (hd + 1) * HEAD_DIM)
        o1, o2 = _rope_t(zq[rows, :], cos_t, sin_t)
        qt_ref[hd, 0:half, :] = (o1 * q_scale).astype(bf16)
        qt_ref[hd, half:HEAD_DIM, :] = (o2 * q_scale).astype(bf16)
        o1, o2 = _rope_t(zqi[rows, :], cos_t, sin_t)
        qit_ref[hd, 0:half, :] = o1.astype(bf16)
        qit_ref[hd, half:HEAD_DIM, :] = o2.astype(bf16)
    zv = proj_t(TROW_V, TROW_QI)
    for hd in range(HEADS):
        vt_ref[hd] = zv[hd * HEAD_DIM:(hd + 1) * HEAD_DIM, :].astype(bf16)
    zw = proj_t(TROW_WI, TROWS)
    wit_ref[...] = zw[0:SUBLANES, :] * (IDX_HEADS ** -0.5 * IDX_DIM ** -0.5)

    zd = proj(COL_D, ROW_COLS)
    dbuf[D_HALO:D_HALO + tm, :] = zd[:, 0:GROUP] * jax.nn.sigmoid(zd[:, GROUP:2 * GROUP])
    acc = jnp.broadcast_to(bcf_ref[...], (tm, GROUP))
    first = D_HALO - (CONF_W - 1)
    for r in range(SUBLANES):
        taps = [t for t in range(CONF_W) if (first + t) % SUBLANES == r]
        span = (first + taps[-1]) - (first + taps[0]) + tm
        sbuf[0:span, :] = dbuf[pl.ds(first + taps[0], span), :]
        for t in taps:
            lo = t - taps[0]
            acc = acc + sbuf[lo:lo + tm, :] * wcf_ref[t:t + 1, :]
    y = _layernorm(acc, lngd_ref[...], lnbd_ref[...])
    yd_ref[...] = (y * jax.nn.sigmoid(y)).astype(bf16)
    dbuf[0:D_HALO, :] = dbuf[tm:tm + D_HALO, :]


def _mix_call(x, g, w_rows, w_t, tables, wca, lngb, lnbb, ws, bsb, wcf, bcf, lngd, lnbd):
    bsz, seq, _ = x.shape
    n = bsz * seq
    tm = MIX_TM
    nt = seq // tm
    bf16 = jnp.bfloat16
    half = HEAD_DIM // 2
    cos, sin, cos_t, sin_t = tables
    row = lambda b, i: (b * nt + i, 0)
    row4 = lambda b, i: (0, b * nt + i, 0)
    col = lambda b, i: (0, b * nt + i)
    col4 = lambda b, i: (0, 0, b * nt + i)
    const2 = lambda b, i: (0, 0)
    const3 = lambda b, i: (0, 0, 0)
    in_specs = [
        pl.BlockSpec((None, tm, D_MODEL), lambda b, i: (b, i, 0)),
        pl.BlockSpec((1, D_MODEL), const2),
        pl.BlockSpec((D_MODEL, ROW_COLS), const2),
        pl.BlockSpec((TROWS, D_MODEL), const2),
        pl.BlockSpec((tm, LANES), lambda b, i: (i, 0)),
        pl.BlockSpec((tm, LANES), lambda b, i: (i, 0)),
        pl.BlockSpec((half, tm), lambda b, i: (0, i)),
        pl.BlockSpec((half, tm), lambda b, i: (0, i)),
        pl.BlockSpec((SHORT_W, GROUP), const2),
        pl.BlockSpec((1, GROUP), const2),
        pl.BlockSpec((1, GROUP), const2),
        pl.BlockSpec((HEADS, CHUNK, CHUNK), const3),
        pl.BlockSpec((CHUNK, GROUP), const2),
        pl.BlockSpec((CONF_W, GROUP), const2),
        pl.BlockSpec((1, GROUP), const2),
        pl.BlockSpec((1, GROUP), const2),
        pl.BlockSpec((1, GROUP), const2),
    ]
    out_shape = [
        jax.ShapeDtypeStruct((n, 2 * GROUP), bf16),
        jax.ShapeDtypeStruct((n, GROUP), bf16),
        jax.ShapeDtypeStruct((HEADS, n, LANES), bf16),
        jax.ShapeDtypeStruct((n, IDX_DIM), bf16),
        jax.ShapeDtypeStruct((HEADS, HEAD_DIM, n), bf16),
        jax.ShapeDtypeStruct((HEADS, HEAD_DIM, n), bf16),
        jax.ShapeDtypeStruct((IDX_HEADS, IDX_DIM, n), bf16),
        jax.ShapeDtypeStruct((SUBLANES, n), jnp.float32),
    ]
    out_specs = [
        pl.BlockSpec((tm, 2 * GROUP), row),
        pl.BlockSpec((tm, GROUP), row),
        pl.BlockSpec((HEADS, tm, LANES), row4),
        pl.BlockSpec((tm, IDX_DIM), row),
        pl.BlockSpec((HEADS, HEAD_DIM, tm), col4),
        pl.BlockSpec((HEADS, HEAD_DIM, tm), col4),
        pl.BlockSpec((IDX_HEADS, IDX_DIM, tm), col4),
        pl.BlockSpec((SUBLANES, tm), col),
    ]
    return pl.pallas_call(
        _mix_kernel,
        grid=(bsz, nt),
        in_specs=in_specs,
        out_specs=out_specs,
        out_shape=out_shape,
        scratch_shapes=[pltpu.VMEM((tm + A_HALO, GROUP), jnp.float32),
                        pltpu.VMEM((tm + D_HALO, GROUP), jnp.float32),
                        pltpu.VMEM((tm + D_HALO, GROUP), jnp.float32)],
        compiler_params=pltpu.CompilerParams(
            dimension_semantics=("arbitrary", "arbitrary"),
            vmem_limit_bytes=VMEM_LIMIT),
        name="mix",
    )(x, g, w_rows, w_t, cos, sin, cos_t, sin_t, wca, lngb, lnbb, ws, bsb, wcf, bcf,
      lngd, lnbd)


def _threshold_bits(trial):
    bits = jnp.where(trial < 0, trial & jnp.int32(0x7FFFFFFF), ~trial)
    return pltpu.bitcast(bits, jnp.float32)


def _dsa_kernel(qit_ref, wit_ref, qt_ref, ki_ref, k4_ref, vt_ref, o_ref,
                sc_ref, acc_ref, s_ref, p_ref, qa_ref, l_ref, kmax_ref, *, seq):
    f32, bf16, i32 = jnp.float32, jnp.bfloat16, jnp.int32
    qb = wit_ref.shape[1]
    kc, sub = DSA_KC, DSA_SUB
    ndiag = qb // kc
    i = pl.program_id(1)
    nfull = i * ndiag
    nchunks = nfull + ndiag
    n_outside = seq - nchunks * kc
    key_i = lax.broadcasted_iota(i32, (sub, qb), 0)
    qry_i = lax.broadcasted_iota(i32, (sub, qb), 1)

    def causal(x, d, r):
        return jnp.where(key_i + (d * kc + r) <= qry_i, x, NEG)

    def run_chunks(chunk_fn, carry):
        carry = lax.fori_loop(0, nfull, lambda c, cr: chunk_fn(c, None, cr), carry)
        for d in range(ndiag):
            carry = chunk_fn(nfull + d, d, carry)
        return carry

    def score_chunk(c, d, carry):
        off = pl.multiple_of(c * kc, kc)
        for t in range(kc // sub):
            r0 = off + t * sub
            kis = ki_ref[pl.ds(r0, sub), :]
            sc = None
            for j in range(IDX_HEADS):
                rel = jnp.dot(kis, qit_ref[j], preferred_element_type=f32)
                term = jnp.maximum(rel, 0.0) * wit_ref[j:j + 1, :]
                sc = term if sc is None else sc + term
            if d is not None:
                sc = causal(sc, d, t * sub)
            sc_ref[pl.ds(r0, sub), :] = sc
        return carry

    run_chunks(score_chunk, 0)

    def count(indicator):
        def body(c, acc):
            off = pl.multiple_of(c * kc, kc)
            for t in range(kc // CNT_ROWS):
                r0 = off + t * CNT_ROWS
                acc = acc + indicator(sc_ref[pl.ds(r0, CNT_ROWS), :], r0)
            return acc
        acc = lax.fori_loop(0, nchunks, body, jnp.zeros((CNT_ROWS, qb), i32))
        return jnp.sum(acc, axis=0, keepdims=True)

    def count_ge(thr):
        inside = count(lambda s, r0: jnp.where(s >= thr, 1, 0))
        return inside + jnp.where(thr <= NEG, n_outside, 0)

    def bit_step(step, state):
        prefix, c_prefix = state
        trial = prefix | jnp.left_shift(jnp.int32(1), 31 - step)
        cnt = count_ge(_threshold_bits(trial))
        accept = cnt >= TOPK
        return jnp.where(accept, trial, prefix), jnp.where(accept, cnt, c_prefix)

    prefix, c_ge = lax.fori_loop(
        0, 32, bit_step, (jnp.zeros((1, qb), i32), jnp.full((1, qb), seq, i32)))
    vstar = _threshold_bits(prefix)

    @pl.when(jnp.max(c_ge) > TOPK)
    def _():
        c_gt = (count(lambda s, r0: jnp.where(s > vstar, 1, 0))
                + jnp.where(vstar < NEG, n_outside, 0))
        want = (TOPK - c_gt).astype(f32)
        tt = TIE_TILE
        tri = jnp.where(lax.broadcasted_iota(i32, (tt, tt), 0)
                        >= lax.broadcasted_iota(i32, (tt, tt), 1), 1.0, 0.0).astype(bf16)

        def tie_body(c, seen):
            off = pl.multiple_of(c * kc, kc)
            running = []
            for t in range(kc // tt):
                s = sc_ref[pl.ds(off + t * tt, tt), :]
                tie = jnp.where(s == vstar, 1.0, 0.0).astype(bf16)
                running.append(jnp.dot(tri, tie, preferred_element_type=f32))
            for t in range(kc // tt):
                s = sc_ref[pl.ds(off + t * tt, tt), :]
                rank = seen + running[t]
                sc_ref[pl.ds(off + t * tt, tt), :] = jnp.where(
                    s == vstar, jnp.where(rank <= want, s, BELOW_NEG), s)
                seen = seen + running[t][tt - 1:tt, :]
            return seen

        lax.fori_loop(0, nchunks, tie_body, jnp.zeros((1, qb), f32))

    def fold8(x, op):
        return op(x.reshape(sub // SUBLANES, SUBLANES, qb), axis=0)

    def cap_chunk(c, d, carry):
        off = pl.multiple_of(c * kc, kc)
        for t in range(kc // sub):
            r = t * sub
            cap = jnp.where(sc_ref[pl.ds(off + r, sub), :] >= vstar, jnp.inf, NEG)
            if d is not None:
                cap = causal(cap, d, r)
            sc_ref[pl.ds(off + r, sub), :] = cap
        return carry

    run_chunks(cap_chunk, 0)
    acc_ref[...] = jnp.zeros(acc_ref.shape, f32)

    @pl.when(i == 0)
    def _():
        kmax_ref[...] = jnp.zeros(kmax_ref.shape, f32)

    bounds = []
    for hd in range(HEADS):
        kk = k4_ref[hd, pl.ds(pl.multiple_of(i * qb, qb), qb), :].astype(f32)
        k2 = jnp.max(jnp.sum(kk * kk, axis=1, keepdims=True), axis=0, keepdims=True)
        kmax_ref[hd] = jnp.maximum(kmax_ref[hd], k2)
        qf = qt_ref[hd].astype(f32)
        q2 = jnp.sum(qf * qf, axis=0, keepdims=True)
        bounds.append(jnp.sqrt(q2 * kmax_ref[hd][0:1, 0:1]))
    worst = bounds[0]
    for hd in range(1, HEADS):
        worst = jnp.maximum(worst, bounds[hd])
    fast = jnp.max(worst) <= SHIFT_LIMIT

    row16 = lax.broadcasted_iota(i32, (BF16_ROWS, qb), 0)
    for hd in range(HEADS):
        qa_ref[hd, 0:HEAD_DIM, :] = qt_ref[hd]
        shift = jnp.where(fast, -bounds[hd], 0.0)
        qa_ref[hd, HEAD_DIM:HEAD_DIM + BF16_ROWS, :] = jnp.where(
            row16 == 0, shift, 0.0).astype(bf16)
        qa_ref[hd, HEAD_DIM + BF16_ROWS:LANES, :] = jnp.zeros(
            (LANES - HEAD_DIM - BF16_ROWS, qb), bf16)

    def capped_logits(hd, off, r):
        s = jnp.dot(k4_ref[hd, pl.ds(off + r, sub), :], qa_ref[hd], preferred_element_type=f32)
        return jnp.minimum(s, sc_ref[pl.ds(off + r, sub), :])

    @pl.when(fast)
    def _():
        def chunk(c, d, lparts):
            off = pl.multiple_of(c * kc, kc)
            lparts = list(lparts)
            for hd in range(HEADS):
                for t in range(kc // sub):
                    r = t * sub
                    p = jnp.exp2(capped_logits(hd, off, r))
                    lparts[hd] = lparts[hd] + fold8(p, jnp.sum)
                    p_ref[hd, r:r + sub, :] = p.astype(bf16)
            for hd in range(HEADS):
                acc_ref[hd] += jnp.dot(vt_ref[hd, :, pl.ds(off, kc)], p_ref[hd],
                                       preferred_element_type=f32)
            return tuple(lparts)

        lparts = run_chunks(chunk, tuple(jnp.zeros((SUBLANES, qb), f32) for _ in range(HEADS)))
        for hd in range(HEADS):
            l_ref[hd] = lparts[hd]

    @pl.when(jnp.logical_not(fast))
    def _():
        def attend_chunk(c, d, carry):
            ms, ls = carry
            off = pl.multiple_of(c * kc, kc)
            new_ms, new_ls = list(ms), list(ls)

            def logits_stage(hd):
                mpart = None
                for t in range(kc // sub):
                    r = t * sub
                    s = capped_logits(hd, off, r)
                    s_ref[hd, r:r + sub, :] = s
                    m8 = fold8(s, jnp.max)
                    mpart = m8 if mpart is None else jnp.maximum(mpart, m8)
                new_ms[hd] = jnp.maximum(ms[hd], jnp.max(mpart, axis=0, keepdims=True))

            def value_stage(hd):
                lpart = None
                for t in range(kc // sub):
                    r = t * sub
                    p = jnp.exp2(s_ref[hd, r:r + sub, :] - new_ms[hd])
                    l8 = fold8(p, jnp.sum)
                    lpart = l8 if lpart is None else lpart + l8
                    p_ref[hd, r:r + sub, :] = p.astype(bf16)
                alpha = jnp.exp2(ms[hd] - new_ms[hd])
                new_ls[hd] = alpha * ls[hd] + jnp.sum(lpart, axis=0, keepdims=True)
                pv = jnp.dot(vt_ref[hd, :, pl.ds(off, kc)], p_ref[hd],
                             preferred_element_type=f32)
                acc_ref[hd] = acc_ref[hd] * alpha + pv

            logits_stage(0)
            for hd in range(HEADS):
                if hd + 1 < HEADS:
                    logits_stage(hd + 1)
                value_stage(hd)
            return tuple(new_ms), tuple(new_ls)

        init = (tuple(jnp.full((1, qb), NEG, f32) for _ in range(HEADS)),
                tuple(jnp.zeros((1, qb), f32) for _ in range(HEADS)))
        _, ls = run_chunks(attend_chunk, init)
        row8 = lax.broadcasted_iota(i32, (SUBLANES, qb), 0)
        for hd in range(HEADS):
            l_ref[hd] = jnp.where(row8 == 0, ls[hd], 0.0)

    out_t = jnp.concatenate(
        [acc_ref[hd] / jnp.sum(l_ref[hd], axis=0, keepdims=True) for hd in range(HEADS)],
        axis=0)
    o_ref[...] = out_t.T.astype(o_ref.dtype)


def _dsa_call(qit, wit, qt, ki, k4, vt, bsz, seq):
    n = bsz * seq
    qb = DSA_QB
    nq = seq // qb
    qcol4 = lambda b, i: (0, 0, b * nq + i)
    return pl.pallas_call(
        functools.partial(_dsa_kernel, seq=seq),
        grid=(bsz, nq),
        in_specs=[
            pl.BlockSpec((IDX_HEADS, IDX_DIM, qb), qcol4),
            pl.BlockSpec((SUBLANES, qb), lambda b, i: (0, b * nq + i)),
            pl.BlockSpec((HEADS, HEAD_DIM, qb), qcol4),
            pl.BlockSpec((seq, IDX_DIM), lambda b, i: (b, 0)),
            pl.BlockSpec((HEADS, seq, LANES), lambda b, i: (0, b, 0)),
            pl.BlockSpec((HEADS, HEAD_DIM, seq), lambda b, i: (0, 0, b)),
        ],
        out_specs=pl.BlockSpec((qb, GROUP), lambda b, i: (b * nq + i, 0)),
        out_shape=jax.ShapeDtypeStruct((n, GROUP), jnp.bfloat16),
        scratch_shapes=[
            pltpu.VMEM((seq, qb), jnp.float32),
            pltpu.VMEM((HEADS, HEAD_DIM, qb), jnp.float32),
            pltpu.VMEM((HEADS, DSA_KC, qb), jnp.float32),
            pltpu.VMEM((HEADS, DSA_KC, qb), jnp.bfloat16),
            pltpu.VMEM((HEADS, LANES, qb), jnp.bfloat16),
            pltpu.VMEM((HEADS, SUBLANES, qb), jnp.float32),
            pltpu.VMEM((HEADS, SUBLANES, LANES), jnp.float32),
        ],
        compiler_params=pltpu.CompilerParams(
            dimension_semantics=("arbitrary", "arbitrary"),
            vmem_limit_bytes=VMEM_LIMIT),
        name="dsa",
    )(qit, wit, qt, ki, k4, vt)


def _out_ffn_kernel(x_ref, yab_ref, yc_ref, yd_ref, wout_ref, gffn_ref, wg_ref, wu_ref,
                    wd_ref, gfin_ref, o_ref, acc_sc, hf_sc, *, final):
    f32, bf16 = jnp.float32, jnp.bfloat16
    j = pl.program_id(1)

    @pl.when(j == 0)
    def _():
        y = jnp.concatenate([yab_ref[...], yc_ref[...], yd_ref[...]], axis=1)
        x1 = x_ref[...] + jnp.dot(y, wout_ref[...], preferred_element_type=f32)
        acc_sc[...] = x1
        hf_sc[...] = _rmsnorm(x1, gffn_ref[...]).astype(bf16)

    hf = hf_sc[...]
    gate = jnp.dot(hf, wg_ref[...], preferred_element_type=f32)
    up = jnp.dot(hf, wu_ref[...], preferred_element_type=f32)
    act = (gate * jax.nn.sigmoid(gate) * up).astype(bf16)
    acc_sc[...] += jnp.dot(act, wd_ref[...], preferred_element_type=f32)

    @pl.when(j == pl.num_programs(1) - 1)
    def _():
        out = acc_sc[...]
        if final:
            out = _rmsnorm(out, gfin_ref[...])
        o_ref[...] = out


def _out_ffn_call(x2d, yab, yc, yd, wout, gffn, wg, wu, wd, gfin, final):
    n = x2d.shape[0]
    tm, th = FFN_TM, FFN_TH
    row = lambda i, j: (i, 0)
    const = lambda i, j: (0, 0)
    return pl.pallas_call(
        functools.partial(_out_ffn_kernel, final=final),
        grid=(n // tm, FFN_HIDDEN // th),
        in_specs=[
            pl.BlockSpec((tm, D_MODEL), row),
            pl.BlockSpec((tm, 2 * GROUP), row),
            pl.BlockSpec((tm, GROUP), row),
            pl.BlockSpec((tm, GROUP), row),
            pl.BlockSpec((D_MODEL, D_MODEL), const),
            pl.BlockSpec((1, D_MODEL), const),
            pl.BlockSpec((D_MODEL, th), lambda i, j: (0, j)),
            pl.BlockSpec((D_MODEL, th), lambda i, j: (0, j)),
            pl.BlockSpec((th, D_MODEL), lambda i, j: (j, 0)),
            pl.BlockSpec((1, D_MODEL), const),
        ],
        out_specs=pl.BlockSpec((tm, D_MODEL), row),
        out_shape=jax.ShapeDtypeStruct((n, D_MODEL), jnp.float32),
        scratch_shapes=[pltpu.VMEM((tm, D_MODEL), jnp.float32),
                        pltpu.VMEM((tm, D_MODEL), jnp.bfloat16)],
        compiler_params=pltpu.CompilerParams(
            dimension_semantics=("arbitrary", "arbitrary"),
            vmem_limit_bytes=VMEM_LIMIT),
        name="out_ffn",
    )(x2d, yab, yc, yd, wout, gffn, wg, wu, wd, gfin)


def _pack_w_in(w):
    g = GROUP
    cuts = [0, 3 * g, 5 * g, 6 * g, 7 * g, 8 * g, 9 * g, 9 * g + IDX_DIM,
            9 * g + IDX_DIM + IDX_HEADS, 11 * g + IDX_DIM + IDX_HEADS]
    a, b, q, k, v, qi, ki, wi, d = [w[:, cuts[t]:cuts[t + 1]] for t in range(9)]
    z = lambda cols: jnp.zeros((w.shape[0], cols), w.dtype)
    rows = jnp.concatenate([a, b, k, ki, z(LANES - IDX_DIM), d], axis=1)
    trans = jnp.concatenate([q, v, qi, wi, z(WI_ROWS - IDX_HEADS)], axis=1).T
    return rows.astype(jnp.bfloat16), trans.astype(jnp.bfloat16)


def _rope_tables(seq):
    inv_freq = ROPE_THETA ** (-jnp.arange(0, HEAD_DIM, 2, dtype=jnp.float32) / HEAD_DIM)
    ang = jnp.arange(seq, dtype=jnp.float32)[:, None] * inv_freq[None, :]
    c, s = jnp.cos(ang), jnp.sin(ang)
    reps = LANES // HEAD_DIM
    cos = jnp.tile(jnp.concatenate([c, c], axis=1), (1, reps))
    sin = jnp.tile(jnp.concatenate([-s, s], axis=1), (1, reps))
    return cos, sin, c.T, s.T


def kernel(x, g_mix, w_in, w_conv_a, gmlp_ln_g, gmlp_ln_b, w_s, b_s, w_conf, b_conf,
           conf_ln_g, conf_ln_b, w_out, g_ffn, w_gate, w_up, w_down, g_final):
    bsz, seq, d = x.shape
    depth = w_in.shape[0]
    bf16 = jnp.bfloat16
    tables = _rope_tables(seq)
    xs = x
    for l in range(depth):
        w_rows, w_t = _pack_w_in(w_in[l])
        yab, yd, k4, ki, qt, vt, qit, wit = _mix_call(
            xs.reshape(bsz, seq, d), g_mix[l][None, :], w_rows, w_t, tables,
            w_conv_a[l], gmlp_ln_g[l][None, :], gmlp_ln_b[l][None, :], w_s[l],
            jnp.repeat(b_s[l].T, HEAD_DIM, axis=1), w_conf[l], b_conf[l][None, :],
            conf_ln_g[l][None, :], conf_ln_b[l][None, :])
        yc = _dsa_call(qit, wit, qt, ki, k4, vt, bsz, seq)
        xs = _out_ffn_call(
            xs.reshape(bsz * seq, d), yab, yc, yd, w_out[l].astype(bf16), g_ffn[l][None, :],
            w_gate[l].astype(bf16), w_up[l].astype(bf16), w_down[l].astype(bf16),
            g_final[None, :], final=(l == depth - 1))
    return xs.reshape(bsz, seq, d)
```

```python
import functools
import math

import jax
import jax.numpy as jnp
import numpy as np
from jax import lax
from jax.experimental import pallas as pl
from jax.experimental.pallas import tpu as pltpu

D_MODEL = 1024
GROUP = 256
HEADS = 4
HEAD_DIM = 64
CHUNK = 128
IDX_HEADS = 4
IDX_DIM = 64
TOPK = 256
SHORT_W = 3
CONF_W = 31
FFN_HIDDEN = 2816
ROPE_THETA = 10000.0
NORM_EPS = 1e-6
LN_EPS = 1e-5
NEG = -1e30
NEG_BF16 = float(np.asarray(NEG, dtype=jnp.bfloat16).astype(np.float32))
BELOW_NEG = -3.0e38

LANES = 128
SUBLANES = 8
A_HALO = 8
D_HALO = 32

COL_A = 0
COL_B = COL_A + 3 * GROUP
COL_K = COL_B + 2 * GROUP
COL_KI = COL_K + GROUP
COL_D = COL_KI + LANES
ROW_COLS = COL_D + 2 * GROUP
TROW_Q = 0
TROW_V = TROW_Q + GROUP
TROW_QI = TROW_V + GROUP
TROW_WI = TROW_QI + IDX_HEADS * IDX_DIM
WI_ROWS = 16
TROWS = TROW_WI + WI_ROWS

MIX_TM = 512
DSA_QB = 512
DSA_KC = 512
DSA_SUB = 64
CNT_ROWS = 16
CNTB_ROWS = 32
TIE_TILE = 128
BF16_ROWS = 16
SHIFT_LIMIT = 45.0
FFN_TM = 512
FFN_TH = 1408
VMEM_LIMIT = 56 * 1024 * 1024

_NT = (((1,), (1,)), ((), ()))


def _rmsnorm(xf, g):
    ms = jnp.mean(xf * xf, axis=-1, keepdims=True)
    return xf * lax.rsqrt(ms + NORM_EPS) * g


def _layernorm(xf, g, b):
    mu = jnp.mean(xf, axis=-1, keepdims=True)
    xc = xf - mu
    var = jnp.mean(xc * xc, axis=-1, keepdims=True)
    return xc * lax.rsqrt(var + LN_EPS) * g + b


def _rope128(x, cos, sin_signed):
    half = HEAD_DIM // 2
    lane = lax.broadcasted_iota(jnp.int32, x.shape, 1) % HEAD_DIM
    up = pltpu.roll(x, LANES - half, axis=1)
    dn = pltpu.roll(x, half, axis=1)
    return x * cos + jnp.where(lane < half, up, dn) * sin_signed


def _rope_t(x, cos_t, sin_t):
    half = HEAD_DIM // 2
    x1, x2 = x[0:half, :], x[half:HEAD_DIM, :]
    return x1 * cos_t - x2 * sin_t, x2 * cos_t + x1 * sin_t


def _mix_kernel(x_ref, g_ref, w_ref, wt_ref, cos_ref, sin_ref, cost_ref, sint_ref,
                wca_ref, lngb_ref, lnbb_ref, ws_ref, bsb_ref, wcf_ref, bcf_ref,
                lngd_ref, lnbd_ref,
                yab_ref, yd_ref, k4_ref, ki_ref, qt_ref, vt_ref, qit_ref, wit_ref,
                abuf, dbuf, sbuf):
    tm = x_ref.shape[0]
    f32, bf16 = jnp.float32, jnp.bfloat16
    half = HEAD_DIM // 2

    @pl.when(pl.program_id(1) == 0)
    def _():
        abuf[0:A_HALO, :] = jnp.zeros((A_HALO, GROUP), f32)
        dbuf[0:D_HALO, :] = jnp.zeros((D_HALO, GROUP), f32)

    h = _rmsnorm(x_ref[...], g_ref[...]).astype(bf16)

    def proj(lo, hi):
        return jnp.dot(h, w_ref[:, lo:hi], preferred_element_type=f32)

    def proj_t(lo, hi):
        return lax.dot_general(wt_ref[lo:hi, :], h, _NT, preferred_element_type=f32)

    za = proj(COL_A, COL_B)
    abuf[A_HALO:A_HALO + tm, :] = za[:, GROUP:2 * GROUP] * za[:, 2 * GROUP:3 * GROUP]
    conv = abuf[pl.ds(A_HALO - 2, tm), :] * wca_ref[0:1, :]
    conv = conv + abuf[pl.ds(A_HALO - 1, tm), :] * wca_ref[1:2, :]
    conv = conv + abuf[pl.ds(A_HALO, tm), :] * wca_ref[2:3, :]
    yab_ref[:, 0:GROUP] = (za[:, 0:GROUP] * conv).astype(bf16)
    abuf[0:A_HALO, :] = abuf[tm:tm + A_HALO, :]

    zb = proj(COL_B, COL_K)
    vn = _layernorm(zb[:, GROUP:2 * GROUP], lngb_ref[...], lnbb_ref[...]).astype(bf16)
    r_i = lax.broadcasted_iota(jnp.int32, (CHUNK, CHUNK), 0)
    c_i = lax.broadcasted_iota(jnp.int32, (CHUNK, CHUNK), 1)
    wsm = [jnp.where(r_i >= c_i, ws_ref[hd], 0.0).astype(bf16) for hd in range(HEADS)]
    lane_head = lax.broadcasted_iota(jnp.int32, (CHUNK, GROUP), 1) // HEAD_DIM
    for c in range(tm // CHUNK):
        rows = slice(c * CHUNK, (c + 1) * CHUNK)
        vc = vn[rows, :]
        mixed = bsb_ref[...]
        for hd in range(HEADS):
            full = jnp.dot(wsm[hd], vc, preferred_element_type=f32)
            mixed = mixed + jnp.where(lane_head == hd, full, 0.0)
        yab_ref[rows, GROUP:2 * GROUP] = (zb[rows, 0:GROUP] * mixed).astype(bf16)

    cos = cos_ref[...]
    sin = sin_ref[...]
    zk = proj(COL_K, COL_D)
    lane = lax.broadcasted_iota(jnp.int32, (tm, LANES), 1)
    bias_col = jnp.where(lane == HEAD_DIM, 1.0, 0.0)
    for pair in range(2):
        kp = _rope128(zk[:, pair * LANES:(pair + 1) * LANES], cos, sin)
        for sub in range(2):
            kh = kp if sub == 0 else pltpu.roll(kp, HEAD_DIM, axis=1)
            k4_ref[2 * pair + sub] = jnp.where(lane < HEAD_DIM, kh, bias_col).astype(bf16)
    kip = _rope128(zk[:, GROUP:GROUP + LANES], cos, sin)
    ki_ref[...] = kip[:, 0:IDX_DIM].astype(bf16)

    cos_t = cost_ref[...]
    sin_t = sint_ref[...]
    q_scale = HEAD_DIM ** -0.5 * math.log2(math.e)
    zq = proj_t(TROW_Q, TROW_V)
    zqi = proj_t(TROW_QI, TROW_WI)
    for hd in range(HEADS):
        rows = slice(hd * HEAD_DIM, (hd + 1) * HEAD_DIM)
        o1, o2 = _rope_t(zq[rows, :], cos_t, sin_t)
        qt_ref[hd, 0:half, :] = (o1 * q_scale).astype(bf16)
        qt_ref[hd, half:HEAD_DIM, :] = (o2 * q_scale).astype(bf16)
        o1, o2 = _rope_t(zqi[rows, :], cos_t, sin_t)
        qit_ref[hd, 0:half, :] = o1.astype(bf16)
        qit_ref[hd, half:HEAD_DIM, :] = o2.astype(bf16)
    zv = proj_t(TROW_V, TROW_QI)
    for hd in range(HEADS):
        vt_ref[hd] = zv[hd * HEAD_DIM:(hd + 1) * HEAD_DIM, :].astype(bf16)
    zw = proj_t(TROW_WI, TROWS)
    wit_ref[...] = zw[0:SUBLANES, :] * (IDX_HEADS ** -0.5 * IDX_DIM ** -0.5)

    zd = proj(COL_D, ROW_COLS)
    dbuf[D_HALO:D_HALO + tm, :] = zd[:, 0:GROUP] * jax.nn.sigmoid(zd[:, GROUP:2 * GROUP])
    acc = jnp.broadcast_to(bcf_ref[...], (tm, GROUP))
    first = D_HALO - (CONF_W - 1)
    for r in range(SUBLANES):
        taps = [t for t in range(CONF_W) if (first + t) % SUBLANES == r]
        span = (first + taps[-1]) - (first + taps[0]) + tm
        sbuf[0:span, :] = dbuf[pl.ds(first + taps[0], span), :]
        for t in taps:
            lo = t - taps[0]
            acc = acc + sbuf[lo:lo + tm, :] * wcf_ref[t:t + 1, :]
    y = _layernorm(acc, lngd_ref[...], lnbd_ref[...])
    yd_ref[...] = (y * jax.nn.sigmoid(y)).astype(bf16)
    dbuf[0:D_HALO, :] = dbuf[tm:tm + D_HALO, :]


def _mix_call(x, g, w_rows, w_t, tables, wca, lngb, lnbb, ws, bsb, wcf, bcf, lngd, lnbd):
    bsz, seq, _ = x.shape
    n = bsz * seq
    tm = MIX_TM
    nt = seq // tm
    bf16 = jnp.bfloat16
    half = HEAD_DIM // 2
    cos, sin, cos_t, sin_t = tables
    row = lambda b, i: (b * nt + i, 0)
    row4 = lambda b, i: (0, b * nt + i, 0)
    col = lambda b, i: (0, b * nt + i)
    col4 = lambda b, i: (0, 0, b * nt + i)
    const2 = lambda b, i: (0, 0)
    const3 = lambda b, i: (0, 0, 0)
    in_specs = [
        pl.BlockSpec((None, tm, D_MODEL), lambda b, i: (b, i, 0)),
        pl.BlockSpec((1, D_MODEL), const2),
        pl.BlockSpec((D_MODEL, ROW_COLS), const2),
        pl.BlockSpec((TROWS, D_MODEL), const2),
        pl.BlockSpec((tm, LANES), lambda b, i: (i, 0)),
        pl.BlockSpec((tm, LANES), lambda b, i: (i, 0)),
        pl.BlockSpec((half, tm), lambda b, i: (0, i)),
        pl.BlockSpec((half, tm), lambda b, i: (0, i)),
        pl.BlockSpec((SHORT_W, GROUP), const2),
        pl.BlockSpec((1, GROUP), const2),
        pl.BlockSpec((1, GROUP), const2),
        pl.BlockSpec((HEADS, CHUNK, CHUNK), const3),
        pl.BlockSpec((CHUNK, GROUP), const2),
        pl.BlockSpec((CONF_W, GROUP), const2),
        pl.BlockSpec((1, GROUP), const2),
        pl.BlockSpec((1, GROUP), const2),
        pl.BlockSpec((1, GROUP), const2),
    ]
    out_shape = [
        jax.ShapeDtypeStruct((n, 2 * GROUP), bf16),
        jax.ShapeDtypeStruct((n, GROUP), bf16),
        jax.ShapeDtypeStruct((HEADS, n, LANES), bf16),
        jax.ShapeDtypeStruct((n, IDX_DIM), bf16),
        jax.ShapeDtypeStruct((HEADS, HEAD_DIM, n), bf16),
        jax.ShapeDtypeStruct((HEADS, HEAD_DIM, n), bf16),
        jax.ShapeDtypeStruct((IDX_HEADS, IDX_DIM, n), bf16),
        jax.ShapeDtypeStruct((SUBLANES, n), jnp.float32),
    ]
    out_specs = [
        pl.BlockSpec((tm, 2 * GROUP), row),
        pl.BlockSpec((tm, GROUP), row),
        pl.BlockSpec((HEADS, tm, LANES), row4),
        pl.BlockSpec((tm, IDX_DIM), row),
        pl.BlockSpec((HEADS, HEAD_DIM, tm), col4),
        pl.BlockSpec((HEADS, HEAD_DIM, tm), col4),
        pl.BlockSpec((IDX_HEADS, IDX_DIM, tm), col4),
        pl.BlockSpec((SUBLANES, tm), col),
    ]
    return pl.pallas_call(
        _mix_kernel,
        grid=(bsz, nt),
        in_specs=in_specs,
        out_specs=out_specs,
        out_shape=out_shape,
        scratch_shapes=[pltpu.VMEM((tm + A_HALO, GROUP), jnp.float32),
                        pltpu.VMEM((tm + D_HALO, GROUP), jnp.float32),
                        pltpu.VMEM((tm + D_HALO, GROUP), jnp.float32)],
        compiler_params=pltpu.CompilerParams(
            dimension_semantics=("arbitrary", "arbitrary"),
            vmem_limit_bytes=VMEM_LIMIT),
        name="mix",
    )(x, g, w_rows, w_t, cos, sin, cos_t, sin_t, wca, lngb, lnbb, ws, bsb, wcf, bcf,
      lngd, lnbd)


def _threshold_bits(trial):
    bits = jnp.where(trial < 0, trial & jnp.int32(0x7FFFFFFF), ~trial)
    return pltpu.bitcast(bits, jnp.float32)


def _dsa_kernel(qit_ref, wit_ref, qt_ref, ki_ref, k4_ref, vt_ref, o_ref,
                sc_ref, scb_ref, acc_ref, s_ref, p_ref, qa_ref, l_ref, kmax_ref, *, seq):
    f32, bf16, i32 = jnp.float32, jnp.bfloat16, jnp.int32
    qb = wit_ref.shape[1]
    kc, sub = DSA_KC, DSA_SUB
    ndiag = qb // kc
    i = pl.program_id(1)
    nfull = i * ndiag
    nchunks = nfull + ndiag
    n_outside = seq - nchunks * kc
    key_i = lax.broadcasted_iota(i32, (sub, qb), 0)
    qry_i = lax.broadcasted_iota(i32, (sub, qb), 1)

    def causal(x, d, r):
        return jnp.where(key_i + (d * kc + r) <= qry_i, x, NEG)

    def run_chunks(chunk_fn, carry):
        carry = lax.fori_loop(0, nfull, lambda c, cr: chunk_fn(c, None, cr), carry)
        for d in range(ndiag):
            carry = chunk_fn(nfull + d, d, carry)
        return carry

    def score_chunk(c, d, carry):
        off = pl.multiple_of(c * kc, kc)
        for t in range(kc // sub):
            r0 = off + t * sub
            kis = ki_ref[pl.ds(r0, sub), :]
            sc = None
            for j in range(IDX_HEADS):
                rel = jnp.dot(kis, qit_ref[j], preferred_element_type=f32)
                term = jnp.maximum(rel, 0.0) * wit_ref[j:j + 1, :]
                sc = term if sc is None else sc + term
            if d is not None:
                sc = causal(sc, d, t * sub)
            sc_ref[pl.ds(r0, sub), :] = sc
            scb_ref[pl.ds(r0, sub), :] = sc.astype(bf16)
        return carry

    run_chunks(score_chunk, 0)

    def count(indicator):
        def body(c, acc):
            off = pl.multiple_of(c * kc, kc)
            for t in range(kc // CNT_ROWS):
                r0 = off + t * CNT_ROWS
                acc = acc + indicator(sc_ref[pl.ds(r0, CNT_ROWS), :], r0)
            return acc
        acc = lax.fori_loop(0, nchunks, body, jnp.zeros((CNT_ROWS, qb), i32))
        return jnp.sum(acc, axis=0, keepdims=True)

    def count_ge(thr):
        inside = count(lambda s, r0: jnp.where(s >= thr, 1, 0))
        return inside + jnp.where(thr <= NEG, n_outside, 0)

    neg_b = NEG_BF16
    one_b, zero_b = jnp.ones((), bf16), jnp.zeros((), bf16)

    def count_ge_b(thr):
        thr_b = thr.astype(bf16)

        def body(c, acc):
            off = pl.multiple_of(c * kc, kc)
            for t in range(kc // CNTB_ROWS):
                x = scb_ref[pl.ds(off + t * CNTB_ROWS, CNTB_ROWS), :]
                acc = acc + jnp.where(x >= thr_b, one_b, zero_b)
            return acc

        acc = lax.fori_loop(0, nchunks, body, jnp.zeros((CNTB_ROWS, qb), bf16))
        inside = jnp.sum(acc.astype(f32), axis=0, keepdims=True).astype(i32)
        return inside + jnp.where(thr_b.astype(f32) <= neg_b, n_outside, 0)

    def coarse_step(step, prefix):
        trial = prefix | jnp.left_shift(jnp.int32(1), 31 - step)
        cnt = count_ge_b(_threshold_bits(trial))
        return jnp.where(cnt >= TOPK, trial, prefix)

    coarse = lax.fori_loop(0, 16, coarse_step, jnp.zeros((1, qb), i32))

    start = coarse - jnp.int32(1 << 16)
    c_start = count_ge(_threshold_bits(start))

    def fine_step(step, state):
        prefix, c_prefix = state
        trial = prefix + jnp.left_shift(jnp.int32(1), 16 - step)
        cnt = count_ge(_threshold_bits(trial))
        accept = cnt >= TOPK
        return jnp.where(accept, trial, prefix), jnp.where(accept, cnt, c_prefix)

    prefix, c_ge = lax.fori_loop(0, 17, fine_step, (start, c_start))
    vstar = _threshold_bits(prefix)

    @pl.when(jnp.max(c_ge) > TOPK)
    def _():
        c_gt = (count(lambda s, r0: jnp.where(s > vstar, 1, 0))
                + jnp.where(vstar < NEG, n_outside, 0))
        want = (TOPK - c_gt).astype(f32)
        tt = TIE_TILE
        tri = jnp.where(lax.broadcasted_iota(i32, (tt, tt), 0)
                        >= lax.broadcasted_iota(i32, (tt, tt), 1), 1.0, 0.0).astype(bf16)

        def tie_body(c, seen):
            off = pl.multiple_of(c * kc, kc)
            running = []
            for t in range(kc // tt):
                s = sc_ref[pl.ds(off + t * tt, tt), :]
                tie = jnp.where(s == vstar, 1.0, 0.0).astype(bf16)
                running.append(jnp.dot(tri, tie, preferred_element_type=f32))
            for t in range(kc // tt):
                s = sc_ref[pl.ds(off + t * tt, tt), :]
                rank = seen + running[t]
                sc_ref[pl.ds(off + t * tt, tt), :] = jnp.where(
                    s == vstar, jnp.where(rank <= want, s, BELOW_NEG), s)
                seen = seen + running[t][tt - 1:tt, :]
            return seen

        lax.fori_loop(0, nchunks, tie_body, jnp.zeros((1, qb), f32))

    def fold8(x, op):
        return op(x.reshape(sub // SUBLANES, SUBLANES, qb), axis=0)

    def cap_chunk(c, d, carry):
        off = pl.multiple_of(c * kc, kc)
        for t in range(kc // sub):
            r = t * sub
            cap = jnp.where(sc_ref[pl.ds(off + r, sub), :] >= vstar, jnp.inf, NEG)
            if d is not None:
                cap = causal(cap, d, r)
            sc_ref[pl.ds(off + r, sub), :] = cap
        return carry

    run_chunks(cap_chunk, 0)
    acc_ref[...] = jnp.zeros(acc_ref.shape, f32)

    @pl.when(i == 0)
    def _():
        kmax_ref[...] = jnp.zeros(kmax_ref.shape, f32)

    bounds = []
    for hd in range(HEADS):
        kk = k4_ref[hd, pl.ds(pl.multiple_of(i * qb, qb), qb), :].astype(f32)
        k2 = jnp.max(jnp.sum(kk * kk, axis=1, keepdims=True), axis=0, keepdims=True)
        kmax_ref[hd] = jnp.maximum(kmax_ref[hd], k2)
        qf = qt_ref[hd].astype(f32)
        q2 = jnp.sum(qf * qf, axis=0, keepdims=True)
        bounds.append(jnp.sqrt(q2 * kmax_ref[hd][0:1, 0:1]))
    worst = bounds[0]
    for hd in range(1, HEADS):
        worst = jnp.maximum(worst, bounds[hd])
    fast = jnp.max(worst) <= SHIFT_LIMIT

    row16 = lax.broadcasted_iota(i32, (BF16_ROWS, qb), 0)
    for hd in range(HEADS):
        qa_ref[hd, 0:HEAD_DIM, :] = qt_ref[hd]
        shift = jnp.where(fast, -bounds[hd], 0.0)
        qa_ref[hd, HEAD_DIM:HEAD_DIM + BF16_ROWS, :] = jnp.where(
            row16 == 0, shift, 0.0).astype(bf16)
        qa_ref[hd, HEAD_DIM + BF16_ROWS:LANES, :] = jnp.zeros(
            (LANES - HEAD_DIM - BF16_ROWS, qb), bf16)

    def capped_logits(hd, off, r):
        s = jnp.dot(k4_ref[hd, pl.ds(off + r, sub), :], qa_ref[hd], preferred_element_type=f32)
        return jnp.minimum(s, sc_ref[pl.ds(off + r, sub), :])

    @pl.when(fast)
    def _():
        def chunk(c, d, lparts):
            off = pl.multiple_of(c * kc, kc)
            lparts = list(lparts)
            for hd in range(HEADS):
                for t in range(kc // sub):
                    r = t * sub
                    p = jnp.exp2(capped_logits(hd, off, r))
                    lparts[hd] = lparts[hd] + fold8(p, jnp.sum)
                    p_ref[hd, r:r + sub, :] = p.astype(bf16)
            for hd in range(HEADS):
                acc_ref[hd] += jnp.dot(vt_ref[hd, :, pl.ds(off, kc)], p_ref[hd],
                                       preferred_element_type=f32)
            return tuple(lparts)

        lparts = run_chunks(chunk, tuple(jnp.zeros((SUBLANES, qb), f32) for _ in range(HEADS)))
        for hd in range(HEADS):
            l_ref[hd] = lparts[hd]

    @pl.when(jnp.logical_not(fast))
    def _():
        def attend_chunk(c, d, carry):
            ms, ls = carry
            off = pl.multiple_of(c * kc, kc)
            new_ms, new_ls = list(ms), list(ls)

            def logits_stage(hd):
                mpart = None
                for t in range(kc // sub):
                    r = t * sub
                    s = capped_logits(hd, off, r)
                    s_ref[hd, r:r + sub, :] = s
                    m8 = fold8(s, jnp.max)
                    mpart = m8 if mpart is None else jnp.maximum(mpart, m8)
                new_ms[hd] = jnp.maximum(ms[hd], jnp.max(mpart, axis=0, keepdims=True))

            def value_stage(hd):
                lpart = None
                for t in range(kc // sub):
                    r = t * sub
                    p = jnp.exp2(s_ref[hd, r:r + sub, :] - new_ms[hd])
                    l8 = fold8(p, jnp.sum)
                    lpart = l8 if lpart is None else lpart + l8
                    p_ref[hd, r:r + sub, :] = p.astype(bf16)
                alpha = jnp.exp2(ms[hd] - new_ms[hd])
                new_ls[hd] = alpha * ls[hd] + jnp.sum(lpart, axis=0, keepdims=True)
                pv = jnp.dot(vt_ref[hd, :, pl.ds(off, kc)], p_ref[hd],
                             preferred_element_type=f32)
                acc_ref[hd] = acc_ref[hd] * alpha + pv

            logits_stage(0)
            for hd in range(HEADS):
                if hd + 1 < HEADS:
                    logits_stage(hd + 1)
                value_stage(hd)
            return tuple(new_ms), tuple(new_ls)

        init = (tuple(jnp.full((1, qb), NEG, f32) for _ in range(HEADS)),
                tuple(jnp.zeros((1, qb), f32) for _ in range(HEADS)))
        _, ls = run_chunks(attend_chunk, init)
        row8 = lax.broadcasted_iota(i32, (SUBLANES, qb), 0)
        for hd in range(HEADS):
            l_ref[hd] = jnp.where(row8 == 0, ls[hd], 0.0)

    out_t = jnp.concatenate(
        [acc_ref[hd] / jnp.sum(l_ref[hd], axis=0, keepdims=True) for hd in range(HEADS)],
        axis=0)
    o_ref[...] = out_t.T.astype(o_ref.dtype)


def _dsa_call(qit, wit, qt, ki, k4, vt, bsz, seq):
    n = bsz * seq
    qb = DSA_QB
    nq = seq // qb
    qcol4 = lambda b, i: (0, 0, b * nq + i)
    return pl.pallas_call(
        functools.partial(_dsa_kernel, seq=seq),
        grid=(bsz, nq),
        in_specs=[
            pl.BlockSpec((IDX_HEADS, IDX_DIM, qb), qcol4),
            pl.BlockSpec((SUBLANES, qb), lambda b, i: (0, b * nq + i)),
            pl.BlockSpec((HEADS, HEAD_DIM, qb), qcol4),
            pl.BlockSpec((seq, IDX_DIM), lambda b, i: (b, 0)),
            pl.BlockSpec((HEADS, seq, LANES), lambda b, i: (0, b, 0)),
            pl.BlockSpec((HEADS, HEAD_DIM, seq), lambda b, i: (0, 0, b)),
        ],
        out_specs=pl.BlockSpec((qb, GROUP), lambda b, i: (b * nq + i, 0)),
        out_shape=jax.ShapeDtypeStruct((n, GROUP), jnp.bfloat16),
        scratch_shapes=[
            pltpu.VMEM((seq, qb), jnp.float32),
            pltpu.VMEM((seq, qb), jnp.bfloat16),
            pltpu.VMEM((HEADS, HEAD_DIM, qb), jnp.float32),
            pltpu.VMEM((HEADS, DSA_KC, qb), jnp.float32),
            pltpu.VMEM((HEADS, DSA_KC, qb), jnp.bfloat16),
            pltpu.VMEM((HEADS, LANES, qb), jnp.bfloat16),
            pltpu.VMEM((HEADS, SUBLANES, qb), jnp.float32),
            pltpu.VMEM((HEADS, SUBLANES, LANES), jnp.float32),
        ],
        compiler_params=pltpu.CompilerParams(
            dimension_semantics=("arbitrary", "arbitrary"),
            vmem_limit_bytes=VMEM_LIMIT),
        name="dsa",
    )(qit, wit, qt, ki, k4, vt)


def _out_ffn_kernel(x_ref, yab_ref, yc_ref, yd_ref, wout_ref, gffn_ref, wg_ref, wu_ref,
                    wd_ref, gfin_ref, o_ref, acc_sc, hf_sc, *, final):
    f32, bf16 = jnp.float32, jnp.bfloat16
    j = pl.program_id(1)

    @pl.when(j == 0)
    def _():
        y = jnp.concatenate([yab_ref[...], yc_ref[...], yd_ref[...]], axis=1)
        x1 = x_ref[...] + jnp.dot(y, wout_ref[...], preferred_element_type=f32)
        acc_sc[...] = x1
        hf_sc[...] = _rmsnorm(x1, gffn_ref[...]).astype(bf16)

    hf = hf_sc[...]
    gate = jnp.dot(hf, wg_ref[...], preferred_element_type=f32)
    up = jnp.dot(hf, wu_ref[...], preferred_element_type=f32)
    act = (gate * jax.nn.sigmoid(gate) * up).astype(bf16)
    acc_sc[...] += jnp.dot(act, wd_ref[...], preferred_element_type=f32)

    @pl.when(j == pl.num_programs(1) - 1)
    def _():
        out = acc_sc[...]
        if final:
            out = _rmsnorm(out, gfin_ref[...])
        o_ref[...] = out


def _out_ffn_call(x2d, yab, yc, yd, wout, gffn, wg, wu, wd, gfin, final):
    n = x2d.shape[0]
    tm, th = FFN_TM, FFN_TH
    row = lambda i, j: (i, 0)
    const = lambda i, j: (0, 0)
    return pl.pallas_call(
        functools.partial(_out_ffn_kernel, final=final),
        grid=(n // tm, FFN_HIDDEN // th),
        in_specs=[
            pl.BlockSpec((tm, D_MODEL), row),
            pl.BlockSpec((tm, 2 * GROUP), row),
            pl.BlockSpec((tm, GROUP), row),
            pl.BlockSpec((tm, GROUP), row),
            pl.BlockSpec((D_MODEL, D_MODEL), const),
            pl.BlockSpec((1, D_MODEL), const),
            pl.BlockSpec((D_MODEL, th), lambda i, j: (0, j)),
            pl.BlockSpec((D_MODEL, th), lambda i, j: (0, j)),
            pl.BlockSpec((th, D_MODEL), lambda i, j: (j, 0)),
            pl.BlockSpec((1, D_MODEL), const),
        ],
        out_specs=pl.BlockSpec((tm, D_MODEL), row),
        out_shape=jax.ShapeDtypeStruct((n, D_MODEL), jnp.float32),
        scratch_shapes=[pltpu.VMEM((tm, D_MODEL), jnp.float32),
                        pltpu.VMEM((tm, D_MODEL), jnp.bfloat16)],
        compiler_params=pltpu.CompilerParams(
            dimension_semantics=("arbitrary", "arbitrary"),
            vmem_limit_bytes=VMEM_LIMIT),
        name="out_ffn",
    )(x2d, yab, yc, yd, wout, gffn, wg, wu, wd, gfin)


def _pack_w_in(w):
    g = GROUP
    cuts = [0, 3 * g, 5 * g, 6 * g, 7 * g, 8 * g, 9 * g, 9 * g + IDX_DIM,
            9 * g + IDX_DIM + IDX_HEADS, 11 * g + IDX_DIM + IDX_HEADS]
    a, b, q, k, v, qi, ki, wi, d = [w[:, cuts[t]:cuts[t + 1]] for t in range(9)]
    z = lambda cols: jnp.zeros((w.shape[0], cols), w.dtype)
    rows = jnp.concatenate([a, b, k, ki, z(LANES - IDX_DIM), d], axis=1)
    trans = jnp.concatenate([q, v, qi, wi, z(WI_ROWS - IDX_HEADS)], axis=1).T
    return rows.astype(jnp.bfloat16), trans.astype(jnp.bfloat16)


def _rope_tables(seq):
    inv_freq = ROPE_THETA ** (-jnp.arange(0, HEAD_DIM, 2, dtype=jnp.float32) / HEAD_DIM)
    ang = jnp.arange(seq, dtype=jnp.float32)[:, None] * inv_freq[None, :]
    c, s = jnp.cos(ang), jnp.sin(ang)
    reps = LANES // HEAD_DIM
    cos = jnp.tile(jnp.concatenate([c, c], axis=1), (1, reps))
    sin = jnp.tile(jnp.concatenate([-s, s], axis=1), (1, reps))
    return cos, sin, c.T, s.T


def kernel(x, g_mix, w_in, w_conv_a, gmlp_ln_g, gmlp_ln_b, w_s, b_s, w_conf, b_conf,
           conf_ln_g, conf_ln_b, w_out, g_ffn, w_gate, w_up, w_down, g_final):
    bsz, seq, d = x.shape
    depth = w_in.shape[0]
    bf16 = jnp.bfloat16
    tables = _rope_tables(seq)
    xs = x
    for l in range(depth):
        w_rows, w_t = _pack_w_in(w_in[l])
        yab, yd, k4, ki, qt, vt, qit, wit = _mix_call(
            xs.reshape(bsz, seq, d), g_mix[l][None, :], w_rows, w_t, tables,
            w_conv_a[l], gmlp_ln_g[l][None, :], gmlp_ln_b[l][None, :], w_s[l],
            jnp.repeat(b_s[l].T, HEAD_DIM, axis=1), w_conf[l], b_conf[l][None, :],
            conf_ln_g[l][None, :], conf_ln_b[l][None, :])
        yc = _dsa_call(qit, wit, qt, ki, k4, vt, bsz, seq)
        xs = _out_ffn_call(
            xs.reshape(bsz * seq, d), yab, yc, yd, w_out[l].astype(bf16), g_ffn[l][None, :],
            w_gate[l].astype(bf16), w_up[l].astype(bf16), w_down[l].astype(bf16),
            g_final[None, :], final=(l == depth - 1))
    return xs.reshape(bsz, seq, d)
```

```python
import functools
import math

import jax
import jax.numpy as jnp
import numpy as np
from jax import lax
from jax.experimental import pallas as pl
from jax.experimental.pallas import tpu as pltpu

D_MODEL = 1024
GROUP = 256
HEADS = 4
HEAD_DIM = 64
CHUNK = 128
IDX_HEADS = 4
IDX_DIM = 64
TOPK = 256
SHORT_W = 3
CONF_W = 31
FFN_HIDDEN = 2816
ROPE_THETA = 10000.0
NORM_EPS = 1e-6
LN_EPS = 1e-5
NEG = -1e30
NEG_BF16 = float(np.asarray(NEG, dtype=jnp.bfloat16).astype(np.float32))
BELOW_NEG = -3.0e38

LANES = 128
SUBLANES = 8
A_HALO = 8
D_HALO = 32

COL_A = 0
COL_B = COL_A + 3 * GROUP
COL_K = COL_B + 2 * GROUP
COL_KI = COL_K + GROUP
COL_D = COL_KI + LANES
ROW_COLS = COL_D + 2 * GROUP
TROW_Q = 0
TROW_V = TROW_Q + GROUP
TROW_QI = TROW_V + GROUP
TROW_WI = TROW_QI + IDX_HEADS * IDX_DIM
WI_ROWS = 16
TROWS = TROW_WI + WI_ROWS

MIX_TM = 512
DSA_QB = 512
DSA_KC = 512
DSA_SUB = 64
CNT_ROWS = 16
CNTB_ROWS = 32
TIE_TILE = 128
BF16_ROWS = 16
SHIFT_LIMIT = 45.0
FFN_TM = 512
VMEM_LIMIT = 56 * 1024 * 1024

_NT = (((1,), (1,)), ((), ()))


def _rmsnorm(xf, g):
    ms = jnp.mean(xf * xf, axis=-1, keepdims=True)
    return xf * lax.rsqrt(ms + NORM_EPS) * g


def _layernorm(xf, g, b):
    mu = jnp.mean(xf, axis=-1, keepdims=True)
    xc = xf - mu
    var = jnp.mean(xc * xc, axis=-1, keepdims=True)
    return xc * lax.rsqrt(var + LN_EPS) * g + b


def _rope128(x, cos, sin_signed):
    half = HEAD_DIM // 2
    lane = lax.broadcasted_iota(jnp.int32, x.shape, 1) % HEAD_DIM
    up = pltpu.roll(x, LANES - half, axis=1)
    dn = pltpu.roll(x, half, axis=1)
    return x * cos + jnp.where(lane < half, up, dn) * sin_signed


def _rope_t(x, cos_t, sin_t):
    half = HEAD_DIM // 2
    x1, x2 = x[0:half, :], x[half:HEAD_DIM, :]
    return x1 * cos_t - x2 * sin_t, x2 * cos_t + x1 * sin_t


def _mix_kernel(x_ref, g_ref, w_ref, wt_ref, cos_ref, sin_ref, cost_ref, sint_ref,
                wca_ref, lngb_ref, lnbb_ref, ws_ref, bsb_ref, wcf_ref, bcf_ref,
                lngd_ref, lnbd_ref,
                yab_ref, yd_ref, k4_ref, ki_ref, qt_ref, vt_ref, qit_ref, wit_ref,
                abuf, dbuf, sbuf):
    tm = x_ref.shape[0]
    f32, bf16 = jnp.float32, jnp.bfloat16
    half = HEAD_DIM // 2

    @pl.when(pl.program_id(1) == 0)
    def _():
        abuf[0:A_HALO, :] = jnp.zeros((A_HALO, GROUP), f32)
        dbuf[0:D_HALO, :] = jnp.zeros((D_HALO, GROUP), f32)

    h = _rmsnorm(x_ref[...], g_ref[...]).astype(bf16)

    def proj(lo, hi):
        return jnp.dot(h, w_ref[:, lo:hi], preferred_element_type=f32)

    def proj_t(lo, hi):
        return lax.dot_general(wt_ref[lo:hi, :], h, _NT, preferred_element_type=f32)

    zd = proj(COL_D, ROW_COLS)
    dbuf[D_HALO:D_HALO + tm, :] = zd[:, 0:GROUP] * jax.nn.sigmoid(zd[:, GROUP:2 * GROUP])
    acc = jnp.broadcast_to(bcf_ref[...], (tm, GROUP))
    first = D_HALO - (CONF_W - 1)
    for r in range(SUBLANES):
        taps = [t for t in range(CONF_W) if (first + t) % SUBLANES == r]
        span = (first + taps[-1]) - (first + taps[0]) + tm
        sbuf[0:span, :] = dbuf[pl.ds(first + taps[0], span), :]
        for t in taps:
            lo = t - taps[0]
            acc = acc + sbuf[lo:lo + tm, :] * wcf_ref[t:t + 1, :]
    y = _layernorm(acc, lngd_ref[...], lnbd_ref[...])
    yd_ref[...] = (y * jax.nn.sigmoid(y)).astype(bf16)
    dbuf[0:D_HALO, :] = dbuf[tm:tm + D_HALO, :]

    za = proj(COL_A, COL_B)
    abuf[A_HALO:A_HALO + tm, :] = za[:, GROUP:2 * GROUP] * za[:, 2 * GROUP:3 * GROUP]
    conv = abuf[pl.ds(A_HALO - 2, tm), :] * wca_ref[0:1, :]
    conv = conv + abuf[pl.ds(A_HALO - 1, tm), :] * wca_ref[1:2, :]
    conv = conv + abuf[pl.ds(A_HALO, tm), :] * wca_ref[2:3, :]
    yab_ref[:, 0:GROUP] = (za[:, 0:GROUP] * conv).astype(bf16)
    abuf[0:A_HALO, :] = abuf[tm:tm + A_HALO, :]

    zb = proj(COL_B, COL_K)
    vn = _layernorm(zb[:, GROUP:2 * GROUP], lngb_ref[...], lnbb_ref[...]).astype(bf16)
    r_i = lax.broadcasted_iota(jnp.int32, (CHUNK, CHUNK), 0)
    c_i = lax.broadcasted_iota(jnp.int32, (CHUNK, CHUNK), 1)
    wsm = [jnp.where(r_i >= c_i, ws_ref[hd], 0.0).astype(bf16) for hd in range(HEADS)]
    lane_head = lax.broadcasted_iota(jnp.int32, (CHUNK, GROUP), 1) // HEAD_DIM
    for c in range(tm // CHUNK):
        rows = slice(c * CHUNK, (c + 1) * CHUNK)
        vc = vn[rows, :]
        mixed = bsb_ref[...]
        for hd in range(HEADS):
            full = jnp.dot(wsm[hd], vc, preferred_element_type=f32)
            mixed = mixed + jnp.where(lane_head == hd, full, 0.0)
        yab_ref[rows, GROUP:2 * GROUP] = (zb[rows, 0:GROUP] * mixed).astype(bf16)

    cos = cos_ref[...]
    sin = sin_ref[...]
    zk = proj(COL_K, COL_D)
    lane = lax.broadcasted_iota(jnp.int32, (tm, LANES), 1)
    bias_col = jnp.where(lane == HEAD_DIM, 1.0, 0.0)
    for pair in range(2):
        kp = _rope128(zk[:, pair * LANES:(pair + 1) * LANES], cos, sin)
        for sub in range(2):
            kh = kp if sub == 0 else pltpu.roll(kp, HEAD_DIM, axis=1)
            k4_ref[2 * pair + sub] = jnp.where(lane < HEAD_DIM, kh, bias_col).astype(bf16)
    kip = _rope128(zk[:, GROUP:GROUP + LANES], cos, sin)
    ki_ref[...] = kip[:, 0:IDX_DIM].astype(bf16)

    cos_t = cost_ref[...]
    sin_t = sint_ref[...]
    q_scale = HEAD_DIM ** -0.5 * math.log2(math.e)
    zq = proj_t(TROW_Q, TROW_V)
    zqi = proj_t(TROW_QI, TROW_WI)
    for hd in range(HEADS):
        rows = slice(hd * HEAD_DIM, (hd + 1) * HEAD_DIM)
        o1, o2 = _rope_t(zq[rows, :], cos_t, sin_t)
        qt_ref[hd, 0:half, :] = (o1 * q_scale).astype(bf16)
        qt_ref[hd, half:HEAD_DIM, :] = (o2 * q_scale).astype(bf16)
        o1, o2 = _rope_t(zqi[rows, :], cos_t, sin_t)
        qit_ref[hd, 0:half, :] = o1.astype(bf16)
        qit_ref[hd, half:HEAD_DIM, :] = o2.astype(bf16)
    zv = proj_t(TROW_V, TROW_QI)
    for hd in range(HEADS):
        vt_ref[hd] = zv[hd * HEAD_DIM:(hd + 1) * HEAD_DIM, :].astype(bf16)
    zw = proj_t(TROW_WI, TROWS)
    wit_ref[...] = zw[0:SUBLANES, :] * (IDX_HEADS ** -0.5 * IDX_DIM ** -0.5)


def _mix_call(x, g, w_rows, w_t, tables, wca, lngb, lnbb, ws, bsb, wcf, bcf, lngd, lnbd):
    bsz, seq, _ = x.shape
    n = bsz * seq
    tm = MIX_TM
    nt = seq // tm
    bf16 = jnp.bfloat16
    half = HEAD_DIM // 2
    cos, sin, cos_t, sin_t = tables
    row = lambda b, i: (b * nt + i, 0)
    row4 = lambda b, i: (0, b * nt + i, 0)
    col = lambda b, i: (0, b * nt + i)
    col4 = lambda b, i: (0, 0, b * nt + i)
    const2 = lambda b, i: (0, 0)
    const3 = lambda b, i: (0, 0, 0)
    in_specs = [
        pl.BlockSpec((None, tm, D_MODEL), lambda b, i: (b, i, 0)),
        pl.BlockSpec((1, D_MODEL), const2),
        pl.BlockSpec((D_MODEL, ROW_COLS), const2),
        pl.BlockSpec((TROWS, D_MODEL), const2),
        pl.BlockSpec((tm, LANES), lambda b, i: (i, 0)),
        pl.BlockSpec((tm, LANES), lambda b, i: (i, 0)),
        pl.BlockSpec((half, tm), lambda b, i: (0, i)),
        pl.BlockSpec((half, tm), lambda b, i: (0, i)),
        pl.BlockSpec((SHORT_W, GROUP), const2),
        pl.BlockSpec((1, GROUP), const2),
        pl.BlockSpec((1, GROUP), const2),
        pl.BlockSpec((HEADS, CHUNK, CHUNK), const3),
        pl.BlockSpec((CHUNK, GROUP), const2),
        pl.BlockSpec((CONF_W, GROUP), const2),
        pl.BlockSpec((1, GROUP), const2),
        pl.BlockSpec((1, GROUP), const2),
        pl.BlockSpec((1, GROUP), const2),
    ]
    out_shape = [
        jax.ShapeDtypeStruct((n, 2 * GROUP), bf16),
        jax.ShapeDtypeStruct((n, GROUP), bf16),
        jax.ShapeDtypeStruct((HEADS, n, LANES), bf16),
        jax.ShapeDtypeStruct((n, IDX_DIM), bf16),
        jax.ShapeDtypeStruct((HEADS, HEAD_DIM, n), bf16),
        jax.ShapeDtypeStruct((HEADS, HEAD_DIM, n), bf16),
        jax.ShapeDtypeStruct((IDX_HEADS, IDX_DIM, n), bf16),
        jax.ShapeDtypeStruct((SUBLANES, n), jnp.float32),
    ]
    out_specs = [
        pl.BlockSpec((tm, 2 * GROUP), row),
        pl.BlockSpec((tm, GROUP), row),
        pl.BlockSpec((HEADS, tm, LANES), row4),
        pl.BlockSpec((tm, IDX_DIM), row),
        pl.BlockSpec((HEADS, HEAD_DIM, tm), col4),
        pl.BlockSpec((HEADS, HEAD_DIM, tm), col4),
        pl.BlockSpec((IDX_HEADS, IDX_DIM, tm), col4),
        pl.BlockSpec((SUBLANES, tm), col),
    ]
    return pl.pallas_call(
        _mix_kernel,
        grid=(bsz, nt),
        in_specs=in_specs,
        out_specs=out_specs,
        out_shape=out_shape,
        scratch_shapes=[pltpu.VMEM((tm + A_HALO, GROUP), jnp.float32),
                        pltpu.VMEM((tm + D_HALO, GROUP), jnp.float32),
                        pltpu.VMEM((tm + D_HALO, GROUP), jnp.float32)],
        compiler_params=pltpu.CompilerParams(
            dimension_semantics=("arbitrary", "arbitrary"),
            vmem_limit_bytes=VMEM_LIMIT),
        name="mix",
    )(x, g, w_rows, w_t, cos, sin, cos_t, sin_t, wca, lngb, lnbb, ws, bsb, wcf, bcf,
      lngd, lnbd)


def _threshold_bits(trial):
    bits = jnp.where(trial < 0, trial & jnp.int32(0x7FFFFFFF), ~trial)
    return pltpu.bitcast(bits, jnp.float32)


def _dsa_kernel(qit_ref, wit_ref, qt_ref, ki_ref, k4_ref, vt_ref, o_ref,
                sc_ref, scb_ref, acc_ref, s_ref, p_ref, qa_ref, l_ref, kmax_ref, *, seq):
    f32, bf16, i32 = jnp.float32, jnp.bfloat16, jnp.int32
    qb = wit_ref.shape[1]
    kc, sub = DSA_KC, DSA_SUB
    ndiag = qb // kc
    i = pl.program_id(1)
    nfull = i * ndiag
    nchunks = nfull + ndiag
    n_outside = seq - nchunks * kc
    key_i = lax.broadcasted_iota(i32, (sub, qb), 0)
    qry_i = lax.broadcasted_iota(i32, (sub, qb), 1)

    def causal(x, d, r):
        return jnp.where(key_i + (d * kc + r) <= qry_i, x, NEG)

    def run_chunks(chunk_fn, carry):
        carry = lax.fori_loop(0, nfull, lambda c, cr: chunk_fn(c, None, cr), carry)
        for d in range(ndiag):
            carry = chunk_fn(nfull + d, d, carry)
        return carry

    def score_chunk(c, d, carry):
        off = pl.multiple_of(c * kc, kc)
        for t in range(kc // sub):
            r0 = off + t * sub
            kis = ki_ref[pl.ds(r0, sub), :]
            sc = None
            for j in range(IDX_HEADS):
                rel = jnp.dot(kis, qit_ref[j], preferred_element_type=f32)
                term = jnp.maximum(rel, 0.0) * wit_ref[j:j + 1, :]
                sc = term if sc is None else sc + term
            if d is not None:
                sc = causal(sc, d, t * sub)
            sc_ref[pl.ds(r0, sub), :] = sc
            scb_ref[pl.ds(r0, sub), :] = sc.astype(bf16)
        return carry

    run_chunks(score_chunk, 0)

    def count(indicator):
        def body(c, acc):
            off = pl.multiple_of(c * kc, kc)
            for t in range(kc // CNT_ROWS):
                r0 = off + t * CNT_ROWS
                acc = acc + indicator(sc_ref[pl.ds(r0, CNT_ROWS), :], r0)
            return acc
        acc = lax.fori_loop(0, nchunks, body, jnp.zeros((CNT_ROWS, qb), i32))
        return jnp.sum(acc, axis=0, keepdims=True)

    def count_ge(thr):
        inside = count(lambda s, r0: jnp.where(s >= thr, 1, 0))
        return inside + jnp.where(thr <= NEG, n_outside, 0)

    neg_b = NEG_BF16
    one_b, zero_b = jnp.ones((), bf16), jnp.zeros((), bf16)

    def count_ge_b(thr):
        thr_b = thr.astype(bf16)

        def body(c, acc):
            off = pl.multiple_of(c * kc, kc)
            for t in range(kc // CNTB_ROWS):
                x = scb_ref[pl.ds(off + t * CNTB_ROWS, CNTB_ROWS), :]
                acc = acc + jnp.where(x >= thr_b, one_b, zero_b)
            return acc

        acc = lax.fori_loop(0, nchunks, body, jnp.zeros((CNTB_ROWS, qb), bf16))
        inside = jnp.sum(acc.astype(f32), axis=0, keepdims=True).astype(i32)
        return inside + jnp.where(thr_b.astype(f32) <= neg_b, n_outside, 0)

    def coarse_step(step, prefix):
        trial = prefix | jnp.left_shift(jnp.int32(1), 31 - step)
        cnt = count_ge_b(_threshold_bits(trial))
        return jnp.where(cnt >= TOPK, trial, prefix)

    coarse = lax.fori_loop(0, 16, coarse_step, jnp.zeros((1, qb), i32))

    start = coarse - jnp.int32(1 << 16)
    c_start = count_ge(_threshold_bits(start))

    def fine_step(step, state):
        prefix, c_prefix = state
        trial = prefix + jnp.left_shift(jnp.int32(1), 16 - step)
        cnt = count_ge(_threshold_bits(trial))
        accept = cnt >= TOPK
        return jnp.where(accept, trial, prefix), jnp.where(accept, cnt, c_prefix)

    prefix, c_ge = lax.fori_loop(0, 17, fine_step, (start, c_start))
    vstar = _threshold_bits(prefix)

    @pl.when(jnp.max(c_ge) > TOPK)
    def _():
        c_gt = (count(lambda s, r0: jnp.where(s > vstar, 1, 0))
                + jnp.where(vstar < NEG, n_outside, 0))
        want = (TOPK - c_gt).astype(f32)
        tt = TIE_TILE
        tri = jnp.where(lax.broadcasted_iota(i32, (tt, tt), 0)
                        >= lax.broadcasted_iota(i32, (tt, tt), 1), 1.0, 0.0).astype(bf16)

        def tie_body(c, seen):
            off = pl.multiple_of(c * kc, kc)
            running = []
            for t in range(kc // tt):
                s = sc_ref[pl.ds(off + t * tt, tt), :]
                tie = jnp.where(s == vstar, 1.0, 0.0).astype(bf16)
                running.append(jnp.dot(tri, tie, preferred_element_type=f32))
            for t in range(kc // tt):
                s = sc_ref[pl.ds(off + t * tt, tt), :]
                rank = seen + running[t]
                sc_ref[pl.ds(off + t * tt, tt), :] = jnp.where(
                    s == vstar, jnp.where(rank <= want, s, BELOW_NEG), s)
                seen = seen + running[t][tt - 1:tt, :]
            return seen

        lax.fori_loop(0, nchunks, tie_body, jnp.zeros((1, qb), f32))

    def fold8(x, op):
        return op(x.reshape(sub // SUBLANES, SUBLANES, qb), axis=0)

    def cap_chunk(c, d, carry):
        off = pl.multiple_of(c * kc, kc)
        for t in range(kc // sub):
            r = t * sub
            cap = jnp.where(sc_ref[pl.ds(off + r, sub), :] >= vstar, jnp.inf, NEG)
            if d is not None:
                cap = causal(cap, d, r)
            sc_ref[pl.ds(off + r, sub), :] = cap
        return carry

    run_chunks(cap_chunk, 0)
    acc_ref[...] = jnp.zeros(acc_ref.shape, f32)

    @pl.when(i == 0)
    def _():
        kmax_ref[...] = jnp.zeros(kmax_ref.shape, f32)

    bounds = []
    for hd in range(HEADS):
        kk = k4_ref[hd, pl.ds(pl.multiple_of(i * qb, qb), qb), :].astype(f32)
        k2 = jnp.max(jnp.sum(kk * kk, axis=1, keepdims=True), axis=0, keepdims=True)
        kmax_ref[hd] = jnp.maximum(kmax_ref[hd], k2)
        qf = qt_ref[hd].astype(f32)
        q2 = jnp.sum(qf * qf, axis=0, keepdims=True)
        bounds.append(jnp.sqrt(q2 * kmax_ref[hd][0:1, 0:1]))
    worst = bounds[0]
    for hd in range(1, HEADS):
        worst = jnp.maximum(worst, bounds[hd])
    fast = jnp.max(worst) <= SHIFT_LIMIT

    row16 = lax.broadcasted_iota(i32, (BF16_ROWS, qb), 0)
    for hd in range(HEADS):
        qa_ref[hd, 0:HEAD_DIM, :] = qt_ref[hd]
        shift = jnp.where(fast, -bounds[hd], 0.0)
        qa_ref[hd, HEAD_DIM:HEAD_DIM + BF16_ROWS, :] = jnp.where(
            row16 == 0, shift, 0.0).astype(bf16)
        qa_ref[hd, HEAD_DIM + BF16_ROWS:LANES, :] = jnp.zeros(
            (LANES - HEAD_DIM - BF16_ROWS, qb), bf16)

    def capped_logits(hd, off, r):
        s = jnp.dot(k4_ref[hd, pl.ds(off + r, sub), :], qa_ref[hd], preferred_element_type=f32)
        return jnp.minimum(s, sc_ref[pl.ds(off + r, sub), :])

    @pl.when(fast)
    def _():
        def chunk(c, d, lparts):
            off = pl.multiple_of(c * kc, kc)
            lparts = list(lparts)
            for hd in range(HEADS):
                for t in range(kc // sub):
                    r = t * sub
                    p = jnp.exp2(capped_logits(hd, off, r))
                    lparts[hd] = lparts[hd] + fold8(p, jnp.sum)
                    p_ref[hd, r:r + sub, :] = p.astype(bf16)
            for hd in range(HEADS):
                acc_ref[hd] += jnp.dot(vt_ref[hd, :, pl.ds(off, kc)], p_ref[hd],
                                       preferred_element_type=f32)
            return tuple(lparts)

        lparts = run_chunks(chunk, tuple(jnp.zeros((SUBLANES, qb), f32) for _ in range(HEADS)))
        for hd in range(HEADS):
            l_ref[hd] = lparts[hd]

    @pl.when(jnp.logical_not(fast))
    def _():
        def attend_chunk(c, d, carry):
            ms, ls = carry
            off = pl.multiple_of(c * kc, kc)
            new_ms, new_ls = list(ms), list(ls)

            def logits_stage(hd):
                mpart = None
                for t in range(kc // sub):
                    r = t * sub
                    s = capped_logits(hd, off, r)
                    s_ref[hd, r:r + sub, :] = s
                    m8 = fold8(s, jnp.max)
                    mpart = m8 if mpart is None else jnp.maximum(mpart, m8)
                new_ms[hd] = jnp.maximum(ms[hd], jnp.max(mpart, axis=0, keepdims=True))

            def value_stage(hd):
                lpart = None
                for t in range(kc // sub):
                    r = t * sub
                    p = jnp.exp2(s_ref[hd, r:r + sub, :] - new_ms[hd])
                    l8 = fold8(p, jnp.sum)
                    lpart = l8 if lpart is None else lpart + l8
                    p_ref[hd, r:r + sub, :] = p.astype(bf16)
                alpha = jnp.exp2(ms[hd] - new_ms[hd])
                new_ls[hd] = alpha * ls[hd] + jnp.sum(lpart, axis=0, keepdims=True)
                pv = jnp.dot(vt_ref[hd, :, pl.ds(off, kc)], p_ref[hd],
                             preferred_element_type=f32)
                acc_ref[hd] = acc_ref[hd] * alpha + pv

            logits_stage(0)
            for hd in range(HEADS):
                if hd + 1 < HEADS:
                    logits_stage(hd + 1)
                value_stage(hd)
            return tuple(new_ms), tuple(new_ls)

        init = (tuple(jnp.full((1, qb), NEG, f32) for _ in range(HEADS)),
                tuple(jnp.zeros((1, qb), f32) for _ in range(HEADS)))
        _, ls = run_chunks(attend_chunk, init)
        row8 = lax.broadcasted_iota(i32, (SUBLANES, qb), 0)
        for hd in range(HEADS):
            l_ref[hd] = jnp.where(row8 == 0, ls[hd], 0.0)

    out_t = jnp.concatenate(
        [acc_ref[hd] / jnp.sum(l_ref[hd], axis=0, keepdims=True) for hd in range(HEADS)],
        axis=0)
    o_ref[...] = out_t.T.astype(o_ref.dtype)


def _dsa_call(qit, wit, qt, ki, k4, vt, bsz, seq):
    n = bsz * seq
    qb = DSA_QB
    nq = seq // qb
    qcol4 = lambda b, i: (0, 0, b * nq + i)
    return pl.pallas_call(
        functools.partial(_dsa_kernel, seq=seq),
        grid=(bsz, nq),
        in_specs=[
            pl.BlockSpec((IDX_HEADS, IDX_DIM, qb), qcol4),
            pl.BlockSpec((SUBLANES, qb), lambda b, i: (0, b * nq + i)),
            pl.BlockSpec((HEADS, HEAD_DIM, qb), qcol4),
            pl.BlockSpec((seq, IDX_DIM), lambda b, i: (b, 0)),
            pl.BlockSpec((HEADS, seq, LANES), lambda b, i: (0, b, 0)),
            pl.BlockSpec((HEADS, HEAD_DIM, seq), lambda b, i: (0, 0, b)),
        ],
        out_specs=pl.BlockSpec((qb, GROUP), lambda b, i: (b * nq + i, 0)),
        out_shape=jax.ShapeDtypeStruct((n, GROUP), jnp.bfloat16),
        scratch_shapes=[
            pltpu.VMEM((seq, qb), jnp.float32),
            pltpu.VMEM((seq, qb), jnp.bfloat16),
            pltpu.VMEM((HEADS, HEAD_DIM, qb), jnp.float32),
            pltpu.VMEM((HEADS, DSA_KC, qb), jnp.float32),
            pltpu.VMEM((HEADS, DSA_KC, qb), jnp.bfloat16),
            pltpu.VMEM((HEADS, LANES, qb), jnp.bfloat16),
            pltpu.VMEM((HEADS, SUBLANES, qb), jnp.float32),
            pltpu.VMEM((HEADS, SUBLANES, LANES), jnp.float32),
        ],
        compiler_params=pltpu.CompilerParams(
            dimension_semantics=("arbitrary", "arbitrary"),
            vmem_limit_bytes=VMEM_LIMIT),
        name="dsa",
    )(qit, wit, qt, ki, k4, vt)


def _out_ffn_kernel(x_ref, yab_ref, yc_ref, yd_ref, wout_ref, gffn_ref, wg_ref, wu_ref,
                    wd_ref, gfin_ref, o_ref, *, final):
    f32, bf16 = jnp.float32, jnp.bfloat16
    y = jnp.concatenate([yab_ref[...], yc_ref[...], yd_ref[...]], axis=1)
    x1 = x_ref[...] + jnp.dot(y, wout_ref[...], preferred_element_type=f32)
    hf = _rmsnorm(x1, gffn_ref[...]).astype(bf16)
    gate = jnp.dot(hf, wg_ref[...], preferred_element_type=f32)
    up = jnp.dot(hf, wu_ref[...], preferred_element_type=f32)
    act = (gate * jax.nn.sigmoid(gate) * up).astype(bf16)
    out = x1 + jnp.dot(act, wd_ref[...], preferred_element_type=f32)
    if final:
        out = _rmsnorm(out, gfin_ref[...])
    o_ref[...] = out


def _out_ffn_call(x2d, yab, yc, yd, wout, gffn, wg, wu, wd, gfin, layer, final):
    n = x2d.shape[0]
    tm = FFN_TM
    row = lambda i: (i, 0)
    const = lambda i: (0, 0)
    resident = lambda shape: pl.BlockSpec((None,) + shape, lambda i: (layer, 0, 0),
                                          pipeline_mode=pl.Buffered(1))
    return pl.pallas_call(
        functools.partial(_out_ffn_kernel, final=final),
        grid=(n // tm,),
        in_specs=[
            pl.BlockSpec((tm, D_MODEL), row),
            pl.BlockSpec((tm, 2 * GROUP), row),
            pl.BlockSpec((tm, GROUP), row),
            pl.BlockSpec((tm, GROUP), row),
            resident((D_MODEL, D_MODEL)),
            pl.BlockSpec((1, D_MODEL), const),
            resident((D_MODEL, FFN_HIDDEN)),
            resident((D_MODEL, FFN_HIDDEN)),
            resident((FFN_HIDDEN, D_MODEL)),
            pl.BlockSpec((1, D_MODEL), const),
        ],
        out_specs=pl.BlockSpec((tm, D_MODEL), row),
        out_shape=jax.ShapeDtypeStruct((n, D_MODEL), jnp.float32),
        compiler_params=pltpu.CompilerParams(
            dimension_semantics=("arbitrary",),
            vmem_limit_bytes=VMEM_LIMIT),
        name="out_ffn",
    )(x2d, yab, yc, yd, wout, gffn, wg, wu, wd, gfin)


def _pack_w_in(w):
    g = GROUP
    cuts = [0, 3 * g, 5 * g, 6 * g, 7 * g, 8 * g, 9 * g, 9 * g + IDX_DIM,
            9 * g + IDX_DIM + IDX_HEADS, 11 * g + IDX_DIM + IDX_HEADS]
    a, b, q, k, v, qi, ki, wi, d = [w[:, cuts[t]:cuts[t + 1]] for t in range(9)]
    z = lambda cols: jnp.zeros((w.shape[0], cols), w.dtype)
    rows = jnp.concatenate([a, b, k, ki, z(LANES - IDX_DIM), d], axis=1)
    trans = jnp.concatenate([q, v, qi, wi, z(WI_ROWS - IDX_HEADS)], axis=1).T
    return rows.astype(jnp.bfloat16), trans.astype(jnp.bfloat16)


def _rope_tables(seq):
    inv_freq = ROPE_THETA ** (-jnp.arange(0, HEAD_DIM, 2, dtype=jnp.float32) / HEAD_DIM)
    ang = jnp.arange(seq, dtype=jnp.float32)[:, None] * inv_freq[None, :]
    c, s = jnp.cos(ang), jnp.sin(ang)
    reps = LANES // HEAD_DIM
    cos = jnp.tile(jnp.concatenate([c, c], axis=1), (1, reps))
    sin = jnp.tile(jnp.concatenate([-s, s], axis=1), (1, reps))
    return cos, sin, c.T, s.T


def kernel(x, g_mix, w_in, w_conv_a, gmlp_ln_g, gmlp_ln_b, w_s, b_s, w_conf, b_conf,
           conf_ln_g, conf_ln_b, w_out, g_ffn, w_gate, w_up, w_down, g_final):
    bsz, seq, d = x.shape
    depth = w_in.shape[0]
    bf16 = jnp.bfloat16
    tables = _rope_tables(seq)
    w_out_b, w_gate_b, w_up_b, w_down_b = (
        w.astype(bf16) for w in (w_out, w_gate, w_up, w_down))
    xs = x
    for l in range(depth):
        w_rows, w_t = _pack_w_in(w_in[l])
        yab, yd, k4, ki, qt, vt, qit, wit = _mix_call(
            xs.reshape(bsz, seq, d), g_mix[l][None, :], w_rows, w_t, tables,
            w_conv_a[l], gmlp_ln_g[l][None, :], gmlp_ln_b[l][None, :], w_s[l],
            jnp.repeat(b_s[l].T, HEAD_DIM, axis=1), w_conf[l], b_conf[l][None, :],
            conf_ln_g[l][None, :], conf_ln_b[l][None, :])
        yc = _dsa_call(qit, wit, qt, ki, k4, vt, bsz, seq)
        xs = _out_ffn_call(
            xs.reshape(bsz * seq, d), yab, yc, yd, w_out_b, g_ffn[l][None, :],
            w_gate_b, w_up_b, w_down_b, g_final[None, :], layer=l, final=(l == depth - 1))
    return xs.reshape(bsz, seq, d)
```

```python
import functools
import math

import jax
import jax.numpy as jnp
import numpy as np
from jax import lax
from jax.experimental import pallas as pl
from jax.experimental.pallas import tpu as pltpu

D_MODEL = 1024
GROUP = 256
HEADS = 4
HEAD_DIM = 64
CHUNK = 128
IDX_HEADS = 4
IDX_DIM = 64
TOPK = 256
SHORT_W = 3
CONF_W = 31
FFN_HIDDEN = 2816
ROPE_THETA = 10000.0
NORM_EPS = 1e-6
LN_EPS = 1e-5
NEG = -1e30
NEG_BF16 = float(np.asarray(NEG, dtype=jnp.bfloat16).astype(np.float32))
BELOW_NEG = -3.0e38

LANES = 128
SUBLANES = 8
A_HALO = 8
D_HALO = 32

COL_A = 0
COL_B = COL_A + 3 * GROUP
COL_K = COL_B + 2 * GROUP
COL_KI = COL_K + GROUP
COL_D = COL_KI + LANES
ROW_COLS = COL_D + 2 * GROUP
TROW_Q = 0
TROW_V = TROW_Q + GROUP
TROW_QI = TROW_V + GROUP
TROW_WI = TROW_QI + IDX_HEADS * IDX_DIM
WI_ROWS = 16
TROWS = TROW_WI + WI_ROWS

MIX_TM = 512
DSA_QB = 512
DSA_KC = 512
DSA_SUB = 64
CNT_ROWS = 16
CNTB_ROWS = 32
TIE_TILE = 128
BF16_ROWS = 16
SHIFT_LIMIT = 45.0
FFN_TM = 512
VMEM_LIMIT = 56 * 1024 * 1024

_NT = (((1,), (1,)), ((), ()))


def _rmsnorm(xf, g):
    ms = jnp.mean(xf * xf, axis=-1, keepdims=True)
    return xf * lax.rsqrt(ms + NORM_EPS) * g


def _layernorm(xf, g, b):
    mu = jnp.mean(xf, axis=-1, keepdims=True)
    xc = xf - mu
    var = jnp.mean(xc * xc, axis=-1, keepdims=True)
    return xc * lax.rsqrt(var + LN_EPS) * g + b


def _rope128(x, cos, sin_signed):
    half = HEAD_DIM // 2
    lane = lax.broadcasted_iota(jnp.int32, x.shape, 1) % HEAD_DIM
    up = pltpu.roll(x, LANES - half, axis=1)
    dn = pltpu.roll(x, half, axis=1)
    return x * cos + jnp.where(lane < half, up, dn) * sin_signed


def _rope_t(x, cos_t, sin_t):
    half = HEAD_DIM // 2
    x1, x2 = x[0:half, :], x[half:HEAD_DIM, :]
    return x1 * cos_t - x2 * sin_t, x2 * cos_t + x1 * sin_t


def _mix_kernel(x_ref, g_ref, w_ref, wt_ref, cos_ref, sin_ref, cost_ref, sint_ref,
                wca_ref, lngb_ref, lnbb_ref, ws_ref, bsb_ref, wcf_ref, bcf_ref,
                lngd_ref, lnbd_ref,
                yab_ref, yd_ref, k4_ref, ki_ref, qt_ref, vt_ref, qit_ref, wit_ref,
                abuf, dbuf, sbuf):
    tm = x_ref.shape[0]
    f32, bf16 = jnp.float32, jnp.bfloat16
    half = HEAD_DIM // 2

    @pl.when(pl.program_id(1) == 0)
    def _():
        abuf[0:A_HALO, :] = jnp.zeros((A_HALO, GROUP), f32)
        dbuf[0:D_HALO, :] = jnp.zeros((D_HALO, GROUP), f32)

    h = _rmsnorm(x_ref[...], g_ref[...]).astype(bf16)

    def proj(lo, hi):
        return jnp.dot(h, w_ref[:, lo:hi], preferred_element_type=f32)

    def proj_t(lo, hi):
        return lax.dot_general(wt_ref[lo:hi, :], h, _NT, preferred_element_type=f32)

    zd = proj(COL_D, ROW_COLS)
    dbuf[D_HALO:D_HALO + tm, :] = zd[:, 0:GROUP] * jax.nn.sigmoid(zd[:, GROUP:2 * GROUP])
    acc = jnp.broadcast_to(bcf_ref[...], (tm, GROUP))
    first = D_HALO - (CONF_W - 1)
    for r in range(SUBLANES):
        taps = [t for t in range(CONF_W) if (first + t) % SUBLANES == r]
        span = (first + taps[-1]) - (first + taps[0]) + tm
        sbuf[0:span, :] = dbuf[pl.ds(first + taps[0], span), :]
        for t in taps:
            lo = t - taps[0]
            acc = acc + sbuf[lo:lo + tm, :] * wcf_ref[t:t + 1, :]
    y = _layernorm(acc, lngd_ref[...], lnbd_ref[...])
    yd_ref[...] = (y * jax.nn.sigmoid(y)).astype(bf16)
    dbuf[0:D_HALO, :] = dbuf[tm:tm + D_HALO, :]

    za = proj(COL_A, COL_B)
    abuf[A_HALO:A_HALO + tm, :] = za[:, GROUP:2 * GROUP] * za[:, 2 * GROUP:3 * GROUP]
    conv = abuf[pl.ds(A_HALO - 2, tm), :] * wca_ref[0:1, :]
    conv = conv + abuf[pl.ds(A_HALO - 1, tm), :] * wca_ref[1:2, :]
    conv = conv + abuf[pl.ds(A_HALO, tm), :] * wca_ref[2:3, :]
    yab_ref[:, 0:GROUP] = (za[:, 0:GROUP] * conv).astype(bf16)
    abuf[0:A_HALO, :] = abuf[tm:tm + A_HALO, :]

    zb = proj(COL_B, COL_K)
    vn = _layernorm(zb[:, GROUP:2 * GROUP], lngb_ref[...], lnbb_ref[...]).astype(bf16)
    r_i = lax.broadcasted_iota(jnp.int32, (CHUNK, CHUNK), 0)
    c_i = lax.broadcasted_iota(jnp.int32, (CHUNK, CHUNK), 1)
    wcat = jnp.concatenate(
        [jnp.where(r_i >= c_i, ws_ref[hd], 0.0).astype(bf16) for hd in range(HEADS)], axis=1)
    lane_head = lax.broadcasted_iota(jnp.int32, (CHUNK, GROUP), 1) // HEAD_DIM
    zero_b = jnp.zeros((), bf16)
    for c in range(tm // CHUNK):
        rows = slice(c * CHUNK, (c + 1) * CHUNK)
        vc = vn[rows, :]
        stacked = jnp.concatenate(
            [jnp.where(lane_head == hd, vc, zero_b) for hd in range(HEADS)], axis=0)
        mixed = bsb_ref[...] + jnp.dot(wcat, stacked, preferred_element_type=f32)
        yab_ref[rows, GROUP:2 * GROUP] = (zb[rows, 0:GROUP] * mixed).astype(bf16)

    cos = cos_ref[...]
    sin = sin_ref[...]
    zk = proj(COL_K, COL_D)
    lane = lax.broadcasted_iota(jnp.int32, (tm, LANES), 1)
    bias_col = jnp.where(lane == HEAD_DIM, 1.0, 0.0)
    for pair in range(2):
        kp = _rope128(zk[:, pair * LANES:(pair + 1) * LANES], cos, sin)
        for sub in range(2):
            kh = kp if sub == 0 else pltpu.roll(kp, HEAD_DIM, axis=1)
            k4_ref[2 * pair + sub] = jnp.where(lane < HEAD_DIM, kh, bias_col).astype(bf16)
    kip = _rope128(zk[:, GROUP:GROUP + LANES], cos, sin)
    ki_ref[...] = kip[:, 0:IDX_DIM].astype(bf16)

    cos_t = cost_ref[...]
    sin_t = sint_ref[...]
    q_scale = HEAD_DIM ** -0.5 * math.log2(math.e)
    zq = proj_t(TROW_Q, TROW_V)
    zqi = proj_t(TROW_QI, TROW_WI)
    for hd in range(HEADS):
        rows = slice(hd * HEAD_DIM, (hd + 1) * HEAD_DIM)
        o1, o2 = _rope_t(zq[rows, :], cos_t, sin_t)
        qt_ref[hd, 0:half, :] = (o1 * q_scale).astype(bf16)
        qt_ref[hd, half:HEAD_DIM, :] = (o2 * q_scale).astype(bf16)
        o1, o2 = _rope_t(zqi[rows, :], cos_t, sin_t)
        qit_ref[hd, 0:half, :] = o1.astype(bf16)
        qit_ref[hd, half:HEAD_DIM, :] = o2.astype(bf16)
    zv = proj_t(TROW_V, TROW_QI)
    for hd in range(HEADS):
        vt_ref[hd] = zv[hd * HEAD_DIM:(hd + 1) * HEAD_DIM, :].astype(bf16)
    zw = proj_t(TROW_WI, TROWS)
    wit_ref[...] = zw[0:SUBLANES, :] * (IDX_HEADS ** -0.5 * IDX_DIM ** -0.5)


def _mix_call(x, g, w_rows, w_t, tables, wca, lngb, lnbb, ws, bsb, wcf, bcf, lngd, lnbd):
    bsz, seq, _ = x.shape
    n = bsz * seq
    tm = MIX_TM
    nt = seq // tm
    bf16 = jnp.bfloat16
    half = HEAD_DIM // 2
    cos, sin, cos_t, sin_t = tables
    row = lambda b, i: (b * nt + i, 0)
    row4 = lambda b, i: (0, b * nt + i, 0)
    col = lambda b, i: (0, b * nt + i)
    col4 = lambda b, i: (0, 0, b * nt + i)
    const2 = lambda b, i: (0, 0)
    const3 = lambda b, i: (0, 0, 0)
    in_specs = [
        pl.BlockSpec((None, tm, D_MODEL), lambda b, i: (b, i, 0)),
        pl.BlockSpec((1, D_MODEL), const2),
        pl.BlockSpec((D_MODEL, ROW_COLS), const2),
        pl.BlockSpec((TROWS, D_MODEL), const2),
        pl.BlockSpec((tm, LANES), lambda b, i: (i, 0)),
        pl.BlockSpec((tm, LANES), lambda b, i: (i, 0)),
        pl.BlockSpec((half, tm), lambda b, i: (0, i)),
        pl.BlockSpec((half, tm), lambda b, i: (0, i)),
        pl.BlockSpec((SHORT_W, GROUP), const2),
        pl.BlockSpec((1, GROUP), const2),
        pl.BlockSpec((1, GROUP), const2),
        pl.BlockSpec((HEADS, CHUNK, CHUNK), const3),
        pl.BlockSpec((CHUNK, GROUP), const2),
        pl.BlockSpec((CONF_W, GROUP), const2),
        pl.BlockSpec((1, GROUP), const2),
        pl.BlockSpec((1, GROUP), const2),
        pl.BlockSpec((1, GROUP), const2),
    ]
    out_shape = [
        jax.ShapeDtypeStruct((n, 2 * GROUP), bf16),
        jax.ShapeDtypeStruct((n, GROUP), bf16),
        jax.ShapeDtypeStruct((HEADS, n, LANES), bf16),
        jax.ShapeDtypeStruct((n, IDX_DIM), bf16),
        jax.ShapeDtypeStruct((HEADS, HEAD_DIM, n), bf16),
        jax.ShapeDtypeStruct((HEADS, HEAD_DIM, n), bf16),
        jax.ShapeDtypeStruct((IDX_HEADS, IDX_DIM, n), bf16),
        jax.ShapeDtypeStruct((SUBLANES, n), jnp.float32),
    ]
    out_specs = [
        pl.BlockSpec((tm, 2 * GROUP), row),
        pl.BlockSpec((tm, GROUP), row),
        pl.BlockSpec((HEADS, tm, LANES), row4),
        pl.BlockSpec((tm, IDX_DIM), row),
        pl.BlockSpec((HEADS, HEAD_DIM, tm), col4),
        pl.BlockSpec((HEADS, HEAD_DIM, tm), col4),
        pl.BlockSpec((IDX_HEADS, IDX_DIM, tm), col4),
        pl.BlockSpec((SUBLANES, tm), col),
    ]
    return pl.pallas_call(
        _mix_kernel,
        grid=(bsz, nt),
        in_specs=in_specs,
        out_specs=out_specs,
        out_shape=out_shape,
        scratch_shapes=[pltpu.VMEM((tm + A_HALO, GROUP), jnp.float32),
                        pltpu.VMEM((tm + D_HALO, GROUP), jnp.float32),
                        pltpu.VMEM((tm + D_HALO, GROUP), jnp.float32)],
        compiler_params=pltpu.CompilerParams(
            dimension_semantics=("arbitrary", "arbitrary"),
            vmem_limit_bytes=VMEM_LIMIT),
        name="mix",
    )(x, g, w_rows, w_t, cos, sin, cos_t, sin_t, wca, lngb, lnbb, ws, bsb, wcf, bcf,
      lngd, lnbd)


def _threshold_bits(trial):
    bits = jnp.where(trial < 0, trial & jnp.int32(0x7FFFFFFF), ~trial)
    return pltpu.bitcast(bits, jnp.float32)


def _dsa_kernel(qit_ref, wit_ref, qt_ref, ki_ref, k4_ref, vt_ref, o_ref,
                sc_ref, scb_ref, acc_ref, s_ref, p_ref, qa_ref, l_ref, kmax_ref, *, seq):
    f32, bf16, i32 = jnp.float32, jnp.bfloat16, jnp.int32
    qb = wit_ref.shape[1]
    kc, sub = DSA_KC, DSA_SUB
    ndiag = qb // kc
    i = pl.program_id(1)
    nfull = i * ndiag
    nchunks = nfull + ndiag
    n_outside = seq - nchunks * kc
    key_i = lax.broadcasted_iota(i32, (sub, qb), 0)
    qry_i = lax.broadcasted_iota(i32, (sub, qb), 1)

    def causal(x, d, r):
        return jnp.where(key_i + (d * kc + r) <= qry_i, x, NEG)

    def run_chunks(chunk_fn, carry):
        carry = lax.fori_loop(0, nfull, lambda c, cr: chunk_fn(c, None, cr), carry)
        for d in range(ndiag):
            carry = chunk_fn(nfull + d, d, carry)
        return carry

    def score_chunk(c, d, carry):
        off = pl.multiple_of(c * kc, kc)
        for t in range(kc // sub):
            r0 = off + t * sub
            kis = ki_ref[pl.ds(r0, sub), :]
            sc = None
            for j in range(IDX_HEADS):
                rel = jnp.dot(kis, qit_ref[j], preferred_element_type=f32)
                term = jnp.maximum(rel, 0.0) * wit_ref[j:j + 1, :]
                sc = term if sc is None else sc + term
            if d is not None:
                sc = causal(sc, d, t * sub)
            sc_ref[pl.ds(r0, sub), :] = sc
            scb_ref[pl.ds(r0, sub), :] = sc.astype(bf16)
        return carry

    run_chunks(score_chunk, 0)

    def count(indicator):
        def body(c, acc):
            off = pl.multiple_of(c * kc, kc)
            for t in range(kc // CNT_ROWS):
                r0 = off + t * CNT_ROWS
                acc = acc + indicator(sc_ref[pl.ds(r0, CNT_ROWS), :], r0)
            return acc
        acc = lax.fori_loop(0, nchunks, body, jnp.zeros((CNT_ROWS, qb), i32))
        return jnp.sum(acc, axis=0, keepdims=True)

    def count_ge(thr):
        inside = count(lambda s, r0: jnp.where(s >= thr, 1, 0))
        return inside + jnp.where(thr <= NEG, n_outside, 0)

    neg_b = NEG_BF16
    one_b, zero_b = jnp.ones((), bf16), jnp.zeros((), bf16)

    def count_ge_b(thr):
        thr_b = thr.astype(bf16)

        def body(c, acc):
            off = pl.multiple_of(c * kc, kc)
            for t in range(kc // CNTB_ROWS):
                x = scb_ref[pl.ds(off + t * CNTB_ROWS, CNTB_ROWS), :]
                acc = acc + jnp.where(x >= thr_b, one_b, zero_b)
            return acc

        acc = lax.fori_loop(0, nchunks, body, jnp.zeros((CNTB_ROWS, qb), bf16))
        inside = jnp.sum(acc.astype(f32), axis=0, keepdims=True).astype(i32)
        return inside + jnp.where(thr_b.astype(f32) <= neg_b, n_outside, 0)

    def coarse_step(step, prefix):
        trial = prefix | jnp.left_shift(jnp.int32(1), 31 - step)
        cnt = count_ge_b(_threshold_bits(trial))
        return jnp.where(cnt >= TOPK, trial, prefix)

    coarse = lax.fori_loop(0, 16, coarse_step, jnp.zeros((1, qb), i32))

    start = coarse - jnp.int32(1 << 16)

    def fine_step(step, state):
        prefix, c_ge = state
        trial = prefix + jnp.left_shift(jnp.int32(1), 16 - step)
        cnt = count_ge(_threshold_bits(trial))
        accept = cnt >= TOPK
        return jnp.where(accept, trial, prefix), jnp.where(accept, cnt, c_ge)

    prefix, c_ge = lax.fori_loop(0, 17, fine_step, (start, jnp.full((1, qb), seq, i32)))
    vstar = _threshold_bits(prefix)

    @pl.when(jnp.max(c_ge) > TOPK)
    def _():
        c_gt = (count(lambda s, r0: jnp.where(s > vstar, 1, 0))
                + jnp.where(vstar < NEG, n_outside, 0))
        want = (TOPK - c_gt).astype(f32)
        tt = TIE_TILE
        tri = jnp.where(lax.broadcasted_iota(i32, (tt, tt), 0)
                        >= lax.broadcasted_iota(i32, (tt, tt), 1), 1.0, 0.0).astype(bf16)

        def tie_body(c, seen):
            off = pl.multiple_of(c * kc, kc)
            running = []
            for t in range(kc // tt):
                s = sc_ref[pl.ds(off + t * tt, tt), :]
                tie = jnp.where(s == vstar, 1.0, 0.0).astype(bf16)
                running.append(jnp.dot(tri, tie, preferred_element_type=f32))
            for t in range(kc // tt):
                s = sc_ref[pl.ds(off + t * tt, tt), :]
                rank = seen + running[t]
                sc_ref[pl.ds(off + t * tt, tt), :] = jnp.where(
                    s == vstar, jnp.where(rank <= want, s, BELOW_NEG), s)
                seen = seen + running[t][tt - 1:tt, :]
            return seen

        lax.fori_loop(0, nchunks, tie_body, jnp.zeros((1, qb), f32))

    def fold8(x, op):
        return op(x.reshape(sub // SUBLANES, SUBLANES, qb), axis=0)

    def cap_chunk(c, d, carry):
        off = pl.multiple_of(c * kc, kc)
        for t in range(kc // sub):
            r = t * sub
            cap = jnp.where(sc_ref[pl.ds(off + r, sub), :] >= vstar, jnp.inf, NEG)
            if d is not None:
                cap = causal(cap, d, r)
            sc_ref[pl.ds(off + r, sub), :] = cap
        return carry

    run_chunks(cap_chunk, 0)
    acc_ref[...] = jnp.zeros(acc_ref.shape, f32)

    @pl.when(i == 0)
    def _():
        kmax_ref[...] = jnp.zeros(kmax_ref.shape, f32)

    bounds = []
    for hd in range(HEADS):
        kk = k4_ref[hd, pl.ds(pl.multiple_of(i * qb, qb), qb), :].astype(f32)
        k2 = jnp.max(jnp.sum(kk * kk, axis=1, keepdims=True), axis=0, keepdims=True)
        kmax_ref[hd] = jnp.maximum(kmax_ref[hd], k2)
        qf = qt_ref[hd].astype(f32)
        q2 = jnp.sum(qf * qf, axis=0, keepdims=True)
        bounds.append(jnp.sqrt(q2 * kmax_ref[hd][0:1, 0:1]))
    worst = bounds[0]
    for hd in range(1, HEADS):
        worst = jnp.maximum(worst, bounds[hd])
    fast = jnp.max(worst) <= SHIFT_LIMIT

    row16 = lax.broadcasted_iota(i32, (BF16_ROWS, qb), 0)
    for hd in range(HEADS):
        qa_ref[hd, 0:HEAD_DIM, :] = qt_ref[hd]
        shift = jnp.where(fast, -bounds[hd], 0.0)
        qa_ref[hd, HEAD_DIM:HEAD_DIM + BF16_ROWS, :] = jnp.where(
            row16 == 0, shift, 0.0).astype(bf16)
        qa_ref[hd, HEAD_DIM + BF16_ROWS:LANES, :] = jnp.zeros(
            (LANES - HEAD_DIM - BF16_ROWS, qb), bf16)

    def capped_logits(hd, off, r):
        s = jnp.dot(k4_ref[hd, pl.ds(off + r, sub), :], qa_ref[hd], preferred_element_type=f32)
        return jnp.minimum(s, sc_ref[pl.ds(off + r, sub), :])

    @pl.when(fast)
    def _():
        def chunk(c, d, lparts):
            off = pl.multiple_of(c * kc, kc)
            lparts = list(lparts)
            for hd in range(HEADS):
                for t in range(kc // sub):
                    r = t * sub
                    p = jnp.exp2(capped_logits(hd, off, r))
                    lparts[hd] = lparts[hd] + fold8(p, jnp.sum)
                    p_ref[hd, r:r + sub, :] = p.astype(bf16)
            for hd in range(HEADS):
                acc_ref[hd] += jnp.dot(vt_ref[hd, :, pl.ds(off, kc)], p_ref[hd],
                                       preferred_element_type=f32)
            return tuple(lparts)

        lparts = run_chunks(chunk, tuple(jnp.zeros((SUBLANES, qb), f32) for _ in range(HEADS)))
        for hd in range(HEADS):
            l_ref[hd] = lparts[hd]

    @pl.when(jnp.logical_not(fast))
    def _():
        def attend_chunk(c, d, carry):
            ms, ls = carry
            off = pl.multiple_of(c * kc, kc)
            new_ms, new_ls = list(ms), list(ls)

            def logits_stage(hd):
                mpart = None
                for t in range(kc // sub):
                    r = t * sub
                    s = capped_logits(hd, off, r)
                    s_ref[hd, r:r + sub, :] = s
                    m8 = fold8(s, jnp.max)
                    mpart = m8 if mpart is None else jnp.maximum(mpart, m8)
                new_ms[hd] = jnp.maximum(ms[hd], jnp.max(mpart, axis=0, keepdims=True))

            def value_stage(hd):
                lpart = None
                for t in range(kc // sub):
                    r = t * sub
                    p = jnp.exp2(s_ref[hd, r:r + sub, :] - new_ms[hd])
                    l8 = fold8(p, jnp.sum)
                    lpart = l8 if lpart is None else lpart + l8
                    p_ref[hd, r:r + sub, :] = p.astype(bf16)
                alpha = jnp.exp2(ms[hd] - new_ms[hd])
                new_ls[hd] = alpha * ls[hd] + jnp.sum(lpart, axis=0, keepdims=True)
                pv = jnp.dot(vt_ref[hd, :, pl.ds(off, kc)], p_ref[hd],
                             preferred_element_type=f32)
                acc_ref[hd] = acc_ref[hd] * alpha + pv

            logits_stage(0)
            for hd in range(HEADS):
                if hd + 1 < HEADS:
                    logits_stage(hd + 1)
                value_stage(hd)
            return tuple(new_ms), tuple(new_ls)

        init = (tuple(jnp.full((1, qb), NEG, f32) for _ in range(HEADS)),
                tuple(jnp.zeros((1, qb), f32) for _ in range(HEADS)))
        _, ls = run_chunks(attend_chunk, init)
        row8 = lax.broadcasted_iota(i32, (SUBLANES, qb), 0)
        for hd in range(HEADS):
            l_ref[hd] = jnp.where(row8 == 0, ls[hd], 0.0)

    out_t = jnp.concatenate(
        [acc_ref[hd] / jnp.sum(l_ref[hd], axis=0, keepdims=True) for hd in range(HEADS)],
        axis=0)
    o_ref[...] = out_t.T.astype(o_ref.dtype)


def _dsa_call(qit, wit, qt, ki, k4, vt, bsz, seq):
    n = bsz * seq
    qb = DSA_QB
    nq = seq // qb
    qcol4 = lambda b, i: (0, 0, b * nq + i)
    return pl.pallas_call(
        functools.partial(_dsa_kernel, seq=seq),
        grid=(bsz, nq),
        in_specs=[
            pl.BlockSpec((IDX_HEADS, IDX_DIM, qb), qcol4),
            pl.BlockSpec((SUBLANES, qb), lambda b, i: (0, b * nq + i)),
            pl.BlockSpec((HEADS, HEAD_DIM, qb), qcol4),
            pl.BlockSpec((seq, IDX_DIM), lambda b, i: (b, 0)),
            pl.BlockSpec((HEADS, seq, LANES), lambda b, i: (0, b, 0)),
            pl.BlockSpec((HEADS, HEAD_DIM, seq), lambda b, i: (0, 0, b)),
        ],
        out_specs=pl.BlockSpec((qb, GROUP), lambda b, i: (b * nq + i, 0)),
        out_shape=jax.ShapeDtypeStruct((n, GROUP), jnp.bfloat16),
        scratch_shapes=[
            pltpu.VMEM((seq, qb), jnp.float32),
            pltpu.VMEM((seq, qb), jnp.bfloat16),
            pltpu.VMEM((HEADS, HEAD_DIM, qb), jnp.float32),
            pltpu.VMEM((HEADS, DSA_KC, qb), jnp.float32),
            pltpu.VMEM((HEADS, DSA_KC, qb), jnp.bfloat16),
            pltpu.VMEM((HEADS, LANES, qb), jnp.bfloat16),
            pltpu.VMEM((HEADS, SUBLANES, qb), jnp.float32),
            pltpu.VMEM((HEADS, SUBLANES, LANES), jnp.float32),
        ],
        compiler_params=pltpu.CompilerParams(
            dimension_semantics=("arbitrary", "arbitrary"),
            vmem_limit_bytes=VMEM_LIMIT),
        name="dsa",
    )(qit, wit, qt, ki, k4, vt)


def _out_ffn_kernel(x_ref, yab_ref, yc_ref, yd_ref, wout_ref, gffn_ref, wg_ref, wu_ref,
                    wd_ref, gfin_ref, o_ref, *, final):
    f32, bf16 = jnp.float32, jnp.bfloat16
    y = jnp.concatenate([yab_ref[...], yc_ref[...], yd_ref[...]], axis=1)
    x1 = x_ref[...] + jnp.dot(y, wout_ref[...], preferred_element_type=f32)
    hf = _rmsnorm(x1, gffn_ref[...]).astype(bf16)
    gate = jnp.dot(hf, wg_ref[...], preferred_element_type=f32)
    up = jnp.dot(hf, wu_ref[...], preferred_element_type=f32)
    act = (gate * jax.nn.sigmoid(gate) * up).astype(bf16)
    out = x1 + jnp.dot(act, wd_ref[...], preferred_element_type=f32)
    if final:
        out = _rmsnorm(out, gfin_ref[...])
    o_ref[...] = out


def _out_ffn_call(x2d, yab, yc, yd, wout, gffn, wg, wu, wd, gfin, layer, final):
    n = x2d.shape[0]
    tm = FFN_TM
    row = lambda i: (i, 0)
    const = lambda i: (0, 0)
    resident = lambda shape: pl.BlockSpec((None,) + shape, lambda i: (layer, 0, 0),
                                          pipeline_mode=pl.Buffered(1))
    return pl.pallas_call(
        functools.partial(_out_ffn_kernel, final=final),
        grid=(n // tm,),
        in_specs=[
            pl.BlockSpec((tm, D_MODEL), row),
            pl.BlockSpec((tm, 2 * GROUP), row),
            pl.BlockSpec((tm, GROUP), row),
            pl.BlockSpec((tm, GROUP), row),
            resident((D_MODEL, D_MODEL)),
            pl.BlockSpec((1, D_MODEL), const),
            resident((D_MODEL, FFN_HIDDEN)),
            resident((D_MODEL, FFN_HIDDEN)),
            resident((FFN_HIDDEN, D_MODEL)),
            pl.BlockSpec((1, D_MODEL), const),
        ],
        out_specs=pl.BlockSpec((tm, D_MODEL), row),
        out_shape=jax.ShapeDtypeStruct((n, D_MODEL), jnp.float32),
        compiler_params=pltpu.CompilerParams(
            dimension_semantics=("arbitrary",),
            vmem_limit_bytes=VMEM_LIMIT),
        name="out_ffn",
    )(x2d, yab, yc, yd, wout, gffn, wg, wu, wd, gfin)


def _pack_w_in(w):
    g = GROUP
    cuts = [0, 3 * g, 5 * g, 6 * g, 7 * g, 8 * g, 9 * g, 9 * g + IDX_DIM,
            9 * g + IDX_DIM + IDX_HEADS, 11 * g + IDX_DIM + IDX_HEADS]
    a, b, q, k, v, qi, ki, wi, d = [w[:, cuts[t]:cuts[t + 1]] for t in range(9)]
    z = lambda cols: jnp.zeros((w.shape[0], cols), w.dtype)
    rows = jnp.concatenate([a, b, k, ki, z(LANES - IDX_DIM), d], axis=1)
    trans = jnp.concatenate([q, v, qi, wi, z(WI_ROWS - IDX_HEADS)], axis=1).T
    return rows.astype(jnp.bfloat16), trans.astype(jnp.bfloat16)


def _rope_tables(seq):
    inv_freq = ROPE_THETA ** (-jnp.arange(0, HEAD_DIM, 2, dtype=jnp.float32) / HEAD_DIM)
    ang = jnp.arange(seq, dtype=jnp.float32)[:, None] * inv_freq[None, :]
    c, s = jnp.cos(ang), jnp.sin(ang)
    reps = LANES // HEAD_DIM
    cos = jnp.tile(jnp.concatenate([c, c], axis=1), (1, reps))
    sin = jnp.tile(jnp.concatenate([-s, s], axis=1), (1, reps))
    return cos, sin, c.T, s.T


def kernel(x, g_mix, w_in, w_conv_a, gmlp_ln_g, gmlp_ln_b, w_s, b_s, w_conf, b_conf,
           conf_ln_g, conf_ln_b, w_out, g_ffn, w_gate, w_up, w_down, g_final):
    bsz, seq, d = x.shape
    depth = w_in.shape[0]
    bf16 = jnp.bfloat16
    tables = _rope_tables(seq)
    w_out_b, w_gate_b, w_up_b, w_down_b = (
        w.astype(bf16) for w in (w_out, w_gate, w_up, w_down))
    xs = x
    for l in range(depth):
        w_rows, w_t = _pack_w_in(w_in[l])
        yab, yd, k4, ki, qt, vt, qit, wit = _mix_call(
            xs.reshape(bsz, seq, d), g_mix[l][None, :], w_rows, w_t, tables,
            w_conv_a[l], gmlp_ln_g[l][None, :], gmlp_ln_b[l][None, :], w_s[l],
            jnp.repeat(b_s[l].T, HEAD_DIM, axis=1), w_conf[l], b_conf[l][None, :],
            conf_ln_g[l][None, :], conf_ln_b[l][None, :])
        yc = _dsa_call(qit, wit, qt, ki, k4, vt, bsz, seq)
        xs = _out_ffn_call(
            xs.reshape(bsz * seq, d), yab, yc, yd, w_out_b, g_ffn[l][None, :],
            w_gate_b, w_up_b, w_down_b, g_final[None, :], layer=l, final=(l == depth - 1))
    return xs.reshape(bsz, seq, d)
```

```python
import functools
import math

import jax
import jax.numpy as jnp
import numpy as np
from jax import lax
from jax.experimental import pallas as pl
from jax.experimental.pallas import tpu as pltpu

D_MODEL = 1024
GROUP = 256
HEADS = 4
HEAD_DIM = 64
CHUNK = 128
IDX_HEADS = 4
IDX_DIM = 64
TOPK = 256
SHORT_W = 3
CONF_W = 31
FFN_HIDDEN = 2816
ROPE_THETA = 10000.0
NORM_EPS = 1e-6
LN_EPS = 1e-5
NEG = -1e30
NEG_BF16 = float(np.asarray(NEG, dtype=jnp.bfloat16).astype(np.float32))

LANES = 128
SUBLANES = 8
A_HALO = 8
D_HALO = 32

COL_A = 0
COL_B = COL_A + 3 * GROUP
COL_K = COL_B + 2 * GROUP
COL_KI = COL_K + GROUP
COL_D = COL_KI + LANES
ROW_COLS = COL_D + 2 * GROUP
TROW_Q = 0
TROW_V = TROW_Q + GROUP
TROW_QI = TROW_V + GROUP
TROW_WI = TROW_QI + IDX_HEADS * IDX_DIM
WI_ROWS = 16
TROWS = TROW_WI + WI_ROWS

MIX_TM = 1024
DSA_QB = 512
DSA_KC = 512
DSA_SUB = 64
CNT_ROWS = 16
CNTB_ROWS = 32
TIE_TILE = 128
BF16_ROWS = 16
SHIFT_LIMIT = 45.0
FFN_TM = 512
VMEM_LIMIT = 56 * 1024 * 1024

_NT = (((1,), (1,)), ((), ()))


def _rmsnorm(xf, g):
    ms = jnp.mean(xf * xf, axis=-1, keepdims=True)
    return xf * lax.rsqrt(ms + NORM_EPS) * g


def _layernorm(xf, g, b):
    mu = jnp.mean(xf, axis=-1, keepdims=True)
    xc = xf - mu
    var = jnp.mean(xc * xc, axis=-1, keepdims=True)
    return xc * lax.rsqrt(var + LN_EPS) * g + b


def _rope128(x, cos, sin_signed):
    half = HEAD_DIM // 2
    lane = lax.broadcasted_iota(jnp.int32, x.shape, 1) % HEAD_DIM
    up = pltpu.roll(x, LANES - half, axis=1)
    dn = pltpu.roll(x, half, axis=1)
    return x * cos + jnp.where(lane < half, up, dn) * sin_signed


def _rope_t(x, cos_t, sin_t):
    half = HEAD_DIM // 2
    x1, x2 = x[0:half, :], x[half:HEAD_DIM, :]
    return x1 * cos_t - x2 * sin_t, x2 * cos_t + x1 * sin_t


def _mix_kernel(x_ref, g_ref, w_ref, wt_ref, cos_ref, sin_ref, cost_ref, sint_ref,
                wca_ref, lngb_ref, lnbb_ref, ws_ref, bsb_ref, wcf_ref, bcf_ref,
                lngd_ref, lnbd_ref,
                yab_ref, yd_ref, k4_ref, ki_ref, qt_ref, vt_ref, qit_ref, wit_ref,
                abuf, dbuf, sbuf):
    tm = x_ref.shape[0]
    f32, bf16 = jnp.float32, jnp.bfloat16
    half = HEAD_DIM // 2

    @pl.when(pl.program_id(1) == 0)
    def _():
        abuf[0:A_HALO, :] = jnp.zeros((A_HALO, GROUP), f32)
        dbuf[0:D_HALO, :] = jnp.zeros((D_HALO, GROUP), f32)

    h = _rmsnorm(x_ref[...], g_ref[...]).astype(bf16)

    def proj(lo, hi):
        return jnp.dot(h, w_ref[:, lo:hi], preferred_element_type=f32)

    def proj_t(lo, hi):
        return lax.dot_general(wt_ref[lo:hi, :], h, _NT, preferred_element_type=f32)

    zd = proj(COL_D, ROW_COLS)
    dbuf[D_HALO:D_HALO + tm, :] = zd[:, 0:GROUP] * jax.nn.sigmoid(zd[:, GROUP:2 * GROUP])
    acc = jnp.broadcast_to(bcf_ref[...], (tm, GROUP))
    first = D_HALO - (CONF_W - 1)
    for r in range(SUBLANES):
        taps = [t for t in range(CONF_W) if (first + t) % SUBLANES == r]
        span = (first + taps[-1]) - (first + taps[0]) + tm
        sbuf[0:span, :] = dbuf[pl.ds(first + taps[0], span), :]
        for t in taps:
            lo = t - taps[0]
            acc = acc + sbuf[lo:lo + tm, :] * wcf_ref[t:t + 1, :]
    y = _layernorm(acc, lngd_ref[...], lnbd_ref[...])
    yd_ref[...] = (y * jax.nn.sigmoid(y)).astype(bf16)
    dbuf[0:D_HALO, :] = dbuf[tm:tm + D_HALO, :]

    za = proj(COL_A, COL_B)
    abuf[A_HALO:A_HALO + tm, :] = za[:, GROUP:2 * GROUP] * za[:, 2 * GROUP:3 * GROUP]
    conv = abuf[pl.ds(A_HALO - 2, tm), :] * wca_ref[0:1, :]
    conv = conv + abuf[pl.ds(A_HALO - 1, tm), :] * wca_ref[1:2, :]
    conv = conv + abuf[pl.ds(A_HALO, tm), :] * wca_ref[2:3, :]
    yab_ref[:, 0:GROUP] = (za[:, 0:GROUP] * conv).astype(bf16)
    abuf[0:A_HALO, :] = abuf[tm:tm + A_HALO, :]

    zb = proj(COL_B, COL_K)
    vn = _layernorm(zb[:, GROUP:2 * GROUP], lngb_ref[...], lnbb_ref[...]).astype(bf16)
    r_i = lax.broadcasted_iota(jnp.int32, (CHUNK, CHUNK), 0)
    c_i = lax.broadcasted_iota(jnp.int32, (CHUNK, CHUNK), 1)
    wcat = jnp.concatenate(
        [jnp.where(r_i >= c_i, ws_ref[hd], 0.0).astype(bf16) for hd in range(HEADS)], axis=1)
    lane_head = lax.broadcasted_iota(jnp.int32, (CHUNK, GROUP), 1) // HEAD_DIM
    zero_b = jnp.zeros((), bf16)
    for c in range(tm // CHUNK):
        rows = slice(c * CHUNK, (c + 1) * CHUNK)
        vc = vn[rows, :]
        stacked = jnp.concatenate(
            [jnp.where(lane_head == hd, vc, zero_b) for hd in range(HEADS)], axis=0)
        mixed = bsb_ref[...] + jnp.dot(wcat, stacked, preferred_element_type=f32)
        yab_ref[rows, GROUP:2 * GROUP] = (zb[rows, 0:GROUP] * mixed).astype(bf16)

    cos = cos_ref[...]
    sin = sin_ref[...]
    zk = proj(COL_K, COL_D)
    lane = lax.broadcasted_iota(jnp.int32, (tm, LANES), 1)
    bias_col = jnp.where(lane == HEAD_DIM, 1.0, 0.0)
    for pair in range(2):
        kp = _rope128(zk[:, pair * LANES:(pair + 1) * LANES], cos, sin)
        for sub in range(2):
            kh = kp if sub == 0 else pltpu.roll(kp, HEAD_DIM, axis=1)
            k4_ref[2 * pair + sub] = jnp.where(lane < HEAD_DIM, kh, bias_col).astype(bf16)
    kip = _rope128(zk[:, GROUP:GROUP + LANES], cos, sin)
    ki_ref[...] = kip[:, 0:IDX_DIM].astype(bf16)

    cos_t = cost_ref[...]
    sin_t = sint_ref[...]
    q_scale = HEAD_DIM ** -0.5 * math.log2(math.e)
    zq = proj_t(TROW_Q, TROW_V)
    zqi = proj_t(TROW_QI, TROW_WI)
    for hd in range(HEADS):
        rows = slice(hd * HEAD_DIM, (hd + 1) * HEAD_DIM)
        o1, o2 = _rope_t(zq[rows, :], cos_t, sin_t)
        qt_ref[hd, 0:half, :] = (o1 * q_scale).astype(bf16)
        qt_ref[hd, half:HEAD_DIM, :] = (o2 * q_scale).astype(bf16)
        o1, o2 = _rope_t(zqi[rows, :], cos_t, sin_t)
        qit_ref[hd, 0:half, :] = o1.astype(bf16)
        qit_ref[hd, half:HEAD_DIM, :] = o2.astype(bf16)
    zv = proj_t(TROW_V, TROW_QI)
    for hd in range(HEADS):
        vt_ref[hd] = zv[hd * HEAD_DIM:(hd + 1) * HEAD_DIM, :].astype(bf16)
    zw = proj_t(TROW_WI, TROWS)
    wit_ref[...] = zw[0:SUBLANES, :] * (IDX_HEADS ** -0.5 * IDX_DIM ** -0.5)


def _mix_call(x, g, w_rows, w_t, tables, wca, lngb, lnbb, ws, bsb, wcf, bcf, lngd, lnbd):
    bsz, seq, _ = x.shape
    n = bsz * seq
    tm = MIX_TM
    nt = seq // tm
    bf16 = jnp.bfloat16
    half = HEAD_DIM // 2
    cos, sin, cos_t, sin_t = tables
    row = lambda b, i: (b * nt + i, 0)
    row4 = lambda b, i: (0, b * nt + i, 0)
    col = lambda b, i: (0, b * nt + i)
    col4 = lambda b, i: (0, 0, b * nt + i)
    const2 = lambda b, i: (0, 0)
    const3 = lambda b, i: (0, 0, 0)
    in_specs = [
        pl.BlockSpec((None, tm, D_MODEL), lambda b, i: (b, i, 0)),
        pl.BlockSpec((1, D_MODEL), const2),
        pl.BlockSpec((D_MODEL, ROW_COLS), const2),
        pl.BlockSpec((TROWS, D_MODEL), const2),
        pl.BlockSpec((tm, LANES), lambda b, i: (i, 0)),
        pl.BlockSpec((tm, LANES), lambda b, i: (i, 0)),
        pl.BlockSpec((half, tm), lambda b, i: (0, i)),
        pl.BlockSpec((half, tm), lambda b, i: (0, i)),
        pl.BlockSpec((SHORT_W, GROUP), const2),
        pl.BlockSpec((1, GROUP), const2),
        pl.BlockSpec((1, GROUP), const2),
        pl.BlockSpec((HEADS, CHUNK, CHUNK), const3),
        pl.BlockSpec((CHUNK, GROUP), const2),
        pl.BlockSpec((CONF_W, GROUP), const2),
        pl.BlockSpec((1, GROUP), const2),
        pl.BlockSpec((1, GROUP), const2),
        pl.BlockSpec((1, GROUP), const2),
    ]
    out_shape = [
        jax.ShapeDtypeStruct((n, 2 * GROUP), bf16),
        jax.ShapeDtypeStruct((n, GROUP), bf16),
        jax.ShapeDtypeStruct((HEADS, n, LANES), bf16),
        jax.ShapeDtypeStruct((n, IDX_DIM), bf16),
        jax.ShapeDtypeStruct((HEADS, HEAD_DIM, n), bf16),
        jax.ShapeDtypeStruct((HEADS, HEAD_DIM, n), bf16),
        jax.ShapeDtypeStruct((IDX_HEADS, IDX_DIM, n), bf16),
        jax.ShapeDtypeStruct((SUBLANES, n), jnp.float32),
    ]
    out_specs = [
        pl.BlockSpec((tm, 2 * GROUP), row),
        pl.BlockSpec((tm, GROUP), row),
        pl.BlockSpec((HEADS, tm, LANES), row4),
        pl.BlockSpec((tm, IDX_DIM), row),
        pl.BlockSpec((HEADS, HEAD_DIM, tm), col4),
        pl.BlockSpec((HEADS, HEAD_DIM, tm), col4),
        pl.BlockSpec((IDX_HEADS, IDX_DIM, tm), col4),
        pl.BlockSpec((SUBLANES, tm), col),
    ]
    return pl.pallas_call(
        _mix_kernel,
        grid=(bsz, nt),
        in_specs=in_specs,
        out_specs=out_specs,
        out_shape=out_shape,
        scratch_shapes=[pltpu.VMEM((tm + A_HALO, GROUP), jnp.float32),
                        pltpu.VMEM((tm + D_HALO, GROUP), jnp.float32),
                        pltpu.VMEM((tm + D_HALO, GROUP), jnp.float32)],
        compiler_params=pltpu.CompilerParams(
            dimension_semantics=("arbitrary", "arbitrary"),
            vmem_limit_bytes=VMEM_LIMIT),
        name="mix",
    )(x, g, w_rows, w_t, cos, sin, cos_t, sin_t, wca, lngb, lnbb, ws, bsb, wcf, bcf,
      lngd, lnbd)


def _threshold_bits(trial):
    bits = jnp.where(trial < 0, trial & jnp.int32(0x7FFFFFFF), ~trial)
    return pltpu.bitcast(bits, jnp.float32)


def _dsa_kernel(qit_ref, wit_ref, qt_ref, ki_ref, k4_ref, vt_ref, o_ref,
                sc_ref, scb_ref, acc_ref, s_ref, p_ref, qa_ref, l_ref, kmax_ref, *, seq):
    f32, bf16, i32 = jnp.float32, jnp.bfloat16, jnp.int32
    qb = wit_ref.shape[1]
    kc, sub = DSA_KC, DSA_SUB
    ndiag = qb // kc
    i = pl.program_id(1)
    nfull = i * ndiag
    nchunks = nfull + ndiag
    n_outside = seq - nchunks * kc
    key_i = lax.broadcasted_iota(i32, (sub, qb), 0)
    qry_i = lax.broadcasted_iota(i32, (sub, qb), 1)

    def causal(x, d, r):
        if x.shape == key_i.shape:
            k_i, q_i = key_i, qry_i
        else:
            k_i = lax.broadcasted_iota(i32, x.shape, 0)
            q_i = lax.broadcasted_iota(i32, x.shape, 1)
        return jnp.where(k_i + (d * kc + r) <= q_i, x, NEG)

    def run_chunks(chunk_fn, carry):
        carry = lax.fori_loop(0, nfull, lambda c, cr: chunk_fn(c, None, cr), carry)
        for d in range(ndiag):
            carry = chunk_fn(nfull + d, d, carry)
        return carry

    def score_chunk(c, d, carry):
        off = pl.multiple_of(c * kc, kc)
        for t in range(kc // sub):
            r0 = off + t * sub
            kis = ki_ref[pl.ds(r0, sub), :]
            sc = None
            for j in range(IDX_HEADS):
                rel = jnp.dot(kis, qit_ref[j], preferred_element_type=f32)
                term = jnp.maximum(rel, 0.0) * wit_ref[j:j + 1, :]
                sc = term if sc is None else sc + term
            if d is not None:
                sc = causal(sc, d, t * sub)
            sc_ref[pl.ds(r0, sub), :] = sc
            scb_ref[pl.ds(r0, sub), :] = sc.astype(bf16)
        return carry

    run_chunks(score_chunk, 0)

    assert ndiag == 1
    half = qb // 2
    n_masked = n_outside + jnp.where(lax.broadcasted_iota(i32, (1, qb), 1) < half, kc - half, 0)

    def count_pass(ref, rows, zero, indicator):
        def update(acc, r0, lo):
            ind = indicator(ref[pl.ds(r0, rows), lo:qb], lo)
            if lo == 0:
                return acc + ind
            return jnp.concatenate([acc[:, :lo], acc[:, lo:] + ind], axis=1)

        def body(c, acc):
            off = pl.multiple_of(c * kc, kc)
            for t in range(kc // rows):
                acc = update(acc, off + t * rows, 0)
            return acc

        acc = lax.fori_loop(0, nfull, body, zero)
        doff = pl.multiple_of(nfull * kc, kc)
        for t in range(kc // rows):
            acc = update(acc, doff + t * rows, 0 if t * rows < half else half)
        return acc

    def count(cmp, thr):
        acc = count_pass(sc_ref, CNT_ROWS, jnp.zeros((CNT_ROWS, qb), i32),
                         lambda x, lo: jnp.where(cmp(x, thr[:, lo:qb]), 1, 0))
        return jnp.sum(acc, axis=0, keepdims=True)

    def count_ge(thr):
        return count(lambda x, t: x >= t, thr) + jnp.where(thr <= NEG, n_masked, 0)

    neg_b = NEG_BF16
    one_b, zero_b = jnp.ones((), bf16), jnp.zeros((), bf16)

    def count_ge_b(thr):
        thr_b = thr.astype(bf16)
        acc = count_pass(scb_ref, CNTB_ROWS, jnp.zeros((CNTB_ROWS, qb), bf16),
                         lambda x, lo: jnp.where(x >= thr_b[:, lo:qb], one_b, zero_b))
        inside = jnp.sum(acc.astype(f32), axis=0, keepdims=True).astype(i32)
        return inside + jnp.where(thr_b.astype(f32) <= neg_b, n_masked, 0)

    def coarse_step(step, prefix):
        trial = prefix | jnp.left_shift(jnp.int32(1), 31 - step)
        cnt = count_ge_b(_threshold_bits(trial))
        return jnp.where(cnt >= TOPK, trial, prefix)

    coarse = lax.fori_loop(0, 16, coarse_step, jnp.zeros((1, qb), i32))

    start = coarse - jnp.int32(1 << 16)

    def fine_step(step, state):
        prefix, c_ge = state
        trial = prefix + jnp.left_shift(jnp.int32(1), 16 - step)
        cnt = count_ge(_threshold_bits(trial))
        accept = cnt >= TOPK
        return jnp.where(accept, trial, prefix), jnp.where(accept, cnt, c_ge)

    prefix, c_ge = lax.fori_loop(0, 17, fine_step, (start, jnp.full((1, qb), seq, i32)))
    vstar = _threshold_bits(prefix)

    has_ties = jnp.max(c_ge) > TOPK

    @pl.when(has_ties)
    def _():
        c_gt = count(lambda x, t: x > t, vstar) + jnp.where(vstar < NEG, n_masked, 0)
        want = (TOPK - c_gt).astype(f32)
        tt = TIE_TILE
        tri = jnp.where(lax.broadcasted_iota(i32, (tt, tt), 0)
                        >= lax.broadcasted_iota(i32, (tt, tt), 1), 1.0, 0.0).astype(bf16)

        def tie_chunk(c, d, seen):
            off = pl.multiple_of(c * kc, kc)
            running = []
            for t in range(kc // tt):
                s = sc_ref[pl.ds(off + t * tt, tt), :]
                tie = jnp.where(s == vstar, 1.0, 0.0).astype(bf16)
                running.append(jnp.dot(tri, tie, preferred_element_type=f32))
            for t in range(kc // tt):
                s = sc_ref[pl.ds(off + t * tt, tt), :]
                rank = seen + running[t]
                admitted = jnp.where(rank <= want, jnp.inf, NEG)
                cap = jnp.where(s > vstar, jnp.inf, jnp.where(s == vstar, admitted, NEG))
                if d is not None:
                    cap = causal(cap, d, t * tt)
                sc_ref[pl.ds(off + t * tt, tt), :] = cap
                seen = seen + running[t][tt - 1:tt, :]
            return seen

        run_chunks(tie_chunk, jnp.zeros((1, qb), f32))

    def fold8(x, op):
        return op(x.reshape(sub // SUBLANES, SUBLANES, qb), axis=0)

    @pl.when(jnp.logical_not(has_ties))
    def _():
        def cap_chunk(c, d, carry):
            off = pl.multiple_of(c * kc, kc)
            for t in range(kc // sub):
                r = t * sub
                cap = jnp.where(sc_ref[pl.ds(off + r, sub), :] >= vstar, jnp.inf, NEG)
                if d is not None:
                    cap = causal(cap, d, r)
                sc_ref[pl.ds(off + r, sub), :] = cap
            return carry

        run_chunks(cap_chunk, 0)

    acc_ref[...] = jnp.zeros(acc_ref.shape, f32)

    @pl.when(i == 0)
    def _():
        kmax_ref[...] = jnp.zeros(kmax_ref.shape, f32)

    bounds = []
    for hd in range(HEADS):
        kk = k4_ref[hd, pl.ds(pl.multiple_of(i * qb, qb), qb), :].astype(f32)
        k2 = jnp.max(jnp.sum(kk * kk, axis=1, keepdims=True), axis=0, keepdims=True)
        kmax_ref[hd] = jnp.maximum(kmax_ref[hd], k2)
        qf = qt_ref[hd].astype(f32)
        q2 = jnp.sum(qf * qf, axis=0, keepdims=True)
        bounds.append(jnp.sqrt(q2 * kmax_ref[hd][0:1, 0:1]))
    worst = bounds[0]
    for hd in range(1, HEADS):
        worst = jnp.maximum(worst, bounds[hd])
    fast = jnp.max(worst) <= SHIFT_LIMIT

    row16 = lax.broadcasted_iota(i32, (BF16_ROWS, qb), 0)
    for hd in range(HEADS):
        qa_ref[hd, 0:HEAD_DIM, :] = qt_ref[hd]
        shift = jnp.where(fast, -bounds[hd], 0.0)
        qa_ref[hd, HEAD_DIM:HEAD_DIM + BF16_ROWS, :] = jnp.where(
            row16 == 0, shift, 0.0).astype(bf16)
        qa_ref[hd, HEAD_DIM + BF16_ROWS:LANES, :] = jnp.zeros(
            (LANES - HEAD_DIM - BF16_ROWS, qb), bf16)

    def capped_logits(hd, off, r):
        s = jnp.dot(k4_ref[hd, pl.ds(off + r, sub), :], qa_ref[hd], preferred_element_type=f32)
        return jnp.minimum(s, sc_ref[pl.ds(off + r, sub), :])

    @pl.when(fast)
    def _():
        def chunk(c, d, lparts):
            off = pl.multiple_of(c * kc, kc)
            lparts = list(lparts)
            for hd in range(HEADS):
                for t in range(kc // sub):
                    r = t * sub
                    p = jnp.exp2(capped_logits(hd, off, r))
                    lparts[hd] = lparts[hd] + fold8(p, jnp.sum)
                    p_ref[hd, r:r + sub, :] = p.astype(bf16)
            for hd in range(HEADS):
                acc_ref[hd] += jnp.dot(vt_ref[hd, :, pl.ds(off, kc)], p_ref[hd],
                                       preferred_element_type=f32)
            return tuple(lparts)

        lparts = run_chunks(chunk, tuple(jnp.zeros((SUBLANES, qb), f32) for _ in range(HEADS)))
        for hd in range(HEADS):
            l_ref[hd] = lparts[hd]

    @pl.when(jnp.logical_not(fast))
    def _():
        def attend_chunk(c, d, carry):
            ms, ls = carry
            off = pl.multiple_of(c * kc, kc)
            new_ms, new_ls = list(ms), list(ls)

            def logits_stage(hd):
                mpart = None
                for t in range(kc // sub):
                    r = t * sub
                    s = capped_logits(hd, off, r)
                    s_ref[hd, r:r + sub, :] = s
                    m8 = fold8(s, jnp.max)
                    mpart = m8 if mpart is None else jnp.maximum(mpart, m8)
                new_ms[hd] = jnp.maximum(ms[hd], jnp.max(mpart, axis=0, keepdims=True))

            def value_stage(hd):
                lpart = None
                for t in range(kc // sub):
                    r = t * sub
                    p = jnp.exp2(s_ref[hd, r:r + sub, :] - new_ms[hd])
                    l8 = fold8(p, jnp.sum)
                    lpart = l8 if lpart is None else lpart + l8
                    p_ref[hd, r:r + sub, :] = p.astype(bf16)
                alpha = jnp.exp2(ms[hd] - new_ms[hd])
                new_ls[hd] = alpha * ls[hd] + jnp.sum(lpart, axis=0, keepdims=True)
                pv = jnp.dot(vt_ref[hd, :, pl.ds(off, kc)], p_ref[hd],
                             preferred_element_type=f32)
                acc_ref[hd] = acc_ref[hd] * alpha + pv

            logits_stage(0)
            for hd in range(HEADS):
                if hd + 1 < HEADS:
                    logits_stage(hd + 1)
                value_stage(hd)
            return tuple(new_ms), tuple(new_ls)

        init = (tuple(jnp.full((1, qb), NEG, f32) for _ in range(HEADS)),
                tuple(jnp.zeros((1, qb), f32) for _ in range(HEADS)))
        _, ls = run_chunks(attend_chunk, init)
        row8 = lax.broadcasted_iota(i32, (SUBLANES, qb), 0)
        for hd in range(HEADS):
            l_ref[hd] = jnp.where(row8 == 0, ls[hd], 0.0)

    out_t = jnp.concatenate(
        [acc_ref[hd] / jnp.sum(l_ref[hd], axis=0, keepdims=True) for hd in range(HEADS)],
        axis=0)
    o_ref[...] = out_t.T.astype(o_ref.dtype)


def _dsa_call(qit, wit, qt, ki, k4, vt, bsz, seq):
    n = bsz * seq
    qb = DSA_QB
    nq = seq // qb
    qcol4 = lambda b, i: (0, 0, b * nq + i)
    return pl.pallas_call(
        functools.partial(_dsa_kernel, seq=seq),
        grid=(bsz, nq),
        in_specs=[
            pl.BlockSpec((IDX_HEADS, IDX_DIM, qb), qcol4),
            pl.BlockSpec((SUBLANES, qb), lambda b, i: (0, b * nq + i)),
            pl.BlockSpec((HEADS, HEAD_DIM, qb), qcol4),
            pl.BlockSpec((seq, IDX_DIM), lambda b, i: (b, 0)),
            pl.BlockSpec((HEADS, seq, LANES), lambda b, i: (0, b, 0)),
            pl.BlockSpec((HEADS, HEAD_DIM, seq), lambda b, i: (0, 0, b)),
        ],
        out_specs=pl.BlockSpec((qb, GROUP), lambda b, i: (b * nq + i, 0)),
        out_shape=jax.ShapeDtypeStruct((n, GROUP), jnp.bfloat16),
        scratch_shapes=[
            pltpu.VMEM((seq, qb), jnp.float32),
            pltpu.VMEM((seq, qb), jnp.bfloat16),
            pltpu.VMEM((HEADS, HEAD_DIM, qb), jnp.float32),
            pltpu.VMEM((HEADS, DSA_KC, qb), jnp.float32),
            pltpu.VMEM((HEADS, DSA_KC, qb), jnp.bfloat16),
            pltpu.VMEM((HEADS, LANES, qb), jnp.bfloat16),
            pltpu.VMEM((HEADS, SUBLANES, qb), jnp.float32),
            pltpu.VMEM((HEADS, SUBLANES, LANES), jnp.float32),
        ],
        compiler_params=pltpu.CompilerParams(
            dimension_semantics=("arbitrary", "arbitrary"),
            vmem_limit_bytes=VMEM_LIMIT),
        name="dsa",
    )(qit, wit, qt, ki, k4, vt)


def _out_ffn_kernel(x_ref, yab_ref, yc_ref, yd_ref, wout_ref, gffn_ref, wg_ref, wu_ref,
                    wd_ref, gfin_ref, o_ref, *, final):
    f32, bf16 = jnp.float32, jnp.bfloat16
    y = jnp.concatenate([yab_ref[...], yc_ref[...], yd_ref[...]], axis=1)
    x1 = x_ref[...] + jnp.dot(y, wout_ref[...], preferred_element_type=f32)
    hf = _rmsnorm(x1, gffn_ref[...]).astype(bf16)
    gate = jnp.dot(hf, wg_ref[...], preferred_element_type=f32)
    up = jnp.dot(hf, wu_ref[...], preferred_element_type=f32)
    act = (gate * jax.nn.sigmoid(gate) * up).astype(bf16)
    out = x1 + jnp.dot(act, wd_ref[...], preferred_element_type=f32)
    if final:
        out = _rmsnorm(out, gfin_ref[...])
    o_ref[...] = out


def _out_ffn_call(x2d, yab, yc, yd, wout, gffn, wg, wu, wd, gfin, layer, final):
    n = x2d.shape[0]
    tm = FFN_TM
    row = lambda i: (i, 0)
    const = lambda i: (0, 0)
    resident = lambda shape: pl.BlockSpec((None,) + shape, lambda i: (layer, 0, 0),
                                          pipeline_mode=pl.Buffered(1))
    return pl.pallas_call(
        functools.partial(_out_ffn_kernel, final=final),
        grid=(n // tm,),
        in_specs=[
            pl.BlockSpec((tm, D_MODEL), row),
            pl.BlockSpec((tm, 2 * GROUP), row),
            pl.BlockSpec((tm, GROUP), row),
            pl.BlockSpec((tm, GROUP), row),
            resident((D_MODEL, D_MODEL)),
            pl.BlockSpec((1, D_MODEL), const),
            resident((D_MODEL, FFN_HIDDEN)),
            resident((D_MODEL, FFN_HIDDEN)),
            resident((FFN_HIDDEN, D_MODEL)),
            pl.BlockSpec((1, D_MODEL), const),
        ],
        out_specs=pl.BlockSpec((tm, D_MODEL), row),
        out_shape=jax.ShapeDtypeStruct((n, D_MODEL), jnp.float32),
        compiler_params=pltpu.CompilerParams(
            dimension_semantics=("arbitrary",),
            vmem_limit_bytes=VMEM_LIMIT),
        name="out_ffn",
    )(x2d, yab, yc, yd, wout, gffn, wg, wu, wd, gfin)


def _pack_w_in(w):
    g = GROUP
    cuts = [0, 3 * g, 5 * g, 6 * g, 7 * g, 8 * g, 9 * g, 9 * g + IDX_DIM,
            9 * g + IDX_DIM + IDX_HEADS, 11 * g + IDX_DIM + IDX_HEADS]
    a, b, q, k, v, qi, ki, wi, d = [w[:, cuts[t]:cuts[t + 1]] for t in range(9)]
    z = lambda cols: jnp.zeros((w.shape[0], cols), w.dtype)
    rows = jnp.concatenate([a, b, k, ki, z(LANES - IDX_DIM), d], axis=1)
    trans = jnp.concatenate([q, v, qi, wi, z(WI_ROWS - IDX_HEADS)], axis=1).T
    return rows.astype(jnp.bfloat16), trans.astype(jnp.bfloat16)


def _rope_tables(seq):
    inv_freq = ROPE_THETA ** (-jnp.arange(0, HEAD_DIM, 2, dtype=jnp.float32) / HEAD_DIM)
    ang = jnp.arange(seq, dtype=jnp.float32)[:, None] * inv_freq[None, :]
    c, s = jnp.cos(ang), jnp.sin(ang)
    reps = LANES // HEAD_DIM
    cos = jnp.tile(jnp.concatenate([c, c], axis=1), (1, reps))
    sin = jnp.tile(jnp.concatenate([-s, s], axis=1), (1, reps))
    return cos, sin, c.T, s.T


def kernel(x, g_mix, w_in, w_conv_a, gmlp_ln_g, gmlp_ln_b, w_s, b_s, w_conf, b_conf,
           conf_ln_g, conf_ln_b, w_out, g_ffn, w_gate, w_up, w_down, g_final):
    bsz, seq, d = x.shape
    depth = w_in.shape[0]
    bf16 = jnp.bfloat16
    tables = _rope_tables(seq)
    w_out_b, w_gate_b, w_up_b, w_down_b = (
        w.astype(bf16) for w in (w_out, w_gate, w_up, w_down))
    xs = x
    for l in range(depth):
        w_rows, w_t = _pack_w_in(w_in[l])
        yab, yd, k4, ki, qt, vt, qit, wit = _mix_call(
            xs.reshape(bsz, seq, d), g_mix[l][None, :], w_rows, w_t, tables,
            w_conv_a[l], gmlp_ln_g[l][None, :], gmlp_ln_b[l][None, :], w_s[l],
            jnp.repeat(b_s[l].T, HEAD_DIM, axis=1), w_conf[l], b_conf[l][None, :],
            conf_ln_g[l][None, :], conf_ln_b[l][None, :])
        yc = _dsa_call(qit, wit, qt, ki, k4, vt, bsz, seq)
        xs = _out_ffn_call(
            xs.reshape(bsz * seq, d), yab, yc, yd, w_out_b, g_ffn[l][None, :],
            w_gate_b, w_up_b, w_down_b, g_final[None, :], layer=l, final=(l == depth - 1))
    return xs.reshape(bsz, seq, d)
```

```python
import functools
import math

import jax
import jax.numpy as jnp
import numpy as np
from jax import lax
from jax.experimental import pallas as pl
from jax.experimental.pallas import tpu as pltpu

D_MODEL = 1024
GROUP = 256
HEADS = 4
HEAD_DIM = 64
CHUNK = 128
IDX_HEADS = 4
IDX_DIM = 64
TOPK = 256
SHORT_W = 3
CONF_W = 31
FFN_HIDDEN = 2816
ROPE_THETA = 10000.0
NORM_EPS = 1e-6
LN_EPS = 1e-5
NEG = -1e30
NEG_BF16 = float(np.asarray(NEG, dtype=jnp.bfloat16).astype(np.float32))

LANES = 128
SUBLANES = 8
A_HALO = 8
D_HALO = 32

COL_A = 0
COL_B = COL_A + 3 * GROUP
COL_K = COL_B + 2 * GROUP
COL_KI = COL_K + GROUP
COL_D = COL_KI + LANES
ROW_COLS = COL_D + 2 * GROUP
TROW_Q = 0
TROW_V = TROW_Q + GROUP
TROW_QI = TROW_V + GROUP
TROW_WI = TROW_QI + IDX_HEADS * IDX_DIM
WI_ROWS = 16
TROWS = TROW_WI + WI_ROWS

MIX_TM = 1024
DSA_QB = 512
DSA_KC = 512
DSA_SUB = 64
CNT_ROWS = 16
CNTB_ROWS = 32
TIE_TILE = 128
BF16_ROWS = 16
SHIFT_LIMIT = 45.0
FFN_TM = 512
VMEM_LIMIT = 56 * 1024 * 1024

_NT = (((1,), (1,)), ((), ()))


def _rmsnorm(xf, g):
    ms = jnp.mean(xf * xf, axis=-1, keepdims=True)
    return xf * lax.rsqrt(ms + NORM_EPS) * g


def _layernorm(xf, g, b):
    mu = jnp.mean(xf, axis=-1, keepdims=True)
    xc = xf - mu
    var = jnp.mean(xc * xc, axis=-1, keepdims=True)
    return xc * lax.rsqrt(var + LN_EPS) * g + b


def _rope128(x, cos, sin_signed):
    half = HEAD_DIM // 2
    lane = lax.broadcasted_iota(jnp.int32, x.shape, 1) % HEAD_DIM
    up = pltpu.roll(x, LANES - half, axis=1)
    dn = pltpu.roll(x, half, axis=1)
    return x * cos + jnp.where(lane < half, up, dn) * sin_signed


def _rope_t(x, cos_t, sin_t):
    half = HEAD_DIM // 2
    x1, x2 = x[0:half, :], x[half:HEAD_DIM, :]
    return x1 * cos_t - x2 * sin_t, x2 * cos_t + x1 * sin_t


def _mix_kernel(x_ref, g_ref, w_ref, wt_ref, cos_ref, sin_ref, cost_ref, sint_ref,
                wca_ref, lngb_ref, lnbb_ref, ws_ref, bsb_ref, wcf_ref, bcf_ref,
                lngd_ref, lnbd_ref,
                yab_ref, yd_ref, k4_ref, ki_ref, qt_ref, vt_ref, qit_ref, wit_ref,
                abuf, dbuf, sbuf):
    tm = x_ref.shape[0]
    f32, bf16 = jnp.float32, jnp.bfloat16
    half = HEAD_DIM // 2

    @pl.when(pl.program_id(1) == 0)
    def _():
        abuf[0:A_HALO, :] = jnp.zeros((A_HALO, GROUP), f32)
        dbuf[0:D_HALO, :] = jnp.zeros((D_HALO, GROUP), f32)

    h = _rmsnorm(x_ref[...], g_ref[...]).astype(bf16)

    def proj(lo, hi):
        return jnp.dot(h, w_ref[:, lo:hi], preferred_element_type=f32)

    def proj_t(lo, hi):
        return lax.dot_general(wt_ref[lo:hi, :], h, _NT, preferred_element_type=f32)

    zd = proj(COL_D, ROW_COLS)
    dbuf[D_HALO:D_HALO + tm, :] = zd[:, 0:GROUP] * jax.nn.sigmoid(zd[:, GROUP:2 * GROUP])
    acc = jnp.broadcast_to(bcf_ref[...], (tm, GROUP))
    first = D_HALO - (CONF_W - 1)
    for r in range(SUBLANES):
        taps = [t for t in range(CONF_W) if (first + t) % SUBLANES == r]
        span = (first + taps[-1]) - (first + taps[0]) + tm
        sbuf[0:span, :] = dbuf[pl.ds(first + taps[0], span), :]
        for t in taps:
            lo = t - taps[0]
            acc = acc + sbuf[lo:lo + tm, :] * wcf_ref[t:t + 1, :]
    y = _layernorm(acc, lngd_ref[...], lnbd_ref[...])
    yd_ref[...] = (y * jax.nn.sigmoid(y)).astype(bf16)
    dbuf[0:D_HALO, :] = dbuf[tm:tm + D_HALO, :]

    za = proj(COL_A, COL_B)
    abuf[A_HALO:A_HALO + tm, :] = za[:, GROUP:2 * GROUP] * za[:, 2 * GROUP:3 * GROUP]
    conv = abuf[pl.ds(A_HALO - 2, tm), :] * wca_ref[0:1, :]
    conv = conv + abuf[pl.ds(A_HALO - 1, tm), :] * wca_ref[1:2, :]
    conv = conv + abuf[pl.ds(A_HALO, tm), :] * wca_ref[2:3, :]
    yab_ref[:, 0:GROUP] = (za[:, 0:GROUP] * conv).astype(bf16)
    abuf[0:A_HALO, :] = abuf[tm:tm + A_HALO, :]

    zb = proj(COL_B, COL_K)
    vn = _layernorm(zb[:, GROUP:2 * GROUP], lngb_ref[...], lnbb_ref[...]).astype(bf16)
    r_i = lax.broadcasted_iota(jnp.int32, (CHUNK, CHUNK), 0)
    c_i = lax.broadcasted_iota(jnp.int32, (CHUNK, CHUNK), 1)
    wcat = jnp.concatenate(
        [jnp.where(r_i >= c_i, ws_ref[hd], 0.0).astype(bf16) for hd in range(HEADS)], axis=1)
    lane_head = lax.broadcasted_iota(jnp.int32, (CHUNK, GROUP), 1) // HEAD_DIM
    zero_b = jnp.zeros((), bf16)
    for c in range(tm // CHUNK):
        rows = slice(c * CHUNK, (c + 1) * CHUNK)
        vc = vn[rows, :]
        stacked = jnp.concatenate(
            [jnp.where(lane_head == hd, vc, zero_b) for hd in range(HEADS)], axis=0)
        mixed = bsb_ref[...] + jnp.dot(wcat, stacked, preferred_element_type=f32)
        yab_ref[rows, GROUP:2 * GROUP] = (zb[rows, 0:GROUP] * mixed).astype(bf16)

    cos = cos_ref[...]
    sin = sin_ref[...]
    zk = proj(COL_K, COL_D)
    lane = lax.broadcasted_iota(jnp.int32, (tm, LANES), 1)
    bias_col = jnp.where(lane == HEAD_DIM, 1.0, 0.0)
    for pair in range(2):
        kp = _rope128(zk[:, pair * LANES:(pair + 1) * LANES], cos, sin)
        for sub in range(2):
            kh = kp if sub == 0 else pltpu.roll(kp, HEAD_DIM, axis=1)
            k4_ref[2 * pair + sub] = jnp.where(lane < HEAD_DIM, kh, bias_col).astype(bf16)
    kip = _rope128(zk[:, GROUP:GROUP + LANES], cos, sin)
    ki_ref[...] = kip[:, 0:IDX_DIM].astype(bf16)

    cos_t = cost_ref[...]
    sin_t = sint_ref[...]
    q_scale = HEAD_DIM ** -0.5 * math.log2(math.e)
    zq = proj_t(TROW_Q, TROW_V)
    zqi = proj_t(TROW_QI, TROW_WI)
    for hd in range(HEADS):
        rows = slice(hd * HEAD_DIM, (hd + 1) * HEAD_DIM)
        o1, o2 = _rope_t(zq[rows, :], cos_t, sin_t)
        qt_ref[hd, 0:half, :] = (o1 * q_scale).astype(bf16)
        qt_ref[hd, half:HEAD_DIM, :] = (o2 * q_scale).astype(bf16)
        o1, o2 = _rope_t(zqi[rows, :], cos_t, sin_t)
        qit_ref[hd, 0:half, :] = o1.astype(bf16)
        qit_ref[hd, half:HEAD_DIM, :] = o2.astype(bf16)
    zv = proj_t(TROW_V, TROW_QI)
    for hd in range(HEADS):
        vt_ref[hd] = zv[hd * HEAD_DIM:(hd + 1) * HEAD_DIM, :].astype(bf16)
    zw = proj_t(TROW_WI, TROWS)
    wit_ref[...] = zw[0:SUBLANES, :] * (IDX_HEADS ** -0.5 * IDX_DIM ** -0.5)


def _mix_call(x, g, w_rows, w_t, tables, wca, lngb, lnbb, ws, bsb, wcf, bcf, lngd, lnbd):
    bsz, seq, _ = x.shape
    n = bsz * seq
    tm = MIX_TM
    nt = seq // tm
    bf16 = jnp.bfloat16
    half = HEAD_DIM // 2
    cos, sin, cos_t, sin_t = tables
    row = lambda b, i: (b * nt + i, 0)
    row4 = lambda b, i: (0, b * nt + i, 0)
    col = lambda b, i: (0, b * nt + i)
    col4 = lambda b, i: (0, 0, b * nt + i)
    const2 = lambda b, i: (0, 0)
    const3 = lambda b, i: (0, 0, 0)
    in_specs = [
        pl.BlockSpec((None, tm, D_MODEL), lambda b, i: (b, i, 0)),
        pl.BlockSpec((1, D_MODEL), const2),
        pl.BlockSpec((D_MODEL, ROW_COLS), const2),
        pl.BlockSpec((TROWS, D_MODEL), const2),
        pl.BlockSpec((tm, LANES), lambda b, i: (i, 0)),
        pl.BlockSpec((tm, LANES), lambda b, i: (i, 0)),
        pl.BlockSpec((half, tm), lambda b, i: (0, i)),
        pl.BlockSpec((half, tm), lambda b, i: (0, i)),
        pl.BlockSpec((SHORT_W, GROUP), const2),
        pl.BlockSpec((1, GROUP), const2),
        pl.BlockSpec((1, GROUP), const2),
        pl.BlockSpec((HEADS, CHUNK, CHUNK), const3),
        pl.BlockSpec((CHUNK, GROUP), const2),
        pl.BlockSpec((CONF_W, GROUP), const2),
        pl.BlockSpec((1, GROUP), const2),
        pl.BlockSpec((1, GROUP), const2),
        pl.BlockSpec((1, GROUP), const2),
    ]
    out_shape = [
        jax.ShapeDtypeStruct((n, 2 * GROUP), bf16),
        jax.ShapeDtypeStruct((n, GROUP), bf16),
        jax.ShapeDtypeStruct((HEADS, n, LANES), bf16),
        jax.ShapeDtypeStruct((n, IDX_DIM), bf16),
        jax.ShapeDtypeStruct((HEADS, HEAD_DIM, n), bf16),
        jax.ShapeDtypeStruct((HEADS, HEAD_DIM, n), bf16),
        jax.ShapeDtypeStruct((IDX_HEADS, IDX_DIM, n), bf16),
        jax.ShapeDtypeStruct((SUBLANES, n), jnp.float32),
    ]
    out_specs = [
        pl.BlockSpec((tm, 2 * GROUP), row),
        pl.BlockSpec((tm, GROUP), row),
        pl.BlockSpec((HEADS, tm, LANES), row4),
        pl.BlockSpec((tm, IDX_DIM), row),
        pl.BlockSpec((HEADS, HEAD_DIM, tm), col4),
        pl.BlockSpec((HEADS, HEAD_DIM, tm), col4),
        pl.BlockSpec((IDX_HEADS, IDX_DIM, tm), col4),
        pl.BlockSpec((SUBLANES, tm), col),
    ]
    return pl.pallas_call(
        _mix_kernel,
        grid=(bsz, nt),
        in_specs=in_specs,
        out_specs=out_specs,
        out_shape=out_shape,
        scratch_shapes=[pltpu.VMEM((tm + A_HALO, GROUP), jnp.float32),
                        pltpu.VMEM((tm + D_HALO, GROUP), jnp.float32),
                        pltpu.VMEM((tm + D_HALO, GROUP), jnp.float32)],
        compiler_params=pltpu.CompilerParams(
            dimension_semantics=("arbitrary", "arbitrary"),
            vmem_limit_bytes=VMEM_LIMIT),
        name="mix",
    )(x, g, w_rows, w_t, cos, sin, cos_t, sin_t, wca, lngb, lnbb, ws, bsb, wcf, bcf,
      lngd, lnbd)


def _threshold_bits(trial):
    bits = jnp.where(trial < 0, trial & jnp.int32(0x7FFFFFFF), ~trial)
    return pltpu.bitcast(bits, jnp.float32)


def _dsa_kernel(qit_ref, wit_ref, qt_ref, ki_ref, k4_ref, vt_ref, o_ref,
                sc_ref, scb_ref, acc_ref, s_ref, p_ref, qa_ref, l_ref, kmax_ref, *, seq):
    f32, bf16, i32 = jnp.float32, jnp.bfloat16, jnp.int32
    qb = wit_ref.shape[1]
    kc, sub = DSA_KC, DSA_SUB
    ndiag = qb // kc
    i = pl.program_id(1)
    nfull = i * ndiag
    nchunks = nfull + ndiag
    n_outside = seq - nchunks * kc
    key_i = lax.broadcasted_iota(i32, (sub, qb), 0)
    qry_i = lax.broadcasted_iota(i32, (sub, qb), 1)

    def causal(x, d, r, lo=0):
        if x.shape == key_i.shape:
            k_i, q_i = key_i, qry_i
        else:
            k_i = lax.broadcasted_iota(i32, x.shape, 0)
            q_i = lax.broadcasted_iota(i32, x.shape, 1)
        return jnp.where(k_i + (d * kc + r - lo) <= q_i, x, NEG)

    def first_lane(d, r):
        return 0 if d is None else ((d * kc + r) // LANES) * LANES

    def run_chunks(chunk_fn, carry):
        carry = lax.fori_loop(0, nfull, lambda c, cr: chunk_fn(c, None, cr), carry)
        for d in range(ndiag):
            carry = chunk_fn(nfull + d, d, carry)
        return carry

    def score_chunk(c, d, carry):
        off = pl.multiple_of(c * kc, kc)
        for t in range(kc // sub):
            r0 = off + t * sub
            kis = ki_ref[pl.ds(r0, sub), :]
            lo = first_lane(d, t * sub)
            sc = None
            for j in range(IDX_HEADS):
                rel = jnp.dot(kis, qit_ref[j, :, lo:qb], preferred_element_type=f32)
                term = jnp.maximum(rel, 0.0) * wit_ref[j:j + 1, lo:qb]
                sc = term if sc is None else sc + term
            if d is not None:
                sc = causal(sc, d, t * sub, lo)
            if lo:
                sc = jnp.concatenate([jnp.full((sub, lo), NEG, f32), sc], axis=1)
            sc_ref[pl.ds(r0, sub), :] = sc
            scb_ref[pl.ds(r0, sub), :] = sc.astype(bf16)
        return carry

    run_chunks(score_chunk, 0)

    assert ndiag == 1
    lane_tile = lax.broadcasted_iota(i32, (1, qb), 1) // LANES
    n_masked = n_outside + (kc - (lane_tile + 1) * LANES)

    def count_pass(ref, rows, zero, indicator):
        def update(acc, r0, lo):
            ind = indicator(ref[pl.ds(r0, rows), lo:qb], lo)
            if lo == 0:
                return acc + ind
            return jnp.concatenate([acc[:, :lo], acc[:, lo:] + ind], axis=1)

        def body(c, acc):
            off = pl.multiple_of(c * kc, kc)
            for t in range(kc // rows):
                acc = update(acc, off + t * rows, 0)
            return acc

        acc = lax.fori_loop(0, nfull, body, zero)
        doff = pl.multiple_of(nfull * kc, kc)
        for t in range(kc // rows):
            acc = update(acc, doff + t * rows, first_lane(0, t * rows))
        return acc

    def count(cmp, thr):
        acc = count_pass(sc_ref, CNT_ROWS, jnp.zeros((CNT_ROWS, qb), i32),
                         lambda x, lo: jnp.where(cmp(x, thr[:, lo:qb]), 1, 0))
        return jnp.sum(acc, axis=0, keepdims=True)

    def count_ge(thr):
        return count(lambda x, t: x >= t, thr) + jnp.where(thr <= NEG, n_masked, 0)

    neg_b = NEG_BF16
    one_b, zero_b = jnp.ones((), bf16), jnp.zeros((), bf16)

    def count_ge_b(thr):
        thr_b = thr.astype(bf16)
        acc = count_pass(scb_ref, CNTB_ROWS, jnp.zeros((CNTB_ROWS, qb), bf16),
                         lambda x, lo: jnp.where(x >= thr_b[:, lo:qb], one_b, zero_b))
        inside = jnp.sum(acc.astype(f32), axis=0, keepdims=True).astype(i32)
        return inside + jnp.where(thr_b.astype(f32) <= neg_b, n_masked, 0)

    def coarse_step(step, prefix):
        trial = prefix | jnp.left_shift(jnp.int32(1), 31 - step)
        cnt = count_ge_b(_threshold_bits(trial))
        return jnp.where(cnt >= TOPK, trial, prefix)

    coarse = lax.fori_loop(0, 16, coarse_step, jnp.zeros((1, qb), i32))

    start = coarse - jnp.int32(1 << 16)

    def fine_step(step, state):
        prefix, c_ge = state
        trial = prefix + jnp.left_shift(jnp.int32(1), 16 - step)
        cnt = count_ge(_threshold_bits(trial))
        accept = cnt >= TOPK
        return jnp.where(accept, trial, prefix), jnp.where(accept, cnt, c_ge)

    prefix, c_ge = lax.fori_loop(0, 17, fine_step, (start, jnp.full((1, qb), seq, i32)))
    vstar = _threshold_bits(prefix)

    has_ties = jnp.max(c_ge) > TOPK

    @pl.when(has_ties)
    def _():
        c_gt = count(lambda x, t: x > t, vstar) + jnp.where(vstar < NEG, n_masked, 0)
        want = (TOPK - c_gt).astype(f32)
        tt = TIE_TILE
        tri = jnp.where(lax.broadcasted_iota(i32, (tt, tt), 0)
                        >= lax.broadcasted_iota(i32, (tt, tt), 1), 1.0, 0.0).astype(bf16)

        def tie_chunk(c, d, seen):
            off = pl.multiple_of(c * kc, kc)
            running = []
            for t in range(kc // tt):
                s = sc_ref[pl.ds(off + t * tt, tt), :]
                tie = jnp.where(s == vstar, 1.0, 0.0).astype(bf16)
                running.append(jnp.dot(tri, tie, preferred_element_type=f32))
            for t in range(kc // tt):
                s = sc_ref[pl.ds(off + t * tt, tt), :]
                rank = seen + running[t]
                admitted = jnp.where(rank <= want, jnp.inf, NEG)
                cap = jnp.where(s > vstar, jnp.inf, jnp.where(s == vstar, admitted, NEG))
                if d is not None:
                    cap = causal(cap, d, t * tt)
                sc_ref[pl.ds(off + t * tt, tt), :] = cap
                seen = seen + running[t][tt - 1:tt, :]
            return seen

        run_chunks(tie_chunk, jnp.zeros((1, qb), f32))

    def fold8(x, op):
        return op(x.reshape(sub // SUBLANES, SUBLANES, x.shape[1]), axis=0)

    @pl.when(jnp.logical_not(has_ties))
    def _():
        def cap_chunk(c, d, carry):
            off = pl.multiple_of(c * kc, kc)
            for t in range(kc // sub):
                r = t * sub
                cap = jnp.where(sc_ref[pl.ds(off + r, sub), :] >= vstar, jnp.inf, NEG)
                if d is not None:
                    cap = causal(cap, d, r)
                sc_ref[pl.ds(off + r, sub), :] = cap
            return carry

        run_chunks(cap_chunk, 0)

    acc_ref[...] = jnp.zeros(acc_ref.shape, f32)

    @pl.when(i == 0)
    def _():
        kmax_ref[...] = jnp.zeros(kmax_ref.shape, f32)

    bounds = []
    for hd in range(HEADS):
        kk = k4_ref[hd, pl.ds(pl.multiple_of(i * qb, qb), qb), :].astype(f32)
        k2 = jnp.max(jnp.sum(kk * kk, axis=1, keepdims=True), axis=0, keepdims=True)
        kmax_ref[hd] = jnp.maximum(kmax_ref[hd], k2)
        qf = qt_ref[hd].astype(f32)
        q2 = jnp.sum(qf * qf, axis=0, keepdims=True)
        bounds.append(jnp.sqrt(q2 * kmax_ref[hd][0:1, 0:1]))
    worst = bounds[0]
    for hd in range(1, HEADS):
        worst = jnp.maximum(worst, bounds[hd])
    fast = jnp.max(worst) <= SHIFT_LIMIT

    row16 = lax.broadcasted_iota(i32, (BF16_ROWS, qb), 0)
    for hd in range(HEADS):
        qa_ref[hd, 0:HEAD_DIM, :] = qt_ref[hd]
        shift = jnp.where(fast, -bounds[hd], 0.0)
        qa_ref[hd, HEAD_DIM:HEAD_DIM + BF16_ROWS, :] = jnp.where(
            row16 == 0, shift, 0.0).astype(bf16)
        qa_ref[hd, HEAD_DIM + BF16_ROWS:LANES, :] = jnp.zeros(
            (LANES - HEAD_DIM - BF16_ROWS, qb), bf16)

    def capped_logits(hd, off, r):
        s = jnp.dot(k4_ref[hd, pl.ds(off + r, sub), :], qa_ref[hd], preferred_element_type=f32)
        return jnp.minimum(s, sc_ref[pl.ds(off + r, sub), :])

    @pl.when(fast)
    def _():
        def chunk(c, d, lparts):
            off = pl.multiple_of(c * kc, kc)
            lparts = list(lparts)
            for hd in range(HEADS):
                for t in range(kc // sub):
                    r = t * sub
                    lo = first_lane(d, r)
                    s = jnp.dot(k4_ref[hd, pl.ds(off + r, sub), :], qa_ref[hd, :, lo:qb],
                                preferred_element_type=f32)
                    p = jnp.exp2(jnp.minimum(s, sc_ref[pl.ds(off + r, sub), lo:qb]))
                    l8, pb = fold8(p, jnp.sum), p.astype(bf16)
                    if lo:
                        l8 = jnp.concatenate([jnp.zeros((SUBLANES, lo), f32), l8], axis=1)
                        pb = jnp.concatenate([jnp.zeros((sub, lo), bf16), pb], axis=1)
                    lparts[hd] = lparts[hd] + l8
                    p_ref[hd, r:r + sub, :] = pb
            for hd in range(HEADS):
                acc_ref[hd] += jnp.dot(vt_ref[hd, :, pl.ds(off, kc)], p_ref[hd],
                                       preferred_element_type=f32)
            return tuple(lparts)

        lparts = run_chunks(chunk, tuple(jnp.zeros((SUBLANES, qb), f32) for _ in range(HEADS)))
        for hd in range(HEADS):
            l_ref[hd] = lparts[hd]

    @pl.when(jnp.logical_not(fast))
    def _():
        def attend_chunk(c, d, carry):
            ms, ls = carry
            off = pl.multiple_of(c * kc, kc)
            new_ms, new_ls = list(ms), list(ls)

            def logits_stage(hd):
                mpart = None
                for t in range(kc // sub):
                    r = t * sub
                    s = capped_logits(hd, off, r)
                    s_ref[hd, r:r + sub, :] = s
                    m8 = fold8(s, jnp.max)
                    mpart = m8 if mpart is None else jnp.maximum(mpart, m8)
                new_ms[hd] = jnp.maximum(ms[hd], jnp.max(mpart, axis=0, keepdims=True))

            def value_stage(hd):
                lpart = None
                for t in range(kc // sub):
                    r = t * sub
                    p = jnp.exp2(s_ref[hd, r:r + sub, :] - new_ms[hd])
                    l8 = fold8(p, jnp.sum)
                    lpart = l8 if lpart is None else lpart + l8
                    p_ref[hd, r:r + sub, :] = p.astype(bf16)
                alpha = jnp.exp2(ms[hd] - new_ms[hd])
                new_ls[hd] = alpha * ls[hd] + jnp.sum(lpart, axis=0, keepdims=True)
                pv = jnp.dot(vt_ref[hd, :, pl.ds(off, kc)], p_ref[hd],
                             preferred_element_type=f32)
                acc_ref[hd] = acc_ref[hd] * alpha + pv

            logits_stage(0)
            for hd in range(HEADS):
                if hd + 1 < HEADS:
                    logits_stage(hd + 1)
                value_stage(hd)
            return tuple(new_ms), tuple(new_ls)

        init = (tuple(jnp.full((1, qb), NEG, f32) for _ in range(HEADS)),
                tuple(jnp.zeros((1, qb), f32) for _ in range(HEADS)))
        _, ls = run_chunks(attend_chunk, init)
        row8 = lax.broadcasted_iota(i32, (SUBLANES, qb), 0)
        for hd in range(HEADS):
            l_ref[hd] = jnp.where(row8 == 0, ls[hd], 0.0)

    out_t = jnp.concatenate(
        [acc_ref[hd] / jnp.sum(l_ref[hd], axis=0, keepdims=True) for hd in range(HEADS)],
        axis=0)
    o_ref[...] = out_t.T.astype(o_ref.dtype)


def _dsa_call(qit, wit, qt, ki, k4, vt, bsz, seq):
    n = bsz * seq
    qb = DSA_QB
    nq = seq // qb
    qcol4 = lambda b, i: (0, 0, b * nq + i)
    return pl.pallas_call(
        functools.partial(_dsa_kernel, seq=seq),
        grid=(bsz, nq),
        in_specs=[
            pl.BlockSpec((IDX_HEADS, IDX_DIM, qb), qcol4),
            pl.BlockSpec((SUBLANES, qb), lambda b, i: (0, b * nq + i)),
            pl.BlockSpec((HEADS, HEAD_DIM, qb), qcol4),
            pl.BlockSpec((seq, IDX_DIM), lambda b, i: (b, 0)),
            pl.BlockSpec((HEADS, seq, LANES), lambda b, i: (0, b, 0)),
            pl.BlockSpec((HEADS, HEAD_DIM, seq), lambda b, i: (0, 0, b)),
        ],
        out_specs=pl.BlockSpec((qb, GROUP), lambda b, i: (b * nq + i, 0)),
        out_shape=jax.ShapeDtypeStruct((n, GROUP), jnp.bfloat16),
        scratch_shapes=[
            pltpu.VMEM((seq, qb), jnp.float32),
            pltpu.VMEM((seq, qb), jnp.bfloat16),
            pltpu.VMEM((HEADS, HEAD_DIM, qb), jnp.float32),
            pltpu.VMEM((HEADS, DSA_KC, qb), jnp.float32),
            pltpu.VMEM((HEADS, DSA_KC, qb), jnp.bfloat16),
            pltpu.VMEM((HEADS, LANES, qb), jnp.bfloat16),
            pltpu.VMEM((HEADS, SUBLANES, qb), jnp.float32),
            pltpu.VMEM((HEADS, SUBLANES, LANES), jnp.float32),
        ],
        compiler_params=pltpu.CompilerParams(
            dimension_semantics=("arbitrary", "arbitrary"),
            vmem_limit_bytes=VMEM_LIMIT),
        name="dsa",
    )(qit, wit, qt, ki, k4, vt)


def _out_ffn_kernel(x_ref, yab_ref, yc_ref, yd_ref, wout_ref, gffn_ref, wg_ref, wu_ref,
                    wd_ref, gfin_ref, o_ref, *, final):
    f32, bf16 = jnp.float32, jnp.bfloat16
    y = jnp.concatenate([yab_ref[...], yc_ref[...], yd_ref[...]], axis=1)
    x1 = x_ref[...] + jnp.dot(y, wout_ref[...], preferred_element_type=f32)
    hf = _rmsnorm(x1, gffn_ref[...]).astype(bf16)
    gate = jnp.dot(hf, wg_ref[...], preferred_element_type=f32)
    up = jnp.dot(hf, wu_ref[...], preferred_element_type=f32)
    act = (gate * jax.nn.sigmoid(gate) * up).astype(bf16)
    out = x1 + jnp.dot(act, wd_ref[...], preferred_element_type=f32)
    if final:
        out = _rmsnorm(out, gfin_ref[...])
    o_ref[...] = out


def _out_ffn_call(x2d, yab, yc, yd, wout, gffn, wg, wu, wd, gfin, layer, final):
    n = x2d.shape[0]
    tm = FFN_TM
    row = lambda i: (i, 0)
    const = lambda i: (0, 0)
    resident = lambda shape: pl.BlockSpec((None,) + shape, lambda i: (layer, 0, 0),
                                          pipeline_mode=pl.Buffered(1))
    return pl.pallas_call(
        functools.partial(_out_ffn_kernel, final=final),
        grid=(n // tm,),
        in_specs=[
            pl.BlockSpec((tm, D_MODEL), row),
            pl.BlockSpec((tm, 2 * GROUP), row),
            pl.BlockSpec((tm, GROUP), row),
            pl.BlockSpec((tm, GROUP), row),
            resident((D_MODEL, D_MODEL)),
            pl.BlockSpec((1, D_MODEL), const),
            resident((D_MODEL, FFN_HIDDEN)),
            resident((D_MODEL, FFN_HIDDEN)),
            resident((FFN_HIDDEN, D_MODEL)),
            pl.BlockSpec((1, D_MODEL), const),
        ],
        out_specs=pl.BlockSpec((tm, D_MODEL), row),
        out_shape=jax.ShapeDtypeStruct((n, D_MODEL), jnp.float32),
        compiler_params=pltpu.CompilerParams(
            dimension_semantics=("arbitrary",),
            vmem_limit_bytes=VMEM_LIMIT),
        name="out_ffn",
    )(x2d, yab, yc, yd, wout, gffn, wg, wu, wd, gfin)


def _pack_w_in(w):
    g = GROUP
    cuts = [0, 3 * g, 5 * g, 6 * g, 7 * g, 8 * g, 9 * g, 9 * g + IDX_DIM,
            9 * g + IDX_DIM + IDX_HEADS, 11 * g + IDX_DIM + IDX_HEADS]
    a, b, q, k, v, qi, ki, wi, d = [w[:, cuts[t]:cuts[t + 1]] for t in range(9)]
    z = lambda cols: jnp.zeros((w.shape[0], cols), w.dtype)
    rows = jnp.concatenate([a, b, k, ki, z(LANES - IDX_DIM), d], axis=1)
    trans = jnp.concatenate([q, v, qi, wi, z(WI_ROWS - IDX_HEADS)], axis=1).T
    return rows.astype(jnp.bfloat16), trans.astype(jnp.bfloat16)


def _rope_tables(seq):
    inv_freq = ROPE_THETA ** (-jnp.arange(0, HEAD_DIM, 2, dtype=jnp.float32) / HEAD_DIM)
    ang = jnp.arange(seq, dtype=jnp.float32)[:, None] * inv_freq[None, :]
    c, s = jnp.cos(ang), jnp.sin(ang)
    reps = LANES // HEAD_DIM
    cos = jnp.tile(jnp.concatenate([c, c], axis=1), (1, reps))
    sin = jnp.tile(jnp.concatenate([-s, s], axis=1), (1, reps))
    return cos, sin, c.T, s.T


def kernel(x, g_mix, w_in, w_conv_a, gmlp_ln_g, gmlp_ln_b, w_s, b_s, w_conf, b_conf,
           conf_ln_g, conf_ln_b, w_out, g_ffn, w_gate, w_up, w_down, g_final):
    bsz, seq, d = x.shape
    depth = w_in.shape[0]
    bf16 = jnp.bfloat16
    tables = _rope_tables(seq)
    w_out_b, w_gate_b, w_up_b, w_down_b = (
        w.astype(bf16) for w in (w_out, w_gate, w_up, w_down))
    xs = x
    for l in range(depth):
        w_rows, w_t = _pack_w_in(w_in[l])
        yab, yd, k4, ki, qt, vt, qit, wit = _mix_call(
            xs.reshape(bsz, seq, d), g_mix[l][None, :], w_rows, w_t, tables,
            w_conv_a[l], gmlp_ln_g[l][None, :], gmlp_ln_b[l][None, :], w_s[l],
            jnp.repeat(b_s[l].T, HEAD_DIM, axis=1), w_conf[l], b_conf[l][None, :],
            conf_ln_g[l][None, :], conf_ln_b[l][None, :])
        yc = _dsa_call(qit, wit, qt, ki, k4, vt, bsz, seq)
        xs = _out_ffn_call(
            xs.reshape(bsz * seq, d), yab, yc, yd, w_out_b, g_ffn[l][None, :],
            w_gate_b, w_up_b, w_down_b, g_final[None, :], layer=l, final=(l == depth - 1))
    return xs.reshape(bsz, seq, d)
```

```python
import functools
import math

import jax
import jax.numpy as jnp
import numpy as np
from jax import lax
from jax.experimental import pallas as pl
from jax.experimental.pallas import tpu as pltpu

D_MODEL = 1024
GROUP = 256
HEADS = 4
HEAD_DIM = 64
CHUNK = 128
IDX_HEADS = 4
IDX_DIM = 64
TOPK = 256
SHORT_W = 3
CONF_W = 31
FFN_HIDDEN = 2816
ROPE_THETA = 10000.0
NORM_EPS = 1e-6
LN_EPS = 1e-5
NEG = -1e30
NEG_BF16 = float(np.asarray(NEG, dtype=jnp.bfloat16).astype(np.float32))

LANES = 128
SUBLANES = 8
A_HALO = 8
D_HALO = 32

COL_A = 0
COL_B = COL_A + 3 * GROUP
COL_K = COL_B + 2 * GROUP
COL_KI = COL_K + GROUP
COL_D = COL_KI + LANES
ROW_COLS = COL_D + 2 * GROUP
TROW_Q = 0
TROW_V = TROW_Q + GROUP
TROW_QI = TROW_V + GROUP
TROW_WI = TROW_QI + IDX_HEADS * IDX_DIM
WI_ROWS = 16
TROWS = TROW_WI + WI_ROWS

MIX_TM = 1024
DSA_QB = 512
DSA_KC = 512
DSA_SUB = 64
CNT_ROWS = 16
CNTB_ROWS = 32
TIE_TILE = 128
BF16_ROWS = 16
SHIFT_LIMIT = 45.0
FFN_TM = 512
VMEM_LIMIT = 56 * 1024 * 1024

_NT = (((1,), (1,)), ((), ()))


def _rmsnorm(xf, g):
    ms = jnp.mean(xf * xf, axis=-1, keepdims=True)
    return xf * lax.rsqrt(ms + NORM_EPS) * g


def _layernorm(xf, g, b):
    mu = jnp.mean(xf, axis=-1, keepdims=True)
    xc = xf - mu
    var = jnp.mean(xc * xc, axis=-1, keepdims=True)
    return xc * lax.rsqrt(var + LN_EPS) * g + b


def _rope128(x, cos, sin_signed):
    half = HEAD_DIM // 2
    lane = lax.broadcasted_iota(jnp.int32, x.shape, 1) % HEAD_DIM
    up = pltpu.roll(x, LANES - half, axis=1)
    dn = pltpu.roll(x, half, axis=1)
    return x * cos + jnp.where(lane < half, up, dn) * sin_signed


def _rope_t(x, cos_t, sin_t):
    half = HEAD_DIM // 2
    x1, x2 = x[0:half, :], x[half:HEAD_DIM, :]
    return x1 * cos_t - x2 * sin_t, x2 * cos_t + x1 * sin_t


def _mix_kernel(x_ref, g_ref, w_ref, wt_ref, cos_ref, sin_ref, cost_ref, sint_ref,
                wca_ref, lngb_ref, lnbb_ref, ws_ref, bsb_ref, wcf_ref, bcf_ref,
                lngd_ref, lnbd_ref,
                yab_ref, yd_ref, k4_ref, ki_ref, qt_ref, vt_ref, qit_ref, wit_ref,
                abuf, dbuf, sbuf):
    tm = x_ref.shape[0]
    f32, bf16 = jnp.float32, jnp.bfloat16
    half = HEAD_DIM // 2

    @pl.when(pl.program_id(1) == 0)
    def _():
        abuf[0:A_HALO, :] = jnp.zeros((A_HALO, GROUP), f32)
        dbuf[0:D_HALO, :] = jnp.zeros((D_HALO, GROUP), f32)

    h = _rmsnorm(x_ref[...], g_ref[...]).astype(bf16)

    def proj(lo, hi):
        return jnp.dot(h, w_ref[:, lo:hi], preferred_element_type=f32)

    def proj_t(lo, hi):
        return lax.dot_general(wt_ref[lo:hi, :], h, _NT, preferred_element_type=f32)

    zd = proj(COL_D, ROW_COLS)
    dbuf[D_HALO:D_HALO + tm, :] = zd[:, 0:GROUP] * jax.nn.sigmoid(zd[:, GROUP:2 * GROUP])
    acc = jnp.broadcast_to(bcf_ref[...], (tm, GROUP))
    first = D_HALO - (CONF_W - 1)
    for r in range(SUBLANES):
        taps = [t for t in range(CONF_W) if (first + t) % SUBLANES == r]
        span = (first + taps[-1]) - (first + taps[0]) + tm
        sbuf[0:span, :] = dbuf[pl.ds(first + taps[0], span), :]
        for t in taps:
            lo = t - taps[0]
            acc = acc + sbuf[lo:lo + tm, :] * wcf_ref[t:t + 1, :]
    y = _layernorm(acc, lngd_ref[...], lnbd_ref[...])
    yd_ref[...] = (y * jax.nn.sigmoid(y)).astype(bf16)
    dbuf[0:D_HALO, :] = dbuf[tm:tm + D_HALO, :]

    za = proj(COL_A, COL_B)
    abuf[A_HALO:A_HALO + tm, :] = za[:, GROUP:2 * GROUP] * za[:, 2 * GROUP:3 * GROUP]
    conv = abuf[pl.ds(A_HALO - 2, tm), :] * wca_ref[0:1, :]
    conv = conv + abuf[pl.ds(A_HALO - 1, tm), :] * wca_ref[1:2, :]
    conv = conv + abuf[pl.ds(A_HALO, tm), :] * wca_ref[2:3, :]
    yab_ref[:, 0:GROUP] = (za[:, 0:GROUP] * conv).astype(bf16)
    abuf[0:A_HALO, :] = abuf[tm:tm + A_HALO, :]

    zb = proj(COL_B, COL_K)
    vn = _layernorm(zb[:, GROUP:2 * GROUP], lngb_ref[...], lnbb_ref[...]).astype(bf16)
    r_i = lax.broadcasted_iota(jnp.int32, (CHUNK, CHUNK), 0)
    c_i = lax.broadcasted_iota(jnp.int32, (CHUNK, CHUNK), 1)
    wcat = jnp.concatenate(
        [jnp.where(r_i >= c_i, ws_ref[hd], 0.0).astype(bf16) for hd in range(HEADS)], axis=1)
    lane_head = lax.broadcasted_iota(jnp.int32, (CHUNK, GROUP), 1) // HEAD_DIM
    zero_b = jnp.zeros((), bf16)
    for c in range(tm // CHUNK):
        rows = slice(c * CHUNK, (c + 1) * CHUNK)
        vc = vn[rows, :]
        stacked = jnp.concatenate(
            [jnp.where(lane_head == hd, vc, zero_b) for hd in range(HEADS)], axis=0)
        mixed = bsb_ref[...] + jnp.dot(wcat, stacked, preferred_element_type=f32)
        yab_ref[rows, GROUP:2 * GROUP] = (zb[rows, 0:GROUP] * mixed).astype(bf16)

    cos = cos_ref[...]
    sin = sin_ref[...]
    zk = proj(COL_K, COL_D)
    lane = lax.broadcasted_iota(jnp.int32, (tm, LANES), 1)
    bias_col = jnp.where(lane == HEAD_DIM, 1.0, 0.0)
    for pair in range(2):
        kp = _rope128(zk[:, pair * LANES:(pair + 1) * LANES], cos, sin)
        for sub in range(2):
            kh = kp if sub == 0 else pltpu.roll(kp, HEAD_DIM, axis=1)
            k4_ref[2 * pair + sub] = jnp.where(lane < HEAD_DIM, kh, bias_col).astype(bf16)
    kip = _rope128(zk[:, GROUP:GROUP + LANES], cos, sin)
    ki_ref[...] = kip[:, 0:IDX_DIM].astype(bf16)

    cos_t = cost_ref[...]
    sin_t = sint_ref[...]
    q_scale = HEAD_DIM ** -0.5 * math.log2(math.e)
    zq = proj_t(TROW_Q, TROW_V)
    zqi = proj_t(TROW_QI, TROW_WI)
    for hd in range(HEADS):
        rows = slice(hd * HEAD_DIM, (hd + 1) * HEAD_DIM)
        o1, o2 = _rope_t(zq[rows, :], cos_t, sin_t)
        qt_ref[hd, 0:half, :] = (o1 * q_scale).astype(bf16)
        qt_ref[hd, half:HEAD_DIM, :] = (o2 * q_scale).astype(bf16)
        o1, o2 = _rope_t(zqi[rows, :], cos_t, sin_t)
        qit_ref[hd, 0:half, :] = o1.astype(bf16)
        qit_ref[hd, half:HEAD_DIM, :] = o2.astype(bf16)
    zv = proj_t(TROW_V, TROW_QI)
    for hd in range(HEADS):
        vt_ref[hd] = zv[hd * HEAD_DIM:(hd + 1) * HEAD_DIM, :].astype(bf16)
    zw = proj_t(TROW_WI, TROWS)
    wit_ref[...] = zw[0:SUBLANES, :] * (IDX_HEADS ** -0.5 * IDX_DIM ** -0.5)


def _mix_call(x, g, w_rows, w_t, tables, wca, lngb, lnbb, ws, bsb, wcf, bcf, lngd, lnbd):
    bsz, seq, _ = x.shape
    n = bsz * seq
    tm = MIX_TM
    nt = seq // tm
    bf16 = jnp.bfloat16
    half = HEAD_DIM // 2
    cos, sin, cos_t, sin_t = tables
    row = lambda b, i: (b * nt + i, 0)
    row4 = lambda b, i: (0, b * nt + i, 0)
    col = lambda b, i: (0, b * nt + i)
    col4 = lambda b, i: (0, 0, b * nt + i)
    const2 = lambda b, i: (0, 0)
    const3 = lambda b, i: (0, 0, 0)
    in_specs = [
        pl.BlockSpec((None, tm, D_MODEL), lambda b, i: (b, i, 0)),
        pl.BlockSpec((1, D_MODEL), const2),
        pl.BlockSpec((D_MODEL, ROW_COLS), const2),
        pl.BlockSpec((TROWS, D_MODEL), const2),
        pl.BlockSpec((tm, LANES), lambda b, i: (i, 0)),
        pl.BlockSpec((tm, LANES), lambda b, i: (i, 0)),
        pl.BlockSpec((half, tm), lambda b, i: (0, i)),
        pl.BlockSpec((half, tm), lambda b, i: (0, i)),
        pl.BlockSpec((SHORT_W, GROUP), const2),
        pl.BlockSpec((1, GROUP), const2),
        pl.BlockSpec((1, GROUP), const2),
        pl.BlockSpec((HEADS, CHUNK, CHUNK), const3),
        pl.BlockSpec((CHUNK, GROUP), const2),
        pl.BlockSpec((CONF_W, GROUP), const2),
        pl.BlockSpec((1, GROUP), const2),
        pl.BlockSpec((1, GROUP), const2),
        pl.BlockSpec((1, GROUP), const2),
    ]
    out_shape = [
        jax.ShapeDtypeStruct((n, 2 * GROUP), bf16),
        jax.ShapeDtypeStruct((n, GROUP), bf16),
        jax.ShapeDtypeStruct((HEADS, n, LANES), bf16),
        jax.ShapeDtypeStruct((n, IDX_DIM), bf16),
        jax.ShapeDtypeStruct((HEADS, HEAD_DIM, n), bf16),
        jax.ShapeDtypeStruct((HEADS, HEAD_DIM, n), bf16),
        jax.ShapeDtypeStruct((IDX_HEADS, IDX_DIM, n), bf16),
        jax.ShapeDtypeStruct((SUBLANES, n), jnp.float32),
    ]
    out_specs = [
        pl.BlockSpec((tm, 2 * GROUP), row),
        pl.BlockSpec((tm, GROUP), row),
        pl.BlockSpec((HEADS, tm, LANES), row4),
        pl.BlockSpec((tm, IDX_DIM), row),
        pl.BlockSpec((HEADS, HEAD_DIM, tm), col4),
        pl.BlockSpec((HEADS, HEAD_DIM, tm), col4),
        pl.BlockSpec((IDX_HEADS, IDX_DIM, tm), col4),
        pl.BlockSpec((SUBLANES, tm), col),
    ]
    return pl.pallas_call(
        _mix_kernel,
        grid=(bsz, nt),
        in_specs=in_specs,
        out_specs=out_specs,
        out_shape=out_shape,
        scratch_shapes=[pltpu.VMEM((tm + A_HALO, GROUP), jnp.float32),
                        pltpu.VMEM((tm + D_HALO, GROUP), jnp.float32),
                        pltpu.VMEM((tm + D_HALO, GROUP), jnp.float32)],
        compiler_params=pltpu.CompilerParams(
            dimension_semantics=("arbitrary", "arbitrary"),
            vmem_limit_bytes=VMEM_LIMIT),
        name="mix",
    )(x, g, w_rows, w_t, cos, sin, cos_t, sin_t, wca, lngb, lnbb, ws, bsb, wcf, bcf,
      lngd, lnbd)


def _threshold_bits(trial):
    bits = jnp.where(trial < 0, trial & jnp.int32(0x7FFFFFFF), ~trial)
    return pltpu.bitcast(bits, jnp.float32)


def _dsa_kernel(qit_ref, wit_ref, qt_ref, ki_ref, k4_ref, vt_ref, o_ref,
                sc_ref, scb_ref, acc_ref, s_ref, p_ref, qa_ref, l_ref, kmax_ref, cnt_ref,
                *, seq):
    f32, bf16, i32 = jnp.float32, jnp.bfloat16, jnp.int32
    qb = wit_ref.shape[1]
    kc, sub = DSA_KC, DSA_SUB
    ndiag = qb // kc
    i = pl.program_id(1)
    nfull = i * ndiag
    nchunks = nfull + ndiag
    n_outside = seq - nchunks * kc
    key_i = lax.broadcasted_iota(i32, (sub, qb), 0)
    qry_i = lax.broadcasted_iota(i32, (sub, qb), 1)

    def causal(x, d, r, lo=0):
        if x.shape == key_i.shape:
            k_i, q_i = key_i, qry_i
        else:
            k_i = lax.broadcasted_iota(i32, x.shape, 0)
            q_i = lax.broadcasted_iota(i32, x.shape, 1)
        return jnp.where(k_i + (d * kc + r - lo) <= q_i, x, NEG)

    def first_lane(d, r):
        return 0 if d is None else ((d * kc + r) // LANES) * LANES

    def run_chunks(chunk_fn, carry):
        carry = lax.fori_loop(0, nfull, lambda c, cr: chunk_fn(c, None, cr), carry)
        for d in range(ndiag):
            carry = chunk_fn(nfull + d, d, carry)
        return carry

    def score_chunk(c, d, carry):
        off = pl.multiple_of(c * kc, kc)
        for t in range(kc // sub):
            r0 = off + t * sub
            kis = ki_ref[pl.ds(r0, sub), :]
            lo = first_lane(d, t * sub)
            sc = None
            for j in range(IDX_HEADS):
                rel = jnp.dot(kis, qit_ref[j, :, lo:qb], preferred_element_type=f32)
                term = jnp.maximum(rel, 0.0) * wit_ref[j:j + 1, lo:qb]
                sc = term if sc is None else sc + term
            if d is not None:
                sc = causal(sc, d, t * sub, lo)
            if lo:
                sc = jnp.concatenate([jnp.full((sub, lo), NEG, f32), sc], axis=1)
            sc_ref[pl.ds(r0, sub), :] = sc
            scb_ref[pl.ds(r0, sub), :] = sc.astype(bf16)
        return carry

    run_chunks(score_chunk, 0)

    assert ndiag == 1
    lane_tile = lax.broadcasted_iota(i32, (1, qb), 1) // LANES
    n_masked = n_outside + (kc - (lane_tile + 1) * LANES)

    def count_pass(ref, rows, zero, indicator):
        def update(acc, r0, lo):
            ind = indicator(ref[pl.ds(r0, rows), lo:qb], lo)
            if lo == 0:
                return acc + ind
            return jnp.concatenate([acc[:, :lo], acc[:, lo:] + ind], axis=1)

        def body(c, acc):
            off = pl.multiple_of(c * kc, kc)
            for t in range(kc // rows):
                acc = update(acc, off + t * rows, 0)
            return acc

        acc = lax.fori_loop(0, nfull, body, zero)
        doff = pl.multiple_of(nfull * kc, kc)
        for t in range(kc // rows):
            acc = update(acc, doff + t * rows, first_lane(0, t * rows))
        return acc

    def count(cmp, thr):
        acc = count_pass(sc_ref, CNT_ROWS, jnp.zeros((CNT_ROWS, qb), i32),
                         lambda x, lo: jnp.where(cmp(x, thr[:, lo:qb]), 1, 0))
        return jnp.sum(acc, axis=0, keepdims=True)

    def count_ge(thr):
        return count(lambda x, t: x >= t, thr) + jnp.where(thr <= NEG, n_masked, 0)

    neg_b = NEG_BF16
    one_b, zero_b = jnp.ones((), bf16), jnp.zeros((), bf16)

    def count_ge_b(thr):
        thr_b = thr.astype(bf16)
        acc = count_pass(scb_ref, CNTB_ROWS, jnp.zeros((CNTB_ROWS, qb), bf16),
                         lambda x, lo: jnp.where(x >= thr_b[:, lo:qb], one_b, zero_b))
        inside = jnp.sum(acc.astype(f32), axis=0, keepdims=True).astype(i32)
        return inside + jnp.where(thr_b.astype(f32) <= neg_b, n_masked, 0)

    def coarse_step(step, prefix):
        trial = prefix | jnp.left_shift(jnp.int32(1), 31 - step)
        cnt = count_ge_b(_threshold_bits(trial))
        return jnp.where(cnt >= TOPK, trial, prefix)

    coarse = lax.fori_loop(0, 16, coarse_step, jnp.zeros((1, qb), i32))

    start = coarse - jnp.int32(1 << 16)

    def fine_step(step, state):
        prefix, c_ge, hit = state
        trial = prefix + jnp.left_shift(jnp.int32(1), 16 - step)
        cnt = count_ge(_threshold_bits(trial))
        accept = cnt >= TOPK
        return (jnp.where(accept, trial, prefix), jnp.where(accept, cnt, c_ge),
                jnp.where(accept, 1, hit))

    prefix, c_ge, hit = lax.fori_loop(
        0, 17, fine_step, (start, jnp.full((1, qb), seq, i32), jnp.zeros((1, qb), i32)))
    vstar = _threshold_bits(prefix)

    has_ties = jnp.max(c_ge) > TOPK

    @pl.when(has_ties)
    def _():
        cnt_ref[...] = jnp.broadcast_to(c_ge, cnt_ref.shape)

        @pl.when(jnp.min(hit) == 0)
        def _():
            cnt_ref[...] = jnp.broadcast_to(count_ge(vstar), cnt_ref.shape)

        excess = (cnt_ref[0:1, :] - TOPK).astype(f32)
        tt = TIE_TILE
        tri = jnp.where(lax.broadcasted_iota(i32, (tt, tt), 0)
                        <= lax.broadcasted_iota(i32, (tt, tt), 1), 1.0, 0.0).astype(bf16)

        def tie_chunk(c, d, after):
            off = pl.multiple_of(c * kc, kc)
            trailing = {}
            for t in reversed(range(kc // tt)):
                s = sc_ref[pl.ds(off + t * tt, tt), :]
                tie = jnp.where(s == vstar, 1.0, 0.0).astype(bf16)
                trailing[t] = jnp.dot(tri, tie, preferred_element_type=f32)
            for t in reversed(range(kc // tt)):
                s = sc_ref[pl.ds(off + t * tt, tt), :]
                admitted = jnp.where(after + trailing[t] > excess, jnp.inf, NEG)
                cap = jnp.where(s > vstar, jnp.inf, jnp.where(s == vstar, admitted, NEG))
                if d is not None:
                    cap = causal(cap, d, t * tt)
                sc_ref[pl.ds(off + t * tt, tt), :] = cap
                after = after + trailing[t][0:1, :]
            return after

        after = jnp.zeros((1, qb), f32)
        for d in reversed(range(ndiag)):
            after = tie_chunk(nfull + d, d, after)
        lax.fori_loop(0, nfull, lambda j, a: tie_chunk(nfull - 1 - j, None, a), after)

    def fold8(x, op):
        return op(x.reshape(sub // SUBLANES, SUBLANES, x.shape[1]), axis=0)

    @pl.when(jnp.logical_not(has_ties))
    def _():
        def cap_chunk(c, d, carry):
            off = pl.multiple_of(c * kc, kc)
            for t in range(kc // sub):
                r = t * sub
                cap = jnp.where(sc_ref[pl.ds(off + r, sub), :] >= vstar, jnp.inf, NEG)
                if d is not None:
                    cap = causal(cap, d, r)
                sc_ref[pl.ds(off + r, sub), :] = cap
            return carry

        run_chunks(cap_chunk, 0)

    acc_ref[...] = jnp.zeros(acc_ref.shape, f32)

    @pl.when(i == 0)
    def _():
        kmax_ref[...] = jnp.zeros(kmax_ref.shape, f32)

    bounds = []
    for hd in range(HEADS):
        kk = k4_ref[hd, pl.ds(pl.multiple_of(i * qb, qb), qb), :].astype(f32)
        k2 = jnp.max(jnp.sum(kk * kk, axis=1, keepdims=True), axis=0, keepdims=True)
        kmax_ref[hd] = jnp.maximum(kmax_ref[hd], k2)
        qf = qt_ref[hd].astype(f32)
        q2 = jnp.sum(qf * qf, axis=0, keepdims=True)
        bounds.append(jnp.sqrt(q2 * kmax_ref[hd][0:1, 0:1]))
    worst = bounds[0]
    for hd in range(1, HEADS):
        worst = jnp.maximum(worst, bounds[hd])
    fast = jnp.max(worst) <= SHIFT_LIMIT

    row16 = lax.broadcasted_iota(i32, (BF16_ROWS, qb), 0)
    for hd in range(HEADS):
        qa_ref[hd, 0:HEAD_DIM, :] = qt_ref[hd]
        shift = jnp.where(fast, -bounds[hd], 0.0)
        qa_ref[hd, HEAD_DIM:HEAD_DIM + BF16_ROWS, :] = jnp.where(
            row16 == 0, shift, 0.0).astype(bf16)
        qa_ref[hd, HEAD_DIM + BF16_ROWS:LANES, :] = jnp.zeros(
            (LANES - HEAD_DIM - BF16_ROWS, qb), bf16)

    def capped_logits(hd, off, r):
        s = jnp.dot(k4_ref[hd, pl.ds(off + r, sub), :], qa_ref[hd], preferred_element_type=f32)
        return jnp.minimum(s, sc_ref[pl.ds(off + r, sub), :])

    @pl.when(fast)
    def _():
        def chunk(c, d, lparts):
            off = pl.multiple_of(c * kc, kc)
            lparts = list(lparts)
            for hd in range(HEADS):
                for t in range(kc // sub):
                    r = t * sub
                    lo = first_lane(d, r)
                    s = jnp.dot(k4_ref[hd, pl.ds(off + r, sub), :], qa_ref[hd, :, lo:qb],
                                preferred_element_type=f32)
                    p = jnp.exp2(jnp.minimum(s, sc_ref[pl.ds(off + r, sub), lo:qb]))
                    l8, pb = fold8(p, jnp.sum), p.astype(bf16)
                    if lo:
                        l8 = jnp.concatenate([jnp.zeros((SUBLANES, lo), f32), l8], axis=1)
                        pb = jnp.concatenate([jnp.zeros((sub, lo), bf16), pb], axis=1)
                    lparts[hd] = lparts[hd] + l8
                    p_ref[hd, r:r + sub, :] = pb
            for hd in range(HEADS):
                acc_ref[hd] += jnp.dot(vt_ref[hd, :, pl.ds(off, kc)], p_ref[hd],
                                       preferred_element_type=f32)
            return tuple(lparts)

        lparts = run_chunks(chunk, tuple(jnp.zeros((SUBLANES, qb), f32) for _ in range(HEADS)))
        for hd in range(HEADS):
            l_ref[hd] = lparts[hd]

    @pl.when(jnp.logical_not(fast))
    def _():
        def attend_chunk(c, d, carry):
            ms, ls = carry
            off = pl.multiple_of(c * kc, kc)
            new_ms, new_ls = list(ms), list(ls)

            def logits_stage(hd):
                mpart = None
                for t in range(kc // sub):
                    r = t * sub
                    s = capped_logits(hd, off, r)
                    s_ref[hd, r:r + sub, :] = s
                    m8 = fold8(s, jnp.max)
                    mpart = m8 if mpart is None else jnp.maximum(mpart, m8)
                new_ms[hd] = jnp.maximum(ms[hd], jnp.max(mpart, axis=0, keepdims=True))

            def value_stage(hd):
                lpart = None
                for t in range(kc // sub):
                    r = t * sub
                    p = jnp.exp2(s_ref[hd, r:r + sub, :] - new_ms[hd])
                    l8 = fold8(p, jnp.sum)
                    lpart = l8 if lpart is None else lpart + l8
                    p_ref[hd, r:r + sub, :] = p.astype(bf16)
                alpha = jnp.exp2(ms[hd] - new_ms[hd])
                new_ls[hd] = alpha * ls[hd] + jnp.sum(lpart, axis=0, keepdims=True)
                pv = jnp.dot(vt_ref[hd, :, pl.ds(off, kc)], p_ref[hd],
                             preferred_element_type=f32)
                acc_ref[hd] = acc_ref[hd] * alpha + pv

            logits_stage(0)
            for hd in range(HEADS):
                if hd + 1 < HEADS:
                    logits_stage(hd + 1)
                value_stage(hd)
            return tuple(new_ms), tuple(new_ls)

        init = (tuple(jnp.full((1, qb), NEG, f32) for _ in range(HEADS)),
                tuple(jnp.zeros((1, qb), f32) for _ in range(HEADS)))
        _, ls = run_chunks(attend_chunk, init)
        row8 = lax.broadcasted_iota(i32, (SUBLANES, qb), 0)
        for hd in range(HEADS):
            l_ref[hd] = jnp.where(row8 == 0, ls[hd], 0.0)

    out_t = jnp.concatenate(
        [acc_ref[hd] / jnp.sum(l_ref[hd], axis=0, keepdims=True) for hd in range(HEADS)],
        axis=0)
    o_ref[...] = out_t.T.astype(o_ref.dtype)


def _dsa_call(qit, wit, qt, ki, k4, vt, bsz, seq):
    n = bsz * seq
    qb = DSA_QB
    nq = seq // qb
    qcol4 = lambda b, i: (0, 0, b * nq + i)
    return pl.pallas_call(
        functools.partial(_dsa_kernel, seq=seq),
        grid=(bsz, nq),
        in_specs=[
            pl.BlockSpec((IDX_HEADS, IDX_DIM, qb), qcol4),
            pl.BlockSpec((SUBLANES, qb), lambda b, i: (0, b * nq + i)),
            pl.BlockSpec((HEADS, HEAD_DIM, qb), qcol4),
            pl.BlockSpec((seq, IDX_DIM), lambda b, i: (b, 0)),
            pl.BlockSpec((HEADS, seq, LANES), lambda b, i: (0, b, 0)),
            pl.BlockSpec((HEADS, HEAD_DIM, seq), lambda b, i: (0, 0, b)),
        ],
        out_specs=pl.BlockSpec((qb, GROUP), lambda b, i: (b * nq + i, 0)),
        out_shape=jax.ShapeDtypeStruct((n, GROUP), jnp.bfloat16),
        scratch_shapes=[
            pltpu.VMEM((seq, qb), jnp.float32),
            pltpu.VMEM((seq, qb), jnp.bfloat16),
            pltpu.VMEM((HEADS, HEAD_DIM, qb), jnp.float32),
            pltpu.VMEM((HEADS, DSA_KC, qb), jnp.float32),
            pltpu.VMEM((HEADS, DSA_KC, qb), jnp.bfloat16),
            pltpu.VMEM((HEADS, LANES, qb), jnp.bfloat16),
            pltpu.VMEM((HEADS, SUBLANES, qb), jnp.float32),
            pltpu.VMEM((HEADS, SUBLANES, LANES), jnp.float32),
            pltpu.VMEM((SUBLANES, qb), jnp.int32),
        ],
        compiler_params=pltpu.CompilerParams(
            dimension_semantics=("arbitrary", "arbitrary"),
            vmem_limit_bytes=VMEM_LIMIT),
        name="dsa",
    )(qit, wit, qt, ki, k4, vt)


def _out_ffn_kernel(x_ref, yab_ref, yc_ref, yd_ref, wout_ref, gffn_ref, wg_ref, wu_ref,
                    wd_ref, gfin_ref, o_ref, *, final):
    f32, bf16 = jnp.float32, jnp.bfloat16
    y = jnp.concatenate([yab_ref[...], yc_ref[...], yd_ref[...]], axis=1)
    x1 = x_ref[...] + jnp.dot(y, wout_ref[...], preferred_element_type=f32)
    hf = _rmsnorm(x1, gffn_ref[...]).astype(bf16)
    gate = jnp.dot(hf, wg_ref[...], preferred_element_type=f32)
    up = jnp.dot(hf, wu_ref[...], preferred_element_type=f32)
    act = (gate * jax.nn.sigmoid(gate) * up).astype(bf16)
    out = x1 + jnp.dot(act, wd_ref[...], preferred_element_type=f32)
    if final:
        out = _rmsnorm(out, gfin_ref[...])
    o_ref[...] = out


def _out_ffn_call(x2d, yab, yc, yd, wout, gffn, wg, wu, wd, gfin, layer, final):
    n = x2d.shape[0]
    tm = FFN_TM
    row = lambda i: (i, 0)
    const = lambda i: (0, 0)
    resident = lambda shape: pl.BlockSpec((None,) + shape, lambda i: (layer, 0, 0),
                                          pipeline_mode=pl.Buffered(1))
    return pl.pallas_call(
        functools.partial(_out_ffn_kernel, final=final),
        grid=(n // tm,),
        in_specs=[
            pl.BlockSpec((tm, D_MODEL), row),
            pl.BlockSpec((tm, 2 * GROUP), row),
            pl.BlockSpec((tm, GROUP), row),
            pl.BlockSpec((tm, GROUP), row),
            resident((D_MODEL, D_MODEL)),
            pl.BlockSpec((1, D_MODEL), const),
            resident((D_MODEL, FFN_HIDDEN)),
            resident((D_MODEL, FFN_HIDDEN)),
            resident((FFN_HIDDEN, D_MODEL)),
            pl.BlockSpec((1, D_MODEL), const),
        ],
        out_specs=pl.BlockSpec((tm, D_MODEL), row),
        out_shape=jax.ShapeDtypeStruct((n, D_MODEL), jnp.float32),
        compiler_params=pltpu.CompilerParams(
            dimension_semantics=("arbitrary",),
            vmem_limit_bytes=VMEM_LIMIT),
        name="out_ffn",
    )(x2d, yab, yc, yd, wout, gffn, wg, wu, wd, gfin)


def _pack_w_in(w):
    g = GROUP
    cuts = [0, 3 * g, 5 * g, 6 * g, 7 * g, 8 * g, 9 * g, 9 * g + IDX_DIM,
            9 * g + IDX_DIM + IDX_HEADS, 11 * g + IDX_DIM + IDX_HEADS]
    a, b, q, k, v, qi, ki, wi, d = [w[:, cuts[t]:cuts[t + 1]] for t in range(9)]
    z = lambda cols: jnp.zeros((w.shape[0], cols), w.dtype)
    rows = jnp.concatenate([a, b, k, ki, z(LANES - IDX_DIM), d], axis=1)
    trans = jnp.concatenate([q, v, qi, wi, z(WI_ROWS - IDX_HEADS)], axis=1).T
    return rows.astype(jnp.bfloat16), trans.astype(jnp.bfloat16)


def _rope_tables(seq):
    inv_freq = ROPE_THETA ** (-jnp.arange(0, HEAD_DIM, 2, dtype=jnp.float32) / HEAD_DIM)
    ang = jnp.arange(seq, dtype=jnp.float32)[:, None] * inv_freq[None, :]
    c, s = jnp.cos(ang), jnp.sin(ang)
    reps = LANES // HEAD_DIM
    cos = jnp.tile(jnp.concatenate([c, c], axis=1), (1, reps))
    sin = jnp.tile(jnp.concatenate([-s, s], axis=1), (1, reps))
    return cos, sin, c.T, s.T


def kernel(x, g_mix, w_in, w_conv_a, gmlp_ln_g, gmlp_ln_b, w_s, b_s, w_conf, b_conf,
           conf_ln_g, conf_ln_b, w_out, g_ffn, w_gate, w_up, w_down, g_final):
    bsz, seq, d = x.shape
    depth = w_in.shape[0]
    bf16 = jnp.bfloat16
    tables = _rope_tables(seq)
    w_out_b, w_gate_b, w_up_b, w_down_b = (
        w.astype(bf16) for w in (w_out, w_gate, w_up, w_down))
    xs = x
    for l in range(depth):
        w_rows, w_t = _pack_w_in(w_in[l])
        yab, yd, k4, ki, qt, vt, qit, wit = _mix_call(
            xs.reshape(bsz, seq, d), g_mix[l][None, :], w_rows, w_t, tables,
            w_conv_a[l], gmlp_ln_g[l][None, :], gmlp_ln_b[l][None, :], w_s[l],
            jnp.repeat(b_s[l].T, HEAD_DIM, axis=1), w_conf[l], b_conf[l][None, :],
            conf_ln_g[l][None, :], conf_ln_b[l][None, :])
        yc = _dsa_call(qit, wit, qt, ki, k4, vt, bsz, seq)
        xs = _out_ffn_call(
            xs.reshape(bsz * seq, d), yab, yc, yd, w_out_b, g_ffn[l][None, :],
            w_gate_b, w_up_b, w_down_b, g_final[None, :], layer=l, final=(l == depth - 1))
    return xs.reshape(bsz, seq, d)
```

```python
import functools
import math

import jax
import jax.numpy as jnp
import numpy as np
from jax import lax
from jax.experimental import pallas as pl
from jax.experimental.pallas import tpu as pltpu

D_MODEL = 1024
GROUP = 256
HEADS = 4
HEAD_DIM = 64
CHUNK = 128
IDX_HEADS = 4
IDX_DIM = 64
TOPK = 256
SHORT_W = 3
CONF_W = 31
FFN_HIDDEN = 2816
ROPE_THETA = 10000.0
NORM_EPS = 1e-6
LN_EPS = 1e-5
NEG = -1e30
NEG_BF16 = float(np.asarray(NEG, dtype=jnp.bfloat16).astype(np.float32))

LANES = 128
SUBLANES = 8
A_HALO = 8
D_HALO = 32

COL_A = 0
COL_B = COL_A + 3 * GROUP
COL_K = COL_B + 2 * GROUP
COL_KI = COL_K + GROUP
COL_D = COL_KI + LANES
ROW_COLS = COL_D + 2 * GROUP
TROW_Q = 0
TROW_V = TROW_Q + GROUP
TROW_QI = TROW_V + GROUP
TROW_WI = TROW_QI + IDX_HEADS * IDX_DIM
WI_ROWS = 16
TROWS = TROW_WI + WI_ROWS

MIX_TM = 1024
DSA_QB = 512
DSA_KC = 512
DSA_SUB = 64
CNT_ROWS = 16
CNTB_ROWS = 32
TIE_TILE = 128
BF16_ROWS = 16
SHIFT_LIMIT = 45.0
FFN_TM = 512
VMEM_LIMIT = 56 * 1024 * 1024

_NT = (((1,), (1,)), ((), ()))


def _rmsnorm(xf, g):
    ms = jnp.mean(xf * xf, axis=-1, keepdims=True)
    return xf * lax.rsqrt(ms + NORM_EPS) * g


def _layernorm(xf, g, b):
    mu = jnp.mean(xf, axis=-1, keepdims=True)
    xc = xf - mu
    var = jnp.mean(xc * xc, axis=-1, keepdims=True)
    return xc * lax.rsqrt(var + LN_EPS) * g + b


def _rope128(x, cos, sin_signed):
    half = HEAD_DIM // 2
    lane = lax.broadcasted_iota(jnp.int32, x.shape, 1) % HEAD_DIM
    up = pltpu.roll(x, LANES - half, axis=1)
    dn = pltpu.roll(x, half, axis=1)
    return x * cos + jnp.where(lane < half, up, dn) * sin_signed


def _rope_t(x, cos_t, sin_t):
    half = HEAD_DIM // 2
    x1, x2 = x[0:half, :], x[half:HEAD_DIM, :]
    return x1 * cos_t - x2 * sin_t, x2 * cos_t + x1 * sin_t


def _mix_kernel(x_ref, g_ref, w_ref, wt_ref, cos_ref, sin_ref, cost_ref, sint_ref,
                wca_ref, lngb_ref, lnbb_ref, ws_ref, bsb_ref, wcf_ref, bcf_ref,
                lngd_ref, lnbd_ref,
                yab_ref, yd_ref, k4_ref, ki_ref, qt_ref, vt_ref, qit_ref, wit_ref,
                abuf, dbuf, sbuf):
    tm = x_ref.shape[0]
    f32, bf16 = jnp.float32, jnp.bfloat16
    half = HEAD_DIM // 2

    @pl.when(pl.program_id(1) == 0)
    def _():
        abuf[0:A_HALO, :] = jnp.zeros((A_HALO, GROUP), f32)
        dbuf[0:D_HALO, :] = jnp.zeros((D_HALO, GROUP), f32)

    h = _rmsnorm(x_ref[...], g_ref[...]).astype(bf16)

    def proj(lo, hi):
        return jnp.dot(h, w_ref[:, lo:hi], preferred_element_type=f32)

    def proj_t(lo, hi):
        return lax.dot_general(wt_ref[lo:hi, :], h, _NT, preferred_element_type=f32)

    zd = proj(COL_D, ROW_COLS)
    dbuf[D_HALO:D_HALO + tm, :] = zd[:, 0:GROUP] * jax.nn.sigmoid(zd[:, GROUP:2 * GROUP])
    acc = jnp.broadcast_to(bcf_ref[...], (tm, GROUP))
    first = D_HALO - (CONF_W - 1)
    for r in range(SUBLANES):
        taps = [t for t in range(CONF_W) if (first + t) % SUBLANES == r]
        span = (first + taps[-1]) - (first + taps[0]) + tm
        sbuf[0:span, :] = dbuf[pl.ds(first + taps[0], span), :]
        for t in taps:
            lo = t - taps[0]
            acc = acc + sbuf[lo:lo + tm, :] * wcf_ref[t:t + 1, :]
    y = _layernorm(acc, lngd_ref[...], lnbd_ref[...])
    yd_ref[...] = (y * jax.nn.sigmoid(y)).astype(bf16)
    dbuf[0:D_HALO, :] = dbuf[tm:tm + D_HALO, :]

    za = proj(COL_A, COL_B)
    abuf[A_HALO:A_HALO + tm, :] = za[:, GROUP:2 * GROUP] * za[:, 2 * GROUP:3 * GROUP]
    conv = abuf[pl.ds(A_HALO - 2, tm), :] * wca_ref[0:1, :]
    conv = conv + abuf[pl.ds(A_HALO - 1, tm), :] * wca_ref[1:2, :]
    conv = conv + abuf[pl.ds(A_HALO, tm), :] * wca_ref[2:3, :]
    yab_ref[:, 0:GROUP] = (za[:, 0:GROUP] * conv).astype(bf16)
    abuf[0:A_HALO, :] = abuf[tm:tm + A_HALO, :]

    zb = proj(COL_B, COL_K)
    vn = _layernorm(zb[:, GROUP:2 * GROUP], lngb_ref[...], lnbb_ref[...]).astype(bf16)
    r_i = lax.broadcasted_iota(jnp.int32, (CHUNK, CHUNK), 0)
    c_i = lax.broadcasted_iota(jnp.int32, (CHUNK, CHUNK), 1)
    wcat = jnp.concatenate(
        [jnp.where(r_i >= c_i, ws_ref[hd], 0.0).astype(bf16) for hd in range(HEADS)], axis=1)
    lane_head = lax.broadcasted_iota(jnp.int32, (CHUNK, GROUP), 1) // HEAD_DIM
    zero_b = jnp.zeros((), bf16)
    for c in range(tm // CHUNK):
        rows = slice(c * CHUNK, (c + 1) * CHUNK)
        vc = vn[rows, :]
        stacked = jnp.concatenate(
            [jnp.where(lane_head == hd, vc, zero_b) for hd in range(HEADS)], axis=0)
        mixed = bsb_ref[...] + jnp.dot(wcat, stacked, preferred_element_type=f32)
        yab_ref[rows, GROUP:2 * GROUP] = (zb[rows, 0:GROUP] * mixed).astype(bf16)

    cos = cos_ref[...]
    sin = sin_ref[...]
    zk = proj(COL_K, COL_D)
    lane = lax.broadcasted_iota(jnp.int32, (tm, LANES), 1)
    bias_col = jnp.where(lane == HEAD_DIM, 1.0, 0.0)
    for pair in range(2):
        kp = _rope128(zk[:, pair * LANES:(pair + 1) * LANES], cos, sin)
        for sub in range(2):
            kh = kp if sub == 0 else pltpu.roll(kp, HEAD_DIM, axis=1)
            k4_ref[2 * pair + sub] = jnp.where(lane < HEAD_DIM, kh, bias_col).astype(bf16)
    kip = _rope128(zk[:, GROUP:GROUP + LANES], cos, sin)
    ki_ref[...] = kip[:, 0:IDX_DIM].astype(bf16)

    cos_t = cost_ref[...]
    sin_t = sint_ref[...]
    q_scale = HEAD_DIM ** -0.5 * math.log2(math.e)
    zq = proj_t(TROW_Q, TROW_V)
    zqi = proj_t(TROW_QI, TROW_WI)
    for hd in range(HEADS):
        rows = slice(hd * HEAD_DIM, (hd + 1) * HEAD_DIM)
        o1, o2 = _rope_t(zq[rows, :], cos_t, sin_t)
        qt_ref[hd, 0:half, :] = (o1 * q_scale).astype(bf16)
        qt_ref[hd, half:HEAD_DIM, :] = (o2 * q_scale).astype(bf16)
        o1, o2 = _rope_t(zqi[rows, :], cos_t, sin_t)
        qit_ref[hd, 0:half, :] = o1.astype(bf16)
        qit_ref[hd, half:HEAD_DIM, :] = o2.astype(bf16)
    zv = proj_t(TROW_V, TROW_QI)
    for hd in range(HEADS):
        vt_ref[hd] = zv[hd * HEAD_DIM:(hd + 1) * HEAD_DIM, :].astype(bf16)
    zw = proj_t(TROW_WI, TROWS)
    wit_ref[...] = zw[0:SUBLANES, :] * (IDX_HEADS ** -0.5 * IDX_DIM ** -0.5)


def _mix_call(x, g, w_rows, w_t, tables, wca, lngb, lnbb, ws, bsb, wcf, bcf, lngd, lnbd):
    bsz, seq, _ = x.shape
    n = bsz * seq
    tm = MIX_TM
    nt = seq // tm
    bf16 = jnp.bfloat16
    half = HEAD_DIM // 2
    cos, sin, cos_t, sin_t = tables
    row = lambda b, i: (b * nt + i, 0)
    row4 = lambda b, i: (0, b * nt + i, 0)
    col = lambda b, i: (0, b * nt + i)
    col4 = lambda b, i: (0, 0, b * nt + i)
    const2 = lambda b, i: (0, 0)
    const3 = lambda b, i: (0, 0, 0)
    in_specs = [
        pl.BlockSpec((None, tm, D_MODEL), lambda b, i: (b, i, 0)),
        pl.BlockSpec((1, D_MODEL), const2),
        pl.BlockSpec((D_MODEL, ROW_COLS), const2),
        pl.BlockSpec((TROWS, D_MODEL), const2),
        pl.BlockSpec((tm, LANES), lambda b, i: (i, 0)),
        pl.BlockSpec((tm, LANES), lambda b, i: (i, 0)),
        pl.BlockSpec((half, tm), lambda b, i: (0, i)),
        pl.BlockSpec((half, tm), lambda b, i: (0, i)),
        pl.BlockSpec((SHORT_W, GROUP), const2),
        pl.BlockSpec((1, GROUP), const2),
        pl.BlockSpec((1, GROUP), const2),
        pl.BlockSpec((HEADS, CHUNK, CHUNK), const3),
        pl.BlockSpec((CHUNK, GROUP), const2),
        pl.BlockSpec((CONF_W, GROUP), const2),
        pl.BlockSpec((1, GROUP), const2),
        pl.BlockSpec((1, GROUP), const2),
        pl.BlockSpec((1, GROUP), const2),
    ]
    out_shape = [
        jax.ShapeDtypeStruct((n, 2 * GROUP), bf16),
        jax.ShapeDtypeStruct((n, GROUP), bf16),
        jax.ShapeDtypeStruct((HEADS, n, LANES), bf16),
        jax.ShapeDtypeStruct((n, IDX_DIM), bf16),
        jax.ShapeDtypeStruct((HEADS, HEAD_DIM, n), bf16),
        jax.ShapeDtypeStruct((HEADS, HEAD_DIM, n), bf16),
        jax.ShapeDtypeStruct((IDX_HEADS, IDX_DIM, n), bf16),
        jax.ShapeDtypeStruct((SUBLANES, n), jnp.float32),
    ]
    out_specs = [
        pl.BlockSpec((tm, 2 * GROUP), row),
        pl.BlockSpec((tm, GROUP), row),
        pl.BlockSpec((HEADS, tm, LANES), row4),
        pl.BlockSpec((tm, IDX_DIM), row),
        pl.BlockSpec((HEADS, HEAD_DIM, tm), col4),
        pl.BlockSpec((HEADS, HEAD_DIM, tm), col4),
        pl.BlockSpec((IDX_HEADS, IDX_DIM, tm), col4),
        pl.BlockSpec((SUBLANES, tm), col),
    ]
    return pl.pallas_call(
        _mix_kernel,
        grid=(bsz, nt),
        in_specs=in_specs,
        out_specs=out_specs,
        out_shape=out_shape,
        scratch_shapes=[pltpu.VMEM((tm + A_HALO, GROUP), jnp.float32),
                        pltpu.VMEM((tm + D_HALO, GROUP), jnp.float32),
                        pltpu.VMEM((tm + D_HALO, GROUP), jnp.float32)],
        compiler_params=pltpu.CompilerParams(
            dimension_semantics=("arbitrary", "arbitrary"),
            vmem_limit_bytes=VMEM_LIMIT),
        name="mix",
    )(x, g, w_rows, w_t, cos, sin, cos_t, sin_t, wca, lngb, lnbb, ws, bsb, wcf, bcf,
      lngd, lnbd)


def _threshold_bits(trial):
    bits = jnp.where(trial < 0, trial & jnp.int32(0x7FFFFFFF), ~trial)
    return pltpu.bitcast(bits, jnp.float32)


def _dsa_kernel(qit_ref, wit_ref, qt_ref, ki_ref, k4_ref, vt_ref, o_ref,
                sc_ref, scb_ref, acc_ref, s_ref, p_ref, qa_ref, l_ref, kmax_ref, cnt_ref,
                *, seq):
    f32, bf16, i32 = jnp.float32, jnp.bfloat16, jnp.int32
    qb = wit_ref.shape[1]
    kc, sub = DSA_KC, DSA_SUB
    ndiag = qb // kc
    i = pl.program_id(1)
    nfull = i * ndiag
    nchunks = nfull + ndiag
    n_outside = seq - nchunks * kc
    key_i = lax.broadcasted_iota(i32, (sub, qb), 0)
    qry_i = lax.broadcasted_iota(i32, (sub, qb), 1)

    def causal(x, d, r, lo=0):
        if x.shape == key_i.shape:
            k_i, q_i = key_i, qry_i
        else:
            k_i = lax.broadcasted_iota(i32, x.shape, 0)
            q_i = lax.broadcasted_iota(i32, x.shape, 1)
        return jnp.where(k_i + (d * kc + r - lo) <= q_i, x, NEG)

    def first_lane(d, r):
        return 0 if d is None else ((d * kc + r) // LANES) * LANES

    def run_chunks(chunk_fn, carry):
        carry = lax.fori_loop(0, nfull, lambda c, cr: chunk_fn(c, None, cr), carry)
        for d in range(ndiag):
            carry = chunk_fn(nfull + d, d, carry)
        return carry

    def score_chunk(c, d, carry):
        off = pl.multiple_of(c * kc, kc)
        for t in range(kc // sub):
            r0 = off + t * sub
            kis = ki_ref[pl.ds(r0, sub), :]
            lo = first_lane(d, t * sub)
            sc = None
            for j in range(IDX_HEADS):
                rel = jnp.dot(kis, qit_ref[j, :, lo:qb], preferred_element_type=f32)
                term = jnp.maximum(rel, 0.0) * wit_ref[j:j + 1, lo:qb]
                sc = term if sc is None else sc + term
            if d is not None:
                sc = causal(sc, d, t * sub, lo)
            if lo:
                sc = jnp.concatenate([jnp.full((sub, lo), NEG, f32), sc], axis=1)
            sc_ref[pl.ds(r0, sub), :] = sc
            scb_ref[pl.ds(r0, sub), :] = sc.astype(bf16)
        return carry

    run_chunks(score_chunk, 0)

    assert ndiag == 1
    lane_tile = lax.broadcasted_iota(i32, (1, qb), 1) // LANES
    n_masked = n_outside + (kc - (lane_tile + 1) * LANES)

    def count_pass(ref, rows, zero, indicator):
        def update(acc, r0, lo):
            ind = indicator(ref[pl.ds(r0, rows), lo:qb], lo)
            if lo == 0:
                return acc + ind
            return jnp.concatenate([acc[:, :lo], acc[:, lo:] + ind], axis=1)

        def body(c, acc):
            off = pl.multiple_of(c * kc, kc)
            for t in range(kc // rows):
                acc = update(acc, off + t * rows, 0)
            return acc

        npair = nfull // 2
        acc = lax.fori_loop(0, npair, lambda c2, a: body(2 * c2 + 1, body(2 * c2, a)), zero)
        acc = lax.fori_loop(2 * npair, nfull, body, acc)
        doff = pl.multiple_of(nfull * kc, kc)
        for t in range(kc // rows):
            acc = update(acc, doff + t * rows, first_lane(0, t * rows))
        return acc

    def count(cmp, thr):
        acc = count_pass(sc_ref, CNT_ROWS, jnp.zeros((CNT_ROWS, qb), i32),
                         lambda x, lo: jnp.where(cmp(x, thr[:, lo:qb]), 1, 0))
        return jnp.sum(acc, axis=0, keepdims=True)

    def count_ge(thr):
        return count(lambda x, t: x >= t, thr) + jnp.where(thr <= NEG, n_masked, 0)

    neg_b = NEG_BF16
    one_b, zero_b = jnp.ones((), bf16), jnp.zeros((), bf16)

    def count_ge_b(thr):
        thr_b = thr.astype(bf16)
        acc = count_pass(scb_ref, CNTB_ROWS, jnp.zeros((CNTB_ROWS, qb), bf16),
                         lambda x, lo: jnp.where(x >= thr_b[:, lo:qb], one_b, zero_b))
        inside = jnp.sum(acc.astype(f32), axis=0, keepdims=True).astype(i32)
        return inside + jnp.where(thr_b.astype(f32) <= neg_b, n_masked, 0)

    def coarse_step(step, prefix):
        trial = prefix | jnp.left_shift(jnp.int32(1), 31 - step)
        cnt = count_ge_b(_threshold_bits(trial))
        return jnp.where(cnt >= TOPK, trial, prefix)

    coarse = lax.fori_loop(0, 16, coarse_step, jnp.zeros((1, qb), i32))

    start = coarse - jnp.int32(1 << 16)

    def fine_step(step, state):
        prefix, c_ge, hit = state
        trial = prefix + jnp.left_shift(jnp.int32(1), 16 - step)
        cnt = count_ge(_threshold_bits(trial))
        accept = cnt >= TOPK
        return (jnp.where(accept, trial, prefix), jnp.where(accept, cnt, c_ge),
                jnp.where(accept, 1, hit))

    prefix, c_ge, hit = lax.fori_loop(
        0, 17, fine_step, (start, jnp.full((1, qb), seq, i32), jnp.zeros((1, qb), i32)))
    vstar = _threshold_bits(prefix)

    has_ties = jnp.max(c_ge) > TOPK

    @pl.when(has_ties)
    def _():
        cnt_ref[...] = jnp.broadcast_to(c_ge, cnt_ref.shape)

        @pl.when(jnp.min(hit) == 0)
        def _():
            cnt_ref[...] = jnp.broadcast_to(count_ge(vstar), cnt_ref.shape)

        excess = (cnt_ref[0:1, :] - TOPK).astype(f32)
        tt = TIE_TILE
        tri = jnp.where(lax.broadcasted_iota(i32, (tt, tt), 0)
                        <= lax.broadcasted_iota(i32, (tt, tt), 1), 1.0, 0.0).astype(bf16)

        def tie_chunk(c, d, after):
            off = pl.multiple_of(c * kc, kc)
            trailing = {}
            for t in reversed(range(kc // tt)):
                s = sc_ref[pl.ds(off + t * tt, tt), :]
                tie = jnp.where(s == vstar, 1.0, 0.0).astype(bf16)
                trailing[t] = jnp.dot(tri, tie, preferred_element_type=f32)
            for t in reversed(range(kc // tt)):
                s = sc_ref[pl.ds(off + t * tt, tt), :]
                admitted = jnp.where(after + trailing[t] > excess, jnp.inf, NEG)
                cap = jnp.where(s > vstar, jnp.inf, jnp.where(s == vstar, admitted, NEG))
                if d is not None:
                    cap = causal(cap, d, t * tt)
                sc_ref[pl.ds(off + t * tt, tt), :] = cap
                after = after + trailing[t][0:1, :]
            return after

        after = jnp.zeros((1, qb), f32)
        for d in reversed(range(ndiag)):
            after = tie_chunk(nfull + d, d, after)
        lax.fori_loop(0, nfull, lambda j, a: tie_chunk(nfull - 1 - j, None, a), after)

    def fold8(x, op):
        return op(x.reshape(sub // SUBLANES, SUBLANES, x.shape[1]), axis=0)

    @pl.when(jnp.logical_not(has_ties))
    def _():
        def cap_chunk(c, d, carry):
            off = pl.multiple_of(c * kc, kc)
            for t in range(kc // sub):
                r = t * sub
                cap = jnp.where(sc_ref[pl.ds(off + r, sub), :] >= vstar, jnp.inf, NEG)
                if d is not None:
                    cap = causal(cap, d, r)
                sc_ref[pl.ds(off + r, sub), :] = cap
            return carry

        run_chunks(cap_chunk, 0)

    acc_ref[...] = jnp.zeros(acc_ref.shape, f32)

    @pl.when(i == 0)
    def _():
        kmax_ref[...] = jnp.zeros(kmax_ref.shape, f32)

    bounds = []
    for hd in range(HEADS):
        kk = k4_ref[hd, pl.ds(pl.multiple_of(i * qb, qb), qb), :].astype(f32)
        k2 = jnp.max(jnp.sum(kk * kk, axis=1, keepdims=True), axis=0, keepdims=True)
        kmax_ref[hd] = jnp.maximum(kmax_ref[hd], k2)
        qf = qt_ref[hd].astype(f32)
        q2 = jnp.sum(qf * qf, axis=0, keepdims=True)
        bounds.append(jnp.sqrt(q2 * kmax_ref[hd][0:1, 0:1]))
    worst = bounds[0]
    for hd in range(1, HEADS):
        worst = jnp.maximum(worst, bounds[hd])
    fast = jnp.max(worst) <= SHIFT_LIMIT

    row16 = lax.broadcasted_iota(i32, (BF16_ROWS, qb), 0)
    for hd in range(HEADS):
        qa_ref[hd, 0:HEAD_DIM, :] = qt_ref[hd]
        shift = jnp.where(fast, -bounds[hd], 0.0)
        qa_ref[hd, HEAD_DIM:HEAD_DIM + BF16_ROWS, :] = jnp.where(
            row16 == 0, shift, 0.0).astype(bf16)
        qa_ref[hd, HEAD_DIM + BF16_ROWS:LANES, :] = jnp.zeros(
            (LANES - HEAD_DIM - BF16_ROWS, qb), bf16)

    def capped_logits(hd, off, r):
        s = jnp.dot(k4_ref[hd, pl.ds(off + r, sub), :], qa_ref[hd], preferred_element_type=f32)
        return jnp.minimum(s, sc_ref[pl.ds(off + r, sub), :])

    @pl.when(fast)
    def _():
        def chunk(c, d, lparts):
            off = pl.multiple_of(c * kc, kc)
            lparts = list(lparts)
            for hd in range(HEADS):
                for t in range(kc // sub):
                    r = t * sub
                    lo = first_lane(d, r)
                    s = jnp.dot(k4_ref[hd, pl.ds(off + r, sub), :], qa_ref[hd, :, lo:qb],
                                preferred_element_type=f32)
                    p = jnp.exp2(jnp.minimum(s, sc_ref[pl.ds(off + r, sub), lo:qb]))
                    l8, pb = fold8(p, jnp.sum), p.astype(bf16)
                    if lo:
                        l8 = jnp.concatenate([jnp.zeros((SUBLANES, lo), f32), l8], axis=1)
                        pb = jnp.concatenate([jnp.zeros((sub, lo), bf16), pb], axis=1)
                    lparts[hd] = lparts[hd] + l8
                    p_ref[hd, r:r + sub, :] = pb
            for hd in range(HEADS):
                acc_ref[hd] += jnp.dot(vt_ref[hd, :, pl.ds(off, kc)], p_ref[hd],
                                       preferred_element_type=f32)
            return tuple(lparts)

        lparts = run_chunks(chunk, tuple(jnp.zeros((SUBLANES, qb), f32) for _ in range(HEADS)))
        for hd in range(HEADS):
            l_ref[hd] = lparts[hd]

    @pl.when(jnp.logical_not(fast))
    def _():
        def attend_chunk(c, d, carry):
            ms, ls = carry
            off = pl.multiple_of(c * kc, kc)
            new_ms, new_ls = list(ms), list(ls)

            def logits_stage(hd):
                mpart = None
                for t in range(kc // sub):
                    r = t * sub
                    s = capped_logits(hd, off, r)
                    s_ref[hd, r:r + sub, :] = s
                    m8 = fold8(s, jnp.max)
                    mpart = m8 if mpart is None else jnp.maximum(mpart, m8)
                new_ms[hd] = jnp.maximum(ms[hd], jnp.max(mpart, axis=0, keepdims=True))

            def value_stage(hd):
                lpart = None
                for t in range(kc // sub):
                    r = t * sub
                    p = jnp.exp2(s_ref[hd, r:r + sub, :] - new_ms[hd])
                    l8 = fold8(p, jnp.sum)
                    lpart = l8 if lpart is None else lpart + l8
                    p_ref[hd, r:r + sub, :] = p.astype(bf16)
                alpha = jnp.exp2(ms[hd] - new_ms[hd])
                new_ls[hd] = alpha * ls[hd] + jnp.sum(lpart, axis=0, keepdims=True)
                pv = jnp.dot(vt_ref[hd, :, pl.ds(off, kc)], p_ref[hd],
                             preferred_element_type=f32)
                acc_ref[hd] = acc_ref[hd] * alpha + pv

            logits_stage(0)
            for hd in range(HEADS):
                if hd + 1 < HEADS:
                    logits_stage(hd + 1)
                value_stage(hd)
            return tuple(new_ms), tuple(new_ls)

        init = (tuple(jnp.full((1, qb), NEG, f32) for _ in range(HEADS)),
                tuple(jnp.zeros((1, qb), f32) for _ in range(HEADS)))
        _, ls = run_chunks(attend_chunk, init)
        row8 = lax.broadcasted_iota(i32, (SUBLANES, qb), 0)
        for hd in range(HEADS):
            l_ref[hd] = jnp.where(row8 == 0, ls[hd], 0.0)

    out_t = jnp.concatenate(
        [acc_ref[hd] / jnp.sum(l_ref[hd], axis=0, keepdims=True) for hd in range(HEADS)],
        axis=0)
    o_ref[...] = out_t.T.astype(o_ref.dtype)


def _dsa_call(qit, wit, qt, ki, k4, vt, bsz, seq):
    n = bsz * seq
    qb = DSA_QB
    nq = seq // qb
    qcol4 = lambda b, i: (0, 0, b * nq + i)
    return pl.pallas_call(
        functools.partial(_dsa_kernel, seq=seq),
        grid=(bsz, nq),
        in_specs=[
            pl.BlockSpec((IDX_HEADS, IDX_DIM, qb), qcol4),
            pl.BlockSpec((SUBLANES, qb), lambda b, i: (0, b * nq + i)),
            pl.BlockSpec((HEADS, HEAD_DIM, qb), qcol4),
            pl.BlockSpec((seq, IDX_DIM), lambda b, i: (b, 0)),
            pl.BlockSpec((HEADS, seq, LANES), lambda b, i: (0, b, 0)),
            pl.BlockSpec((HEADS, HEAD_DIM, seq), lambda b, i: (0, 0, b)),
        ],
        out_specs=pl.BlockSpec((qb, GROUP), lambda b, i: (b * nq + i, 0)),
        out_shape=jax.ShapeDtypeStruct((n, GROUP), jnp.bfloat16),
        scratch_shapes=[
            pltpu.VMEM((seq, qb), jnp.float32),
            pltpu.VMEM((seq, qb), jnp.bfloat16),
            pltpu.VMEM((HEADS, HEAD_DIM, qb), jnp.float32),
            pltpu.VMEM((HEADS, DSA_KC, qb), jnp.float32),
            pltpu.VMEM((HEADS, DSA_KC, qb), jnp.bfloat16),
            pltpu.VMEM((HEADS, LANES, qb), jnp.bfloat16),
            pltpu.VMEM((HEADS, SUBLANES, qb), jnp.float32),
            pltpu.VMEM((HEADS, SUBLANES, LANES), jnp.float32),
            pltpu.VMEM((SUBLANES, qb), jnp.int32),
        ],
        compiler_params=pltpu.CompilerParams(
            dimension_semantics=("arbitrary", "arbitrary"),
            vmem_limit_bytes=VMEM_LIMIT),
        name="dsa",
    )(qit, wit, qt, ki, k4, vt)


def _out_ffn_kernel(x_ref, yab_ref, yc_ref, yd_ref, wout_ref, gffn_ref, wg_ref, wu_ref,
                    wd_ref, gfin_ref, o_ref, *, final):
    f32, bf16 = jnp.float32, jnp.bfloat16
    y = jnp.concatenate([yab_ref[...], yc_ref[...], yd_ref[...]], axis=1)
    x1 = x_ref[...] + jnp.dot(y, wout_ref[...], preferred_element_type=f32)
    hf = _rmsnorm(x1, gffn_ref[...]).astype(bf16)
    gate = jnp.dot(hf, wg_ref[...], preferred_element_type=f32)
    up = jnp.dot(hf, wu_ref[...], preferred_element_type=f32)
    act = (gate * jax.nn.sigmoid(gate) * up).astype(bf16)
    out = x1 + jnp.dot(act, wd_ref[...], preferred_element_type=f32)
    if final:
        out = _rmsnorm(out, gfin_ref[...])
    o_ref[...] = out


def _out_ffn_call(x2d, yab, yc, yd, wout, gffn, wg, wu, wd, gfin, layer, final):
    n = x2d.shape[0]
    tm = FFN_TM
    row = lambda i: (i, 0)
    const = lambda i: (0, 0)
    resident = lambda shape: pl.BlockSpec((None,) + shape, lambda i: (layer, 0, 0),
                                          pipeline_mode=pl.Buffered(1))
    return pl.pallas_call(
        functools.partial(_out_ffn_kernel, final=final),
        grid=(n // tm,),
        in_specs=[
            pl.BlockSpec((tm, D_MODEL), row),
            pl.BlockSpec((tm, 2 * GROUP), row),
            pl.BlockSpec((tm, GROUP), row),
            pl.BlockSpec((tm, GROUP), row),
            resident((D_MODEL, D_MODEL)),
            pl.BlockSpec((1, D_MODEL), const),
            resident((D_MODEL, FFN_HIDDEN)),
            resident((D_MODEL, FFN_HIDDEN)),
            resident((FFN_HIDDEN, D_MODEL)),
            pl.BlockSpec((1, D_MODEL), const),
        ],
        out_specs=pl.BlockSpec((tm, D_MODEL), row),
        out_shape=jax.ShapeDtypeStruct((n, D_MODEL), jnp.float32),
        compiler_params=pltpu.CompilerParams(
            dimension_semantics=("arbitrary",),
            vmem_limit_bytes=VMEM_LIMIT),
        name="out_ffn",
    )(x2d, yab, yc, yd, wout, gffn, wg, wu, wd, gfin)


def _pack_w_in(w):
    g = GROUP
    cuts = [0, 3 * g, 5 * g, 6 * g, 7 * g, 8 * g, 9 * g, 9 * g + IDX_DIM,
            9 * g + IDX_DIM + IDX_HEADS, 11 * g + IDX_DIM + IDX_HEADS]
    a, b, q, k, v, qi, ki, wi, d = [w[:, cuts[t]:cuts[t + 1]] for t in range(9)]
    z = lambda cols: jnp.zeros((w.shape[0], cols), w.dtype)
    rows = jnp.concatenate([a, b, k, ki, z(LANES - IDX_DIM), d], axis=1)
    trans = jnp.concatenate([q, v, qi, wi, z(WI_ROWS - IDX_HEADS)], axis=1).T
    return rows.astype(jnp.bfloat16), trans.astype(jnp.bfloat16)


def _rope_tables(seq):
    inv_freq = ROPE_THETA ** (-jnp.arange(0, HEAD_DIM, 2, dtype=jnp.float32) / HEAD_DIM)
    ang = jnp.arange(seq, dtype=jnp.float32)[:, None] * inv_freq[None, :]
    c, s = jnp.cos(ang), jnp.sin(ang)
    reps = LANES // HEAD_DIM
    cos = jnp.tile(jnp.concatenate([c, c], axis=1), (1, reps))
    sin = jnp.tile(jnp.concatenate([-s, s], axis=1), (1, reps))
    return cos, sin, c.T, s.T


def kernel(x, g_mix, w_in, w_conv_a, gmlp_ln_g, gmlp_ln_b, w_s, b_s, w_conf, b_conf,
           conf_ln_g, conf_ln_b, w_out, g_ffn, w_gate, w_up, w_down, g_final):
    bsz, seq, d = x.shape
    depth = w_in.shape[0]
    bf16 = jnp.bfloat16
    tables = _rope_tables(seq)
    w_out_b, w_gate_b, w_up_b, w_down_b = (
        w.astype(bf16) for w in (w_out, w_gate, w_up, w_down))
    xs = x
    for l in range(depth):
        w_rows, w_t = _pack_w_in(w_in[l])
        yab, yd, k4, ki, qt, vt, qit, wit = _mix_call(
            xs.reshape(bsz, seq, d), g_mix[l][None, :], w_rows, w_t, tables,
            w_conv_a[l], gmlp_ln_g[l][None, :], gmlp_ln_b[l][None, :], w_s[l],
            jnp.repeat(b_s[l].T, HEAD_DIM, axis=1), w_conf[l], b_conf[l][None, :],
            conf_ln_g[l][None, :], conf_ln_b[l][None, :])
        yc = _dsa_call(qit, wit, qt, ki, k4, vt, bsz, seq)
        xs = _out_ffn_call(
            xs.reshape(bsz * seq, d), yab, yc, yd, w_out_b, g_ffn[l][None, :],
            w_gate_b, w_up_b, w_down_b, g_final[None, :], layer=l, final=(l == depth - 1))
    return xs.reshape(bsz, seq, d)
```

```python
import functools
import math

import jax
import jax.numpy as jnp
import numpy as np
from jax import lax
from jax.experimental import pallas as pl
from jax.experimental.pallas import tpu as pltpu

D_MODEL = 1024
GROUP = 256
HEADS = 4
HEAD_DIM = 64
CHUNK = 128
IDX_HEADS = 4
IDX_DIM = 64
TOPK = 256
SHORT_W = 3
CONF_W = 31
FFN_HIDDEN = 2816
ROPE_THETA = 10000.0
NORM_EPS = 1e-6
LN_EPS = 1e-5
NEG = -1e30
NEG_BF16 = float(np.asarray(NEG, dtype=jnp.bfloat16).astype(np.float32))

LANES = 128
SUBLANES = 8
A_HALO = 8
D_HALO = 32

COL_A = 0
COL_B = COL_A + 3 * GROUP
COL_K = COL_B + 2 * GROUP
COL_KI = COL_K + GROUP
COL_D = COL_KI + LANES
ROW_COLS = COL_D + 2 * GROUP
TROW_Q = 0
TROW_V = TROW_Q + GROUP
TROW_QI = TROW_V + GROUP
TROW_WI = TROW_QI + IDX_HEADS * IDX_DIM
WI_ROWS = 16
TROWS = TROW_WI + WI_ROWS

MIX_TM = 1024
DSA_QB = 512
DSA_KC = 512
DSA_SUB = 64
CNT_ROWS = 32
CNTB_ROWS = 64
TIE_TILE = 128
BF16_ROWS = 16
SHIFT_LIMIT = 45.0
FFN_TM = 512
VMEM_LIMIT = 56 * 1024 * 1024

_NT = (((1,), (1,)), ((), ()))


def _rmsnorm(xf, g):
    ms = jnp.mean(xf * xf, axis=-1, keepdims=True)
    return xf * lax.rsqrt(ms + NORM_EPS) * g


def _layernorm(xf, g, b):
    mu = jnp.mean(xf, axis=-1, keepdims=True)
    xc = xf - mu
    var = jnp.mean(xc * xc, axis=-1, keepdims=True)
    return xc * lax.rsqrt(var + LN_EPS) * g + b


def _rope128(x, cos, sin_signed):
    half = HEAD_DIM // 2
    lane = lax.broadcasted_iota(jnp.int32, x.shape, 1) % HEAD_DIM
    up = pltpu.roll(x, LANES - half, axis=1)
    dn = pltpu.roll(x, half, axis=1)
    return x * cos + jnp.where(lane < half, up, dn) * sin_signed


def _rope_t(x, cos_t, sin_t):
    half = HEAD_DIM // 2
    x1, x2 = x[0:half, :], x[half:HEAD_DIM, :]
    return x1 * cos_t - x2 * sin_t, x2 * cos_t + x1 * sin_t


def _mix_kernel(x_ref, g_ref, w_ref, wt_ref, cos_ref, sin_ref, cost_ref, sint_ref,
                wca_ref, lngb_ref, lnbb_ref, ws_ref, bsb_ref, wcf_ref, bcf_ref,
                lngd_ref, lnbd_ref,
                yab_ref, yd_ref, k4_ref, ki_ref, qt_ref, vt_ref, qit_ref, wit_ref,
                abuf, dbuf, sbuf):
    tm = x_ref.shape[0]
    f32, bf16 = jnp.float32, jnp.bfloat16
    half = HEAD_DIM // 2

    @pl.when(pl.program_id(1) == 0)
    def _():
        abuf[0:A_HALO, :] = jnp.zeros((A_HALO, GROUP), f32)
        dbuf[0:D_HALO, :] = jnp.zeros((D_HALO, GROUP), f32)

    h = _rmsnorm(x_ref[...], g_ref[...]).astype(bf16)

    def proj(lo, hi):
        return jnp.dot(h, w_ref[:, lo:hi], preferred_element_type=f32)

    def proj_t(lo, hi):
        return lax.dot_general(wt_ref[lo:hi, :], h, _NT, preferred_element_type=f32)

    zd = proj(COL_D, ROW_COLS)
    dbuf[D_HALO:D_HALO + tm, :] = zd[:, 0:GROUP] * jax.nn.sigmoid(zd[:, GROUP:2 * GROUP])
    acc = jnp.broadcast_to(bcf_ref[...], (tm, GROUP))
    first = D_HALO - (CONF_W - 1)
    for r in range(SUBLANES):
        taps = [t for t in range(CONF_W) if (first + t) % SUBLANES == r]
        span = (first + taps[-1]) - (first + taps[0]) + tm
        sbuf[0:span, :] = dbuf[pl.ds(first + taps[0], span), :]
        for t in taps:
            lo = t - taps[0]
            acc = acc + sbuf[lo:lo + tm, :] * wcf_ref[t:t + 1, :]
    y = _layernorm(acc, lngd_ref[...], lnbd_ref[...])
    yd_ref[...] = (y * jax.nn.sigmoid(y)).astype(bf16)
    dbuf[0:D_HALO, :] = dbuf[tm:tm + D_HALO, :]

    za = proj(COL_A, COL_B)
    abuf[A_HALO:A_HALO + tm, :] = za[:, GROUP:2 * GROUP] * za[:, 2 * GROUP:3 * GROUP]
    conv = abuf[pl.ds(A_HALO - 2, tm), :] * wca_ref[0:1, :]
    conv = conv + abuf[pl.ds(A_HALO - 1, tm), :] * wca_ref[1:2, :]
    conv = conv + abuf[pl.ds(A_HALO, tm), :] * wca_ref[2:3, :]
    yab_ref[:, 0:GROUP] = (za[:, 0:GROUP] * conv).astype(bf16)
    abuf[0:A_HALO, :] = abuf[tm:tm + A_HALO, :]

    zb = proj(COL_B, COL_K)
    vn = _layernorm(zb[:, GROUP:2 * GROUP], lngb_ref[...], lnbb_ref[...]).astype(bf16)
    r_i = lax.broadcasted_iota(jnp.int32, (CHUNK, CHUNK), 0)
    c_i = lax.broadcasted_iota(jnp.int32, (CHUNK, CHUNK), 1)
    wcat = jnp.concatenate(
        [jnp.where(r_i >= c_i, ws_ref[hd], 0.0).astype(bf16) for hd in range(HEADS)], axis=1)
    lane_head = lax.broadcasted_iota(jnp.int32, (CHUNK, GROUP), 1) // HEAD_DIM
    zero_b = jnp.zeros((), bf16)
    for c in range(tm // CHUNK):
        rows = slice(c * CHUNK, (c + 1) * CHUNK)
        vc = vn[rows, :]
        stacked = jnp.concatenate(
            [jnp.where(lane_head == hd, vc, zero_b) for hd in range(HEADS)], axis=0)
        mixed = bsb_ref[...] + jnp.dot(wcat, stacked, preferred_element_type=f32)
        yab_ref[rows, GROUP:2 * GROUP] = (zb[rows, 0:GROUP] * mixed).astype(bf16)

    cos = cos_ref[...]
    sin = sin_ref[...]
    zk = proj(COL_K, COL_D)
    lane = lax.broadcasted_iota(jnp.int32, (tm, LANES), 1)
    bias_col = jnp.where(lane == HEAD_DIM, 1.0, 0.0)
    for pair in range(2):
        kp = _rope128(zk[:, pair * LANES:(pair + 1) * LANES], cos, sin)
        for sub in range(2):
            kh = kp if sub == 0 else pltpu.roll(kp, HEAD_DIM, axis=1)
            k4_ref[2 * pair + sub] = jnp.where(lane < HEAD_DIM, kh, bias_col).astype(bf16)
    kip = _rope128(zk[:, GROUP:GROUP + LANES], cos, sin)
    ki_ref[...] = kip[:, 0:IDX_DIM].astype(bf16)

    cos_t = cost_ref[...]
    sin_t = sint_ref[...]
    q_scale = HEAD_DIM ** -0.5 * math.log2(math.e)
    zq = proj_t(TROW_Q, TROW_V)
    zqi = proj_t(TROW_QI, TROW_WI)
    for hd in range(HEADS):
        rows = slice(hd * HEAD_DIM, (hd + 1) * HEAD_DIM)
        o1, o2 = _rope_t(zq[rows, :], cos_t, sin_t)
        qt_ref[hd, 0:half, :] = (o1 * q_scale).astype(bf16)
        qt_ref[hd, half:HEAD_DIM, :] = (o2 * q_scale).astype(bf16)
        o1, o2 = _rope_t(zqi[rows, :], cos_t, sin_t)
        qit_ref[hd, 0:half, :] = o1.astype(bf16)
        qit_ref[hd, half:HEAD_DIM, :] = o2.astype(bf16)
    zv = proj_t(TROW_V, TROW_QI)
    for hd in range(HEADS):
        vt_ref[hd] = zv[hd * HEAD_DIM:(hd + 1) * HEAD_DIM, :].astype(bf16)
    zw = proj_t(TROW_WI, TROWS)
    wit_ref[...] = zw[0:SUBLANES, :] * (IDX_HEADS ** -0.5 * IDX_DIM ** -0.5)


def _mix_call(x, g, w_rows, w_t, tables, wca, lngb, lnbb, ws, bsb, wcf, bcf, lngd, lnbd):
    bsz, seq, _ = x.shape
    n = bsz * seq
    tm = MIX_TM
    nt = seq // tm
    bf16 = jnp.bfloat16
    half = HEAD_DIM // 2
    cos, sin, cos_t, sin_t = tables
    row = lambda b, i: (b * nt + i, 0)
    row4 = lambda b, i: (0, b * nt + i, 0)
    col = lambda b, i: (0, b * nt + i)
    col4 = lambda b, i: (0, 0, b * nt + i)
    const2 = lambda b, i: (0, 0)
    const3 = lambda b, i: (0, 0, 0)
    in_specs = [
        pl.BlockSpec((None, tm, D_MODEL), lambda b, i: (b, i, 0)),
        pl.BlockSpec((1, D_MODEL), const2),
        pl.BlockSpec((D_MODEL, ROW_COLS), const2),
        pl.BlockSpec((TROWS, D_MODEL), const2),
        pl.BlockSpec((tm, LANES), lambda b, i: (i, 0)),
        pl.BlockSpec((tm, LANES), lambda b, i: (i, 0)),
        pl.BlockSpec((half, tm), lambda b, i: (0, i)),
        pl.BlockSpec((half, tm), lambda b, i: (0, i)),
        pl.BlockSpec((SHORT_W, GROUP), const2),
        pl.BlockSpec((1, GROUP), const2),
        pl.BlockSpec((1, GROUP), const2),
        pl.BlockSpec((HEADS, CHUNK, CHUNK), const3),
        pl.BlockSpec((CHUNK, GROUP), const2),
        pl.BlockSpec((CONF_W, GROUP), const2),
        pl.BlockSpec((1, GROUP), const2),
        pl.BlockSpec((1, GROUP), const2),
        pl.BlockSpec((1, GROUP), const2),
    ]
    out_shape = [
        jax.ShapeDtypeStruct((n, 2 * GROUP), bf16),
        jax.ShapeDtypeStruct((n, GROUP), bf16),
        jax.ShapeDtypeStruct((HEADS, n, LANES), bf16),
        jax.ShapeDtypeStruct((n, IDX_DIM), bf16),
        jax.ShapeDtypeStruct((HEADS, HEAD_DIM, n), bf16),
        jax.ShapeDtypeStruct((HEADS, HEAD_DIM, n), bf16),
        jax.ShapeDtypeStruct((IDX_HEADS, IDX_DIM, n), bf16),
        jax.ShapeDtypeStruct((SUBLANES, n), jnp.float32),
    ]
    out_specs = [
        pl.BlockSpec((tm, 2 * GROUP), row),
        pl.BlockSpec((tm, GROUP), row),
        pl.BlockSpec((HEADS, tm, LANES), row4),
        pl.BlockSpec((tm, IDX_DIM), row),
        pl.BlockSpec((HEADS, HEAD_DIM, tm), col4),
        pl.BlockSpec((HEADS, HEAD_DIM, tm), col4),
        pl.BlockSpec((IDX_HEADS, IDX_DIM, tm), col4),
        pl.BlockSpec((SUBLANES, tm), col),
    ]
    return pl.pallas_call(
        _mix_kernel,
        grid=(bsz, nt),
        in_specs=in_specs,
        out_specs=out_specs,
        out_shape=out_shape,
        scratch_shapes=[pltpu.VMEM((tm + A_HALO, GROUP), jnp.float32),
                        pltpu.VMEM((tm + D_HALO, GROUP), jnp.float32),
                        pltpu.VMEM((tm + D_HALO, GROUP), jnp.float32)],
        compiler_params=pltpu.CompilerParams(
            dimension_semantics=("arbitrary", "arbitrary"),
            vmem_limit_bytes=VMEM_LIMIT),
        name="mix",
    )(x, g, w_rows, w_t, cos, sin, cos_t, sin_t, wca, lngb, lnbb, ws, bsb, wcf, bcf,
      lngd, lnbd)


def _threshold_bits(trial):
    bits = jnp.where(trial < 0, trial & jnp.int32(0x7FFFFFFF), ~trial)
    return pltpu.bitcast(bits, jnp.float32)


def _dsa_kernel(qit_ref, wit_ref, qt_ref, ki_ref, k4_ref, vt_ref, o_ref,
                sc_ref, scb_ref, acc_ref, s_ref, p_ref, qa_ref, l_ref, kmax_ref, cnt_ref,
                *, seq):
    f32, bf16, i32 = jnp.float32, jnp.bfloat16, jnp.int32
    qb = wit_ref.shape[1]
    kc, sub = DSA_KC, DSA_SUB
    ndiag = qb // kc
    i = pl.program_id(1)
    nfull = i * ndiag
    nchunks = nfull + ndiag
    n_outside = seq - nchunks * kc
    key_i = lax.broadcasted_iota(i32, (sub, qb), 0)
    qry_i = lax.broadcasted_iota(i32, (sub, qb), 1)

    def causal(x, d, r, lo=0):
        if x.shape == key_i.shape:
            k_i, q_i = key_i, qry_i
        else:
            k_i = lax.broadcasted_iota(i32, x.shape, 0)
            q_i = lax.broadcasted_iota(i32, x.shape, 1)
        return jnp.where(k_i + (d * kc + r - lo) <= q_i, x, NEG)

    def first_lane(d, r):
        return 0 if d is None else ((d * kc + r) // LANES) * LANES

    def run_chunks(chunk_fn, carry):
        carry = lax.fori_loop(0, nfull, lambda c, cr: chunk_fn(c, None, cr), carry)
        for d in range(ndiag):
            carry = chunk_fn(nfull + d, d, carry)
        return carry

    def score_chunk(c, d, carry):
        off = pl.multiple_of(c * kc, kc)
        for t in range(kc // sub):
            r0 = off + t * sub
            kis = ki_ref[pl.ds(r0, sub), :]
            lo = first_lane(d, t * sub)
            sc = None
            for j in range(IDX_HEADS):
                rel = jnp.dot(kis, qit_ref[j, :, lo:qb], preferred_element_type=f32)
                term = jnp.maximum(rel, 0.0) * wit_ref[j:j + 1, lo:qb]
                sc = term if sc is None else sc + term
            if d is not None:
                sc = causal(sc, d, t * sub, lo)
            if lo:
                sc = jnp.concatenate([jnp.full((sub, lo), NEG, f32), sc], axis=1)
            sc_ref[pl.ds(r0, sub), :] = sc
            scb_ref[pl.ds(r0, sub), :] = sc.astype(bf16)
        return carry

    run_chunks(score_chunk, 0)

    assert ndiag == 1
    lane_tile = lax.broadcasted_iota(i32, (1, qb), 1) // LANES
    n_masked = n_outside + (kc - (lane_tile + 1) * LANES)

    def count_pass(ref, rows, zero, indicator):
        def update(acc, r0, lo):
            ind = indicator(ref[pl.ds(r0, rows), lo:qb], lo)
            if lo == 0:
                return acc + ind
            return jnp.concatenate([acc[:, :lo], acc[:, lo:] + ind], axis=1)

        def body(c, acc):
            off = pl.multiple_of(c * kc, kc)
            for t in range(kc // rows):
                acc = update(acc, off + t * rows, 0)
            return acc

        acc = lax.fori_loop(0, nfull, body, zero)
        doff = pl.multiple_of(nfull * kc, kc)
        for t in range(kc // rows):
            acc = update(acc, doff + t * rows, first_lane(0, t * rows))
        return acc

    def count(cmp, thr):
        acc = count_pass(sc_ref, CNT_ROWS, jnp.zeros((CNT_ROWS, qb), i32),
                         lambda x, lo: jnp.where(cmp(x, thr[:, lo:qb]), 1, 0))
        return jnp.sum(acc, axis=0, keepdims=True)

    def count_ge(thr):
        return count(lambda x, t: x >= t, thr) + jnp.where(thr <= NEG, n_masked, 0)

    neg_b = NEG_BF16
    one_b, zero_b = jnp.ones((), bf16), jnp.zeros((), bf16)

    def count_ge_b(thr):
        thr_b = thr.astype(bf16)
        acc = count_pass(scb_ref, CNTB_ROWS, jnp.zeros((CNTB_ROWS, qb), bf16),
                         lambda x, lo: jnp.where(x >= thr_b[:, lo:qb], one_b, zero_b))
        inside = jnp.sum(acc.astype(f32), axis=0, keepdims=True).astype(i32)
        return inside + jnp.where(thr_b.astype(f32) <= neg_b, n_masked, 0)

    def coarse_step(step, prefix):
        trial = prefix | jnp.left_shift(jnp.int32(1), 31 - step)
        cnt = count_ge_b(_threshold_bits(trial))
        return jnp.where(cnt >= TOPK, trial, prefix)

    coarse = lax.fori_loop(0, 16, coarse_step, jnp.zeros((1, qb), i32))

    start = coarse - jnp.int32(1 << 16)

    def fine_step(step, state):
        prefix, c_ge, hit = state
        trial = prefix + jnp.left_shift(jnp.int32(1), 16 - step)
        cnt = count_ge(_threshold_bits(trial))
        accept = cnt >= TOPK
        return (jnp.where(accept, trial, prefix), jnp.where(accept, cnt, c_ge),
                jnp.where(accept, 1, hit))

    prefix, c_ge, hit = lax.fori_loop(
        0, 17, fine_step, (start, jnp.full((1, qb), seq, i32), jnp.zeros((1, qb), i32)))
    vstar = _threshold_bits(prefix)

    has_ties = jnp.max(c_ge) > TOPK

    @pl.when(has_ties)
    def _():
        cnt_ref[...] = jnp.broadcast_to(c_ge, cnt_ref.shape)

        @pl.when(jnp.min(hit) == 0)
        def _():
            cnt_ref[...] = jnp.broadcast_to(count_ge(vstar), cnt_ref.shape)

        excess = (cnt_ref[0:1, :] - TOPK).astype(f32)
        tt = TIE_TILE
        tri = jnp.where(lax.broadcasted_iota(i32, (tt, tt), 0)
                        <= lax.broadcasted_iota(i32, (tt, tt), 1), 1.0, 0.0).astype(bf16)

        def tie_chunk(c, d, after):
            off = pl.multiple_of(c * kc, kc)
            trailing = {}
            for t in reversed(range(kc // tt)):
                s = sc_ref[pl.ds(off + t * tt, tt), :]
                tie = jnp.where(s == vstar, 1.0, 0.0).astype(bf16)
                trailing[t] = jnp.dot(tri, tie, preferred_element_type=f32)
            for t in reversed(range(kc // tt)):
                s = sc_ref[pl.ds(off + t * tt, tt), :]
                admitted = jnp.where(after + trailing[t] > excess, jnp.inf, NEG)
                cap = jnp.where(s > vstar, jnp.inf, jnp.where(s == vstar, admitted, NEG))
                if d is not None:
                    cap = causal(cap, d, t * tt)
                sc_ref[pl.ds(off + t * tt, tt), :] = cap
                after = after + trailing[t][0:1, :]
            return after

        after = jnp.zeros((1, qb), f32)
        for d in reversed(range(ndiag)):
            after = tie_chunk(nfull + d, d, after)
        lax.fori_loop(0, nfull, lambda j, a: tie_chunk(nfull - 1 - j, None, a), after)

    def fold8(x, op):
        return op(x.reshape(sub // SUBLANES, SUBLANES, x.shape[1]), axis=0)

    @pl.when(jnp.logical_not(has_ties))
    def _():
        def cap_chunk(c, d, carry):
            off = pl.multiple_of(c * kc, kc)
            for t in range(kc // sub):
                r = t * sub
                cap = jnp.where(sc_ref[pl.ds(off + r, sub), :] >= vstar, jnp.inf, NEG)
                if d is not None:
                    cap = causal(cap, d, r)
                sc_ref[pl.ds(off + r, sub), :] = cap
            return carry

        run_chunks(cap_chunk, 0)

    acc_ref[...] = jnp.zeros(acc_ref.shape, f32)

    @pl.when(i == 0)
    def _():
        kmax_ref[...] = jnp.zeros(kmax_ref.shape, f32)

    bounds = []
    for hd in range(HEADS):
        kk = k4_ref[hd, pl.ds(pl.multiple_of(i * qb, qb), qb), :].astype(f32)
        k2 = jnp.max(jnp.sum(kk * kk, axis=1, keepdims=True), axis=0, keepdims=True)
        kmax_ref[hd] = jnp.maximum(kmax_ref[hd], k2)
        qf = qt_ref[hd].astype(f32)
        q2 = jnp.sum(qf * qf, axis=0, keepdims=True)
        bounds.append(jnp.sqrt(q2 * kmax_ref[hd][0:1, 0:1]))
    worst = bounds[0]
    for hd in range(1, HEADS):
        worst = jnp.maximum(worst, bounds[hd])
    fast = jnp.max(worst) <= SHIFT_LIMIT

    row16 = lax.broadcasted_iota(i32, (BF16_ROWS, qb), 0)
    for hd in range(HEADS):
        qa_ref[hd, 0:HEAD_DIM, :] = qt_ref[hd]
        shift = jnp.where(fast, -bounds[hd], 0.0)
        qa_ref[hd, HEAD_DIM:HEAD_DIM + BF16_ROWS, :] = jnp.where(
            row16 == 0, shift, 0.0).astype(bf16)
        qa_ref[hd, HEAD_DIM + BF16_ROWS:LANES, :] = jnp.zeros(
            (LANES - HEAD_DIM - BF16_ROWS, qb), bf16)

    def capped_logits(hd, off, r):
        s = jnp.dot(k4_ref[hd, pl.ds(off + r, sub), :], qa_ref[hd], preferred_element_type=f32)
        return jnp.minimum(s, sc_ref[pl.ds(off + r, sub), :])

    @pl.when(fast)
    def _():
        def chunk(c, d, lparts):
            off = pl.multiple_of(c * kc, kc)
            lparts = list(lparts)
            for hd in range(HEADS):
                for t in range(kc // sub):
                    r = t * sub
                    lo = first_lane(d, r)
                    s = jnp.dot(k4_ref[hd, pl.ds(off + r, sub), :], qa_ref[hd, :, lo:qb],
                                preferred_element_type=f32)
                    p = jnp.exp2(jnp.minimum(s, sc_ref[pl.ds(off + r, sub), lo:qb]))
                    l8, pb = fold8(p, jnp.sum), p.astype(bf16)
                    if lo:
                        l8 = jnp.concatenate([jnp.zeros((SUBLANES, lo), f32), l8], axis=1)
                        pb = jnp.concatenate([jnp.zeros((sub, lo), bf16), pb], axis=1)
                    lparts[hd] = lparts[hd] + l8
                    p_ref[hd, r:r + sub, :] = pb
            for hd in range(HEADS):
                acc_ref[hd] += jnp.dot(vt_ref[hd, :, pl.ds(off, kc)], p_ref[hd],
                                       preferred_element_type=f32)
            return tuple(lparts)

        lparts = run_chunks(chunk, tuple(jnp.zeros((SUBLANES, qb), f32) for _ in range(HEADS)))
        for hd in range(HEADS):
            l_ref[hd] = lparts[hd]

    @pl.when(jnp.logical_not(fast))
    def _():
        def attend_chunk(c, d, carry):
            ms, ls = carry
            off = pl.multiple_of(c * kc, kc)
            new_ms, new_ls = list(ms), list(ls)

            def logits_stage(hd):
                mpart = None
                for t in range(kc // sub):
                    r = t * sub
                    s = capped_logits(hd, off, r)
                    s_ref[hd, r:r + sub, :] = s
                    m8 = fold8(s, jnp.max)
                    mpart = m8 if mpart is None else jnp.maximum(mpart, m8)
                new_ms[hd] = jnp.maximum(ms[hd], jnp.max(mpart, axis=0, keepdims=True))

            def value_stage(hd):
                lpart = None
                for t in range(kc // sub):
                    r = t * sub
                    p = jnp.exp2(s_ref[hd, r:r + sub, :] - new_ms[hd])
                    l8 = fold8(p, jnp.sum)
                    lpart = l8 if lpart is None else lpart + l8
                    p_ref[hd, r:r + sub, :] = p.astype(bf16)
                alpha = jnp.exp2(ms[hd] - new_ms[hd])
                new_ls[hd] = alpha * ls[hd] + jnp.sum(lpart, axis=0, keepdims=True)
                pv = jnp.dot(vt_ref[hd, :, pl.ds(off, kc)], p_ref[hd],
                             preferred_element_type=f32)
                acc_ref[hd] = acc_ref[hd] * alpha + pv

            logits_stage(0)
            for hd in range(HEADS):
                if hd + 1 < HEADS:
                    logits_stage(hd + 1)
                value_stage(hd)
            return tuple(new_ms), tuple(new_ls)

        init = (tuple(jnp.full((1, qb), NEG, f32) for _ in range(HEADS)),
                tuple(jnp.zeros((1, qb), f32) for _ in range(HEADS)))
        _, ls = run_chunks(attend_chunk, init)
        row8 = lax.broadcasted_iota(i32, (SUBLANES, qb), 0)
        for hd in range(HEADS):
            l_ref[hd] = jnp.where(row8 == 0, ls[hd], 0.0)

    out_t = jnp.concatenate(
        [acc_ref[hd] / jnp.sum(l_ref[hd], axis=0, keepdims=True) for hd in range(HEADS)],
        axis=0)
    o_ref[...] = out_t.T.astype(o_ref.dtype)


def _dsa_call(qit, wit, qt, ki, k4, vt, bsz, seq):
    n = bsz * seq
    qb = DSA_QB
    nq = seq // qb
    qcol4 = lambda b, i: (0, 0, b * nq + i)
    return pl.pallas_call(
        functools.partial(_dsa_kernel, seq=seq),
        grid=(bsz, nq),
        in_specs=[
            pl.BlockSpec((IDX_HEADS, IDX_DIM, qb), qcol4),
            pl.BlockSpec((SUBLANES, qb), lambda b, i: (0, b * nq + i)),
            pl.BlockSpec((HEADS, HEAD_DIM, qb), qcol4),
            pl.BlockSpec((seq, IDX_DIM), lambda b, i: (b, 0)),
            pl.BlockSpec((HEADS, seq, LANES), lambda b, i: (0, b, 0)),
            pl.BlockSpec((HEADS, HEAD_DIM, seq), lambda b, i: (0, 0, b)),
        ],
        out_specs=pl.BlockSpec((qb, GROUP), lambda b, i: (b * nq + i, 0)),
        out_shape=jax.ShapeDtypeStruct((n, GROUP), jnp.bfloat16),
        scratch_shapes=[
            pltpu.VMEM((seq, qb), jnp.float32),
            pltpu.VMEM((seq, qb), jnp.bfloat16),
            pltpu.VMEM((HEADS, HEAD_DIM, qb), jnp.float32),
            pltpu.VMEM((HEADS, DSA_KC, qb), jnp.float32),
            pltpu.VMEM((HEADS, DSA_KC, qb), jnp.bfloat16),
            pltpu.VMEM((HEADS, LANES, qb), jnp.bfloat16),
            pltpu.VMEM((HEADS, SUBLANES, qb), jnp.float32),
            pltpu.VMEM((HEADS, SUBLANES, LANES), jnp.float32),
            pltpu.VMEM((SUBLANES, qb), jnp.int32),
        ],
        compiler_params=pltpu.CompilerParams(
            dimension_semantics=("arbitrary", "arbitrary"),
            vmem_limit_bytes=VMEM_LIMIT),
        name="dsa",
    )(qit, wit, qt, ki, k4, vt)


def _out_ffn_kernel(x_ref, yab_ref, yc_ref, yd_ref, wout_ref, gffn_ref, wg_ref, wu_ref,
                    wd_ref, gfin_ref, o_ref, *, final):
    f32, bf16 = jnp.float32, jnp.bfloat16
    y = jnp.concatenate([yab_ref[...], yc_ref[...], yd_ref[...]], axis=1)
    x1 = x_ref[...] + jnp.dot(y, wout_ref[...], preferred_element_type=f32)
    hf = _rmsnorm(x1, gffn_ref[...]).astype(bf16)
    gate = jnp.dot(hf, wg_ref[...], preferred_element_type=f32)
    up = jnp.dot(hf, wu_ref[...], preferred_element_type=f32)
    act = (gate * jax.nn.sigmoid(gate) * up).astype(bf16)
    out = x1 + jnp.dot(act, wd_ref[...], preferred_element_type=f32)
    if final:
        out = _rmsnorm(out, gfin_ref[...])
    o_ref[...] = out


def _out_ffn_call(x2d, yab, yc, yd, wout, gffn, wg, wu, wd, gfin, layer, final):
    n = x2d.shape[0]
    tm = FFN_TM
    row = lambda i: (i, 0)
    const = lambda i: (0, 0)
    resident = lambda shape: pl.BlockSpec((None,) + shape, lambda i: (layer, 0, 0),
                                          pipeline_mode=pl.Buffered(1))
    return pl.pallas_call(
        functools.partial(_out_ffn_kernel, final=final),
        grid=(n // tm,),
        in_specs=[
            pl.BlockSpec((tm, D_MODEL), row),
            pl.BlockSpec((tm, 2 * GROUP), row),
            pl.BlockSpec((tm, GROUP), row),
            pl.BlockSpec((tm, GROUP), row),
            resident((D_MODEL, D_MODEL)),
            pl.BlockSpec((1, D_MODEL), const),
            resident((D_MODEL, FFN_HIDDEN)),
            resident((D_MODEL, FFN_HIDDEN)),
            resident((FFN_HIDDEN, D_MODEL)),
            pl.BlockSpec((1, D_MODEL), const),
        ],
        out_specs=pl.BlockSpec((tm, D_MODEL), row),
        out_shape=jax.ShapeDtypeStruct((n, D_MODEL), jnp.float32),
        compiler_params=pltpu.CompilerParams(
            dimension_semantics=("arbitrary",),
            vmem_limit_bytes=VMEM_LIMIT),
        name="out_ffn",
    )(x2d, yab, yc, yd, wout, gffn, wg, wu, wd, gfin)


def _pack_w_in(w):
    g = GROUP
    cuts = [0, 3 * g, 5 * g, 6 * g, 7 * g, 8 * g, 9 * g, 9 * g + IDX_DIM,
            9 * g + IDX_DIM + IDX_HEADS, 11 * g + IDX_DIM + IDX_HEADS]
    a, b, q, k, v, qi, ki, wi, d = [w[:, cuts[t]:cuts[t + 1]] for t in range(9)]
    z = lambda cols: jnp.zeros((w.shape[0], cols), w.dtype)
    rows = jnp.concatenate([a, b, k, ki, z(LANES - IDX_DIM), d], axis=1)
    trans = jnp.concatenate([q, v, qi, wi, z(WI_ROWS - IDX_HEADS)], axis=1).T
    return rows.astype(jnp.bfloat16), trans.astype(jnp.bfloat16)


def _rope_tables(seq):
    inv_freq = ROPE_THETA ** (-jnp.arange(0, HEAD_DIM, 2, dtype=jnp.float32) / HEAD_DIM)
    ang = jnp.arange(seq, dtype=jnp.float32)[:, None] * inv_freq[None, :]
    c, s = jnp.cos(ang), jnp.sin(ang)
    reps = LANES // HEAD_DIM
    cos = jnp.tile(jnp.concatenate([c, c], axis=1), (1, reps))
    sin = jnp.tile(jnp.concatenate([-s, s], axis=1), (1, reps))
    return cos, sin, c.T, s.T


def kernel(x, g_mix, w_in, w_conv_a, gmlp_ln_g, gmlp_ln_b, w_s, b_s, w_conf, b_conf,
           conf_ln_g, conf_ln_b, w_out, g_ffn, w_gate, w_up, w_down, g_final):
    bsz, seq, d = x.shape
    depth = w_in.shape[0]
    bf16 = jnp.bfloat16
    tables = _rope_tables(seq)
    w_out_b, w_gate_b, w_up_b, w_down_b = (
        w.astype(bf16) for w in (w_out, w_gate, w_up, w_down))
    xs = x
    for l in range(depth):
        w_rows, w_t = _pack_w_in(w_in[l])
        yab, yd, k4, ki, qt, vt, qit, wit = _mix_call(
            xs.reshape(bsz, seq, d), g_mix[l][None, :], w_rows, w_t, tables,
            w_conv_a[l], gmlp_ln_g[l][None, :], gmlp_ln_b[l][None, :], w_s[l],
            jnp.repeat(b_s[l].T, HEAD_DIM, axis=1), w_conf[l], b_conf[l][None, :],
            conf_ln_g[l][None, :], conf_ln_b[l][None, :])
        yc = _dsa_call(qit, wit, qt, ki, k4, vt, bsz, seq)
        xs = _out_ffn_call(
            xs.reshape(bsz * seq, d), yab, yc, yd, w_out_b, g_ffn[l][None, :],
            w_gate_b, w_up_b, w_down_b, g_final[None, :], layer=l, final=(l == depth - 1))
    return xs.reshape(bsz, seq, d)
```

```python
import functools
import math

import jax
import jax.numpy as jnp
import numpy as np
from jax import lax
from jax.experimental import pallas as pl
from jax.experimental.pallas import tpu as pltpu

D_MODEL = 1024
GROUP = 256
HEADS = 4
HEAD_DIM = 64
CHUNK = 128
IDX_HEADS = 4
IDX_DIM = 64
TOPK = 256
SHORT_W = 3
CONF_W = 31
FFN_HIDDEN = 2816
ROPE_THETA = 10000.0
NORM_EPS = 1e-6
LN_EPS = 1e-5
NEG = -1e30
NEG_BF16 = float(np.asarray(NEG, dtype=jnp.bfloat16).astype(np.float32))

LANES = 128
SUBLANES = 8
A_HALO = 8
D_HALO = 32

COL_A = 0
COL_B = COL_A + 3 * GROUP
COL_K = COL_B + 2 * GROUP
COL_KI = COL_K + GROUP
COL_D = COL_KI + LANES
ROW_COLS = COL_D + 2 * GROUP
TROW_Q = 0
TROW_V = TROW_Q + GROUP
TROW_QI = TROW_V + GROUP
TROW_WI = TROW_QI + IDX_HEADS * IDX_DIM
WI_ROWS = 16
TROWS = TROW_WI + WI_ROWS

MIX_TM = 1024
DSA_QB = 512
DSA_KC = 512
DSA_SUB = 64
CNT_ROWS = 16
CNTB_ROWS = 32
TIE_TILE = 128
BF16_ROWS = 16
SHIFT_LIMIT = 45.0
FFN_TM = 512
FFN_SPLITS = ((0, 1536), (1536, FFN_HIDDEN))
VMEM_LIMIT = 56 * 1024 * 1024

_NT = (((1,), (1,)), ((), ()))


def _rmsnorm(xf, g):
    ms = jnp.mean(xf * xf, axis=-1, keepdims=True)
    return xf * lax.rsqrt(ms + NORM_EPS) * g


def _layernorm(xf, g, b):
    mu = jnp.mean(xf, axis=-1, keepdims=True)
    xc = xf - mu
    var = jnp.mean(xc * xc, axis=-1, keepdims=True)
    return xc * lax.rsqrt(var + LN_EPS) * g + b


def _rope128(x, cos, sin_signed):
    half = HEAD_DIM // 2
    lane = lax.broadcasted_iota(jnp.int32, x.shape, 1) % HEAD_DIM
    up = pltpu.roll(x, LANES - half, axis=1)
    dn = pltpu.roll(x, half, axis=1)
    return x * cos + jnp.where(lane < half, up, dn) * sin_signed


def _rope_t(x, cos_t, sin_t):
    half = HEAD_DIM // 2
    x1, x2 = x[0:half, :], x[half:HEAD_DIM, :]
    return x1 * cos_t - x2 * sin_t, x2 * cos_t + x1 * sin_t


def _mix_kernel(x_ref, g_ref, w_ref, wt_ref, cos_ref, sin_ref, cost_ref, sint_ref,
                wca_ref, lngb_ref, lnbb_ref, ws_ref, bsb_ref, wcf_ref, bcf_ref,
                lngd_ref, lnbd_ref,
                yab_ref, yd_ref, k4_ref, ki_ref, qt_ref, vt_ref, qit_ref, wit_ref,
                abuf, dbuf, sbuf):
    tm = x_ref.shape[0]
    f32, bf16 = jnp.float32, jnp.bfloat16
    half = HEAD_DIM // 2

    @pl.when(pl.program_id(1) == 0)
    def _():
        abuf[0:A_HALO, :] = jnp.zeros((A_HALO, GROUP), f32)
        dbuf[0:D_HALO, :] = jnp.zeros((D_HALO, GROUP), f32)

    h = _rmsnorm(x_ref[...], g_ref[...]).astype(bf16)

    def proj(lo, hi):
        return jnp.dot(h, w_ref[:, lo:hi], preferred_element_type=f32)

    def proj_t(lo, hi):
        return lax.dot_general(wt_ref[lo:hi, :], h, _NT, preferred_element_type=f32)

    zd = proj(COL_D, ROW_COLS)
    dbuf[D_HALO:D_HALO + tm, :] = zd[:, 0:GROUP] * jax.nn.sigmoid(zd[:, GROUP:2 * GROUP])
    acc = jnp.broadcast_to(bcf_ref[...], (tm, GROUP))
    first = D_HALO - (CONF_W - 1)
    for r in range(SUBLANES):
        taps = [t for t in range(CONF_W) if (first + t) % SUBLANES == r]
        span = (first + taps[-1]) - (first + taps[0]) + tm
        sbuf[0:span, :] = dbuf[pl.ds(first + taps[0], span), :]
        for t in taps:
            lo = t - taps[0]
            acc = acc + sbuf[lo:lo + tm, :] * wcf_ref[t:t + 1, :]
    y = _layernorm(acc, lngd_ref[...], lnbd_ref[...])
    yd_ref[...] = (y * jax.nn.sigmoid(y)).astype(bf16)
    dbuf[0:D_HALO, :] = dbuf[tm:tm + D_HALO, :]

    za = proj(COL_A, COL_B)
    abuf[A_HALO:A_HALO + tm, :] = za[:, GROUP:2 * GROUP] * za[:, 2 * GROUP:3 * GROUP]
    conv = abuf[pl.ds(A_HALO - 2, tm), :] * wca_ref[0:1, :]
    conv = conv + abuf[pl.ds(A_HALO - 1, tm), :] * wca_ref[1:2, :]
    conv = conv + abuf[pl.ds(A_HALO, tm), :] * wca_ref[2:3, :]
    yab_ref[:, 0:GROUP] = (za[:, 0:GROUP] * conv).astype(bf16)
    abuf[0:A_HALO, :] = abuf[tm:tm + A_HALO, :]

    zb = proj(COL_B, COL_K)
    vn = _layernorm(zb[:, GROUP:2 * GROUP], lngb_ref[...], lnbb_ref[...]).astype(bf16)
    r_i = lax.broadcasted_iota(jnp.int32, (CHUNK, CHUNK), 0)
    c_i = lax.broadcasted_iota(jnp.int32, (CHUNK, CHUNK), 1)
    wcat = jnp.concatenate(
        [jnp.where(r_i >= c_i, ws_ref[hd], 0.0).astype(bf16) for hd in range(HEADS)], axis=1)
    lane_head = lax.broadcasted_iota(jnp.int32, (CHUNK, GROUP), 1) // HEAD_DIM
    zero_b = jnp.zeros((), bf16)
    for c in range(tm // CHUNK):
        rows = slice(c * CHUNK, (c + 1) * CHUNK)
        vc = vn[rows, :]
        stacked = jnp.concatenate(
            [jnp.where(lane_head == hd, vc, zero_b) for hd in range(HEADS)], axis=0)
        mixed = bsb_ref[...] + jnp.dot(wcat, stacked, preferred_element_type=f32)
        yab_ref[rows, GROUP:2 * GROUP] = (zb[rows, 0:GROUP] * mixed).astype(bf16)

    cos = cos_ref[...]
    sin = sin_ref[...]
    zk = proj(COL_K, COL_D)
    lane = lax.broadcasted_iota(jnp.int32, (tm, LANES), 1)
    bias_col = jnp.where(lane == HEAD_DIM, 1.0, 0.0)
    for pair in range(2):
        kp = _rope128(zk[:, pair * LANES:(pair + 1) * LANES], cos, sin)
        for sub in range(2):
            kh = kp if sub == 0 else pltpu.roll(kp, HEAD_DIM, axis=1)
            k4_ref[2 * pair + sub] = jnp.where(lane < HEAD_DIM, kh, bias_col).astype(bf16)
    kip = _rope128(zk[:, GROUP:GROUP + LANES], cos, sin)
    ki_ref[...] = kip[:, 0:IDX_DIM].astype(bf16)

    cos_t = cost_ref[...]
    sin_t = sint_ref[...]
    q_scale = HEAD_DIM ** -0.5 * math.log2(math.e)
    zq = proj_t(TROW_Q, TROW_V)
    zqi = proj_t(TROW_QI, TROW_WI)
    for hd in range(HEADS):
        rows = slice(hd * HEAD_DIM, (hd + 1) * HEAD_DIM)
        o1, o2 = _rope_t(zq[rows, :], cos_t, sin_t)
        qt_ref[hd, 0:half, :] = (o1 * q_scale).astype(bf16)
        qt_ref[hd, half:HEAD_DIM, :] = (o2 * q_scale).astype(bf16)
        o1, o2 = _rope_t(zqi[rows, :], cos_t, sin_t)
        qit_ref[hd, 0:half, :] = o1.astype(bf16)
        qit_ref[hd, half:HEAD_DIM, :] = o2.astype(bf16)
    zv = proj_t(TROW_V, TROW_QI)
    for hd in range(HEADS):
        vt_ref[hd] = zv[hd * HEAD_DIM:(hd + 1) * HEAD_DIM, :].astype(bf16)
    zw = proj_t(TROW_WI, TROWS)
    wit_ref[...] = zw[0:SUBLANES, :] * (IDX_HEADS ** -0.5 * IDX_DIM ** -0.5)


def _mix_call(x, g, w_rows, w_t, tables, wca, lngb, lnbb, ws, bsb, wcf, bcf, lngd, lnbd):
    bsz, seq, _ = x.shape
    n = bsz * seq
    tm = MIX_TM
    nt = seq // tm
    bf16 = jnp.bfloat16
    half = HEAD_DIM // 2
    cos, sin, cos_t, sin_t = tables
    row = lambda b, i: (b * nt + i, 0)
    row4 = lambda b, i: (0, b * nt + i, 0)
    col = lambda b, i: (0, b * nt + i)
    col4 = lambda b, i: (0, 0, b * nt + i)
    const2 = lambda b, i: (0, 0)
    const3 = lambda b, i: (0, 0, 0)
    in_specs = [
        pl.BlockSpec((None, tm, D_MODEL), lambda b, i: (b, i, 0)),
        pl.BlockSpec((1, D_MODEL), const2),
        pl.BlockSpec((D_MODEL, ROW_COLS), const2),
        pl.BlockSpec((TROWS, D_MODEL), const2),
        pl.BlockSpec((tm, LANES), lambda b, i: (i, 0)),
        pl.BlockSpec((tm, LANES), lambda b, i: (i, 0)),
        pl.BlockSpec((half, tm), lambda b, i: (0, i)),
        pl.BlockSpec((half, tm), lambda b, i: (0, i)),
        pl.BlockSpec((SHORT_W, GROUP), const2),
        pl.BlockSpec((1, GROUP), const2),
        pl.BlockSpec((1, GROUP), const2),
        pl.BlockSpec((HEADS, CHUNK, CHUNK), const3),
        pl.BlockSpec((CHUNK, GROUP), const2),
        pl.BlockSpec((CONF_W, GROUP), const2),
        pl.BlockSpec((1, GROUP), const2),
        pl.BlockSpec((1, GROUP), const2),
        pl.BlockSpec((1, GROUP), const2),
    ]
    out_shape = [
        jax.ShapeDtypeStruct((n, 2 * GROUP), bf16),
        jax.ShapeDtypeStruct((n, GROUP), bf16),
        jax.ShapeDtypeStruct((HEADS, n, LANES), bf16),
        jax.ShapeDtypeStruct((n, IDX_DIM), bf16),
        jax.ShapeDtypeStruct((HEADS, HEAD_DIM, n), bf16),
        jax.ShapeDtypeStruct((HEADS, HEAD_DIM, n), bf16),
        jax.ShapeDtypeStruct((IDX_HEADS, IDX_DIM, n), bf16),
        jax.ShapeDtypeStruct((SUBLANES, n), jnp.float32),
    ]
    out_specs = [
        pl.BlockSpec((tm, 2 * GROUP), row),
        pl.BlockSpec((tm, GROUP), row),
        pl.BlockSpec((HEADS, tm, LANES), row4),
        pl.BlockSpec((tm, IDX_DIM), row),
        pl.BlockSpec((HEADS, HEAD_DIM, tm), col4),
        pl.BlockSpec((HEADS, HEAD_DIM, tm), col4),
        pl.BlockSpec((IDX_HEADS, IDX_DIM, tm), col4),
        pl.BlockSpec((SUBLANES, tm), col),
    ]
    return pl.pallas_call(
        _mix_kernel,
        grid=(bsz, nt),
        in_specs=in_specs,
        out_specs=out_specs,
        out_shape=out_shape,
        scratch_shapes=[pltpu.VMEM((tm + A_HALO, GROUP), jnp.float32),
                        pltpu.VMEM((tm + D_HALO, GROUP), jnp.float32),
                        pltpu.VMEM((tm + D_HALO, GROUP), jnp.float32)],
        compiler_params=pltpu.CompilerParams(
            dimension_semantics=("arbitrary", "arbitrary"),
            vmem_limit_bytes=VMEM_LIMIT),
        name="mix",
    )(x, g, w_rows, w_t, cos, sin, cos_t, sin_t, wca, lngb, lnbb, ws, bsb, wcf, bcf,
      lngd, lnbd)


def _threshold_bits(trial):
    bits = jnp.where(trial < 0, trial & jnp.int32(0x7FFFFFFF), ~trial)
    return pltpu.bitcast(bits, jnp.float32)


def _dsa_kernel(qit_ref, wit_ref, qt_ref, ki_ref, k4_ref, vt_ref, o_ref,
                sc_ref, scb_ref, acc_ref, s_ref, p_ref, qa_ref, l_ref, kmax_ref, cnt_ref,
                *, seq):
    f32, bf16, i32 = jnp.float32, jnp.bfloat16, jnp.int32
    qb = wit_ref.shape[1]
    kc, sub = DSA_KC, DSA_SUB
    ndiag = qb // kc
    i = pl.program_id(1)
    nfull = i * ndiag
    nchunks = nfull + ndiag
    n_outside = seq - nchunks * kc
    key_i = lax.broadcasted_iota(i32, (sub, qb), 0)
    qry_i = lax.broadcasted_iota(i32, (sub, qb), 1)

    def causal(x, d, r, lo=0):
        if x.shape == key_i.shape:
            k_i, q_i = key_i, qry_i
        else:
            k_i = lax.broadcasted_iota(i32, x.shape, 0)
            q_i = lax.broadcasted_iota(i32, x.shape, 1)
        return jnp.where(k_i + (d * kc + r - lo) <= q_i, x, NEG)

    def first_lane(d, r):
        return 0 if d is None else ((d * kc + r) // LANES) * LANES

    def run_chunks(chunk_fn, carry):
        carry = lax.fori_loop(0, nfull, lambda c, cr: chunk_fn(c, None, cr), carry)
        for d in range(ndiag):
            carry = chunk_fn(nfull + d, d, carry)
        return carry

    def score_chunk(c, d, carry):
        off = pl.multiple_of(c * kc, kc)
        for t in range(kc // sub):
            r0 = off + t * sub
            kis = ki_ref[pl.ds(r0, sub), :]
            lo = first_lane(d, t * sub)
            sc = None
            for j in range(IDX_HEADS):
                rel = jnp.dot(kis, qit_ref[j, :, lo:qb], preferred_element_type=f32)
                term = jnp.maximum(rel, 0.0) * wit_ref[j:j + 1, lo:qb]
                sc = term if sc is None else sc + term
            if d is not None:
                sc = causal(sc, d, t * sub, lo)
            if lo:
                sc = jnp.concatenate([jnp.full((sub, lo), NEG, f32), sc], axis=1)
            sc_ref[pl.ds(r0, sub), :] = sc
            scb_ref[pl.ds(r0, sub), :] = sc.astype(bf16)
        return carry

    run_chunks(score_chunk, 0)

    assert ndiag == 1
    lane_tile = lax.broadcasted_iota(i32, (1, qb), 1) // LANES
    n_masked = n_outside + (kc - (lane_tile + 1) * LANES)

    def count_pass(ref, rows, zero, indicator):
        def update(acc, r0, lo):
            ind = indicator(ref[pl.ds(r0, rows), lo:qb], lo)
            if lo == 0:
                return acc + ind
            return jnp.concatenate([acc[:, :lo], acc[:, lo:] + ind], axis=1)

        def body(c, acc):
            off = pl.multiple_of(c * kc, kc)
            for t in range(kc // rows):
                acc = update(acc, off + t * rows, 0)
            return acc

        acc = lax.fori_loop(0, nfull, body, zero)
        doff = pl.multiple_of(nfull * kc, kc)
        for t in range(kc // rows):
            acc = update(acc, doff + t * rows, first_lane(0, t * rows))
        return acc

    def count(cmp, thr):
        acc = count_pass(sc_ref, CNT_ROWS, jnp.zeros((CNT_ROWS, qb), i32),
                         lambda x, lo: jnp.where(cmp(x, thr[:, lo:qb]), 1, 0))
        return jnp.sum(acc, axis=0, keepdims=True)

    def count_ge(thr):
        return count(lambda x, t: x >= t, thr) + jnp.where(thr <= NEG, n_masked, 0)

    neg_b = NEG_BF16
    one_b, zero_b = jnp.ones((), bf16), jnp.zeros((), bf16)

    def count_ge_b(thr):
        thr_b = thr.astype(bf16)
        acc = count_pass(scb_ref, CNTB_ROWS, jnp.zeros((CNTB_ROWS, qb), bf16),
                         lambda x, lo: jnp.where(x >= thr_b[:, lo:qb], one_b, zero_b))
        inside = jnp.sum(acc.astype(f32), axis=0, keepdims=True).astype(i32)
        return inside + jnp.where(thr_b.astype(f32) <= neg_b, n_masked, 0)

    def coarse_step(step, prefix):
        trial = prefix | jnp.left_shift(jnp.int32(1), 31 - step)
        cnt = count_ge_b(_threshold_bits(trial))
        return jnp.where(cnt >= TOPK, trial, prefix)

    coarse = lax.fori_loop(0, 16, coarse_step, jnp.zeros((1, qb), i32))

    start = coarse - jnp.int32(1 << 16)

    def fine_step(step, state):
        prefix, c_ge, hit = state
        trial = prefix + jnp.left_shift(jnp.int32(1), 16 - step)
        cnt = count_ge(_threshold_bits(trial))
        accept = cnt >= TOPK
        return (jnp.where(accept, trial, prefix), jnp.where(accept, cnt, c_ge),
                jnp.where(accept, 1, hit))

    prefix, c_ge, hit = lax.fori_loop(
        0, 17, fine_step, (start, jnp.full((1, qb), seq, i32), jnp.zeros((1, qb), i32)))
    vstar = _threshold_bits(prefix)

    has_ties = jnp.max(c_ge) > TOPK

    @pl.when(has_ties)
    def _():
        cnt_ref[...] = jnp.broadcast_to(c_ge, cnt_ref.shape)

        @pl.when(jnp.min(hit) == 0)
        def _():
            cnt_ref[...] = jnp.broadcast_to(count_ge(vstar), cnt_ref.shape)

        excess = (cnt_ref[0:1, :] - TOPK).astype(f32)
        tt = TIE_TILE
        tri = jnp.where(lax.broadcasted_iota(i32, (tt, tt), 0)
                        <= lax.broadcasted_iota(i32, (tt, tt), 1), 1.0, 0.0).astype(bf16)

        def tie_chunk(c, d, after):
            off = pl.multiple_of(c * kc, kc)
            trailing = {}
            for t in reversed(range(kc // tt)):
                s = sc_ref[pl.ds(off + t * tt, tt), :]
                tie = jnp.where(s == vstar, 1.0, 0.0).astype(bf16)
                trailing[t] = jnp.dot(tri, tie, preferred_element_type=f32)
            for t in reversed(range(kc // tt)):
                s = sc_ref[pl.ds(off + t * tt, tt), :]
                admitted = jnp.where(after + trailing[t] > excess, jnp.inf, NEG)
                cap = jnp.where(s > vstar, jnp.inf, jnp.where(s == vstar, admitted, NEG))
                if d is not None:
                    cap = causal(cap, d, t * tt)
                sc_ref[pl.ds(off + t * tt, tt), :] = cap
                after = after + trailing[t][0:1, :]
            return after

        after = jnp.zeros((1, qb), f32)
        for d in reversed(range(ndiag)):
            after = tie_chunk(nfull + d, d, after)
        lax.fori_loop(0, nfull, lambda j, a: tie_chunk(nfull - 1 - j, None, a), after)

    def fold8(x, op):
        return op(x.reshape(sub // SUBLANES, SUBLANES, x.shape[1]), axis=0)

    @pl.when(jnp.logical_not(has_ties))
    def _():
        def cap_chunk(c, d, carry):
            off = pl.multiple_of(c * kc, kc)
            for t in range(kc // sub):
                r = t * sub
                cap = jnp.where(sc_ref[pl.ds(off + r, sub), :] >= vstar, jnp.inf, NEG)
                if d is not None:
                    cap = causal(cap, d, r)
                sc_ref[pl.ds(off + r, sub), :] = cap
            return carry

        run_chunks(cap_chunk, 0)

    acc_ref[...] = jnp.zeros(acc_ref.shape, f32)

    @pl.when(i == 0)
    def _():
        kmax_ref[...] = jnp.zeros(kmax_ref.shape, f32)

    bounds = []
    for hd in range(HEADS):
        kk = k4_ref[hd, pl.ds(pl.multiple_of(i * qb, qb), qb), :].astype(f32)
        k2 = jnp.max(jnp.sum(kk * kk, axis=1, keepdims=True), axis=0, keepdims=True)
        kmax_ref[hd] = jnp.maximum(kmax_ref[hd], k2)
        qf = qt_ref[hd].astype(f32)
        q2 = jnp.sum(qf * qf, axis=0, keepdims=True)
        bounds.append(jnp.sqrt(q2 * kmax_ref[hd][0:1, 0:1]))
    worst = bounds[0]
    for hd in range(1, HEADS):
        worst = jnp.maximum(worst, bounds[hd])
    fast = jnp.max(worst) <= SHIFT_LIMIT

    row16 = lax.broadcasted_iota(i32, (BF16_ROWS, qb), 0)
    for hd in range(HEADS):
        qa_ref[hd, 0:HEAD_DIM, :] = qt_ref[hd]
        shift = jnp.where(fast, -bounds[hd], 0.0)
        qa_ref[hd, HEAD_DIM:HEAD_DIM + BF16_ROWS, :] = jnp.where(
            row16 == 0, shift, 0.0).astype(bf16)
        qa_ref[hd, HEAD_DIM + BF16_ROWS:LANES, :] = jnp.zeros(
            (LANES - HEAD_DIM - BF16_ROWS, qb), bf16)

    def capped_logits(hd, off, r):
        s = jnp.dot(k4_ref[hd, pl.ds(off + r, sub), :], qa_ref[hd], preferred_element_type=f32)
        return jnp.minimum(s, sc_ref[pl.ds(off + r, sub), :])

    @pl.when(fast)
    def _():
        def chunk(c, d, lparts):
            off = pl.multiple_of(c * kc, kc)
            lparts = list(lparts)
            for hd in range(HEADS):
                for t in range(kc // sub):
                    r = t * sub
                    lo = first_lane(d, r)
                    s = jnp.dot(k4_ref[hd, pl.ds(off + r, sub), :], qa_ref[hd, :, lo:qb],
                                preferred_element_type=f32)
                    p = jnp.exp2(jnp.minimum(s, sc_ref[pl.ds(off + r, sub), lo:qb]))
                    l8, pb = fold8(p, jnp.sum), p.astype(bf16)
                    if lo:
                        l8 = jnp.concatenate([jnp.zeros((SUBLANES, lo), f32), l8], axis=1)
                        pb = jnp.concatenate([jnp.zeros((sub, lo), bf16), pb], axis=1)
                    lparts[hd] = lparts[hd] + l8
                    p_ref[hd, r:r + sub, :] = pb
            for hd in range(HEADS):
                acc_ref[hd] += jnp.dot(vt_ref[hd, :, pl.ds(off, kc)], p_ref[hd],
                                       preferred_element_type=f32)
            return tuple(lparts)

        lparts = run_chunks(chunk, tuple(jnp.zeros((SUBLANES, qb), f32) for _ in range(HEADS)))
        for hd in range(HEADS):
            l_ref[hd] = lparts[hd]

    @pl.when(jnp.logical_not(fast))
    def _():
        def attend_chunk(c, d, carry):
            ms, ls = carry
            off = pl.multiple_of(c * kc, kc)
            new_ms, new_ls = list(ms), list(ls)

            def logits_stage(hd):
                mpart = None
                for t in range(kc // sub):
                    r = t * sub
                    s = capped_logits(hd, off, r)
                    s_ref[hd, r:r + sub, :] = s
                    m8 = fold8(s, jnp.max)
                    mpart = m8 if mpart is None else jnp.maximum(mpart, m8)
                new_ms[hd] = jnp.maximum(ms[hd], jnp.max(mpart, axis=0, keepdims=True))

            def value_stage(hd):
                lpart = None
                for t in range(kc // sub):
                    r = t * sub
                    p = jnp.exp2(s_ref[hd, r:r + sub, :] - new_ms[hd])
                    l8 = fold8(p, jnp.sum)
                    lpart = l8 if lpart is None else lpart + l8
                    p_ref[hd, r:r + sub, :] = p.astype(bf16)
                alpha = jnp.exp2(ms[hd] - new_ms[hd])
                new_ls[hd] = alpha * ls[hd] + jnp.sum(lpart, axis=0, keepdims=True)
                pv = jnp.dot(vt_ref[hd, :, pl.ds(off, kc)], p_ref[hd],
                             preferred_element_type=f32)
                acc_ref[hd] = acc_ref[hd] * alpha + pv

            logits_stage(0)
            for hd in range(HEADS):
                if hd + 1 < HEADS:
                    logits_stage(hd + 1)
                value_stage(hd)
            return tuple(new_ms), tuple(new_ls)

        init = (tuple(jnp.full((1, qb), NEG, f32) for _ in range(HEADS)),
                tuple(jnp.zeros((1, qb), f32) for _ in range(HEADS)))
        _, ls = run_chunks(attend_chunk, init)
        row8 = lax.broadcasted_iota(i32, (SUBLANES, qb), 0)
        for hd in range(HEADS):
            l_ref[hd] = jnp.where(row8 == 0, ls[hd], 0.0)

    out_t = jnp.concatenate(
        [acc_ref[hd] / jnp.sum(l_ref[hd], axis=0, keepdims=True) for hd in range(HEADS)],
        axis=0)
    o_ref[...] = out_t.T.astype(o_ref.dtype)


def _dsa_call(qit, wit, qt, ki, k4, vt, bsz, seq):
    n = bsz * seq
    qb = DSA_QB
    nq = seq // qb
    qcol4 = lambda b, i: (0, 0, b * nq + i)
    return pl.pallas_call(
        functools.partial(_dsa_kernel, seq=seq),
        grid=(bsz, nq),
        in_specs=[
            pl.BlockSpec((IDX_HEADS, IDX_DIM, qb), qcol4),
            pl.BlockSpec((SUBLANES, qb), lambda b, i: (0, b * nq + i)),
            pl.BlockSpec((HEADS, HEAD_DIM, qb), qcol4),
            pl.BlockSpec((seq, IDX_DIM), lambda b, i: (b, 0)),
            pl.BlockSpec((HEADS, seq, LANES), lambda b, i: (0, b, 0)),
            pl.BlockSpec((HEADS, HEAD_DIM, seq), lambda b, i: (0, 0, b)),
        ],
        out_specs=pl.BlockSpec((qb, GROUP), lambda b, i: (b * nq + i, 0)),
        out_shape=jax.ShapeDtypeStruct((n, GROUP), jnp.bfloat16),
        scratch_shapes=[
            pltpu.VMEM((seq, qb), jnp.float32),
            pltpu.VMEM((seq, qb), jnp.bfloat16),
            pltpu.VMEM((HEADS, HEAD_DIM, qb), jnp.float32),
            pltpu.VMEM((HEADS, DSA_KC, qb), jnp.float32),
            pltpu.VMEM((HEADS, DSA_KC, qb), jnp.bfloat16),
            pltpu.VMEM((HEADS, LANES, qb), jnp.bfloat16),
            pltpu.VMEM((HEADS, SUBLANES, qb), jnp.float32),
            pltpu.VMEM((HEADS, SUBLANES, LANES), jnp.float32),
            pltpu.VMEM((SUBLANES, qb), jnp.int32),
        ],
        compiler_params=pltpu.CompilerParams(
            dimension_semantics=("arbitrary", "arbitrary"),
            vmem_limit_bytes=VMEM_LIMIT),
        name="dsa",
    )(qit, wit, qt, ki, k4, vt)


def _out_ffn_kernel(x_ref, yab_ref, yc_ref, yd_ref, wout_ref, gffn_ref, wg_ref, wu_ref,
                    wd_ref, gfin_ref, o_ref, *, final):
    f32, bf16 = jnp.float32, jnp.bfloat16
    y = jnp.concatenate([yab_ref[...], yc_ref[...], yd_ref[...]], axis=1)
    x1 = x_ref[...] + jnp.dot(y, wout_ref[...], preferred_element_type=f32)
    hf = _rmsnorm(x1, gffn_ref[...]).astype(bf16)
    out = x1
    for lo, hi in FFN_SPLITS:
        gate = jnp.dot(hf, wg_ref[:, lo:hi], preferred_element_type=f32)
        up = jnp.dot(hf, wu_ref[:, lo:hi], preferred_element_type=f32)
        act = (gate * jax.nn.sigmoid(gate) * up).astype(bf16)
        out = jnp.dot(act, wd_ref[lo:hi, :], preferred_element_type=f32) + out
    if final:
        out = _rmsnorm(out, gfin_ref[...])
    o_ref[...] = out


def _out_ffn_call(x2d, yab, yc, yd, wout, gffn, wg, wu, wd, gfin, layer, final):
    n = x2d.shape[0]
    tm = FFN_TM
    row = lambda i: (i, 0)
    const = lambda i: (0, 0)
    resident = lambda shape: pl.BlockSpec((None,) + shape, lambda i: (layer, 0, 0),
                                          pipeline_mode=pl.Buffered(1))
    return pl.pallas_call(
        functools.partial(_out_ffn_kernel, final=final),
        grid=(n // tm,),
        in_specs=[
            pl.BlockSpec((tm, D_MODEL), row),
            pl.BlockSpec((tm, 2 * GROUP), row),
            pl.BlockSpec((tm, GROUP), row),
            pl.BlockSpec((tm, GROUP), row),
            resident((D_MODEL, D_MODEL)),
            pl.BlockSpec((1, D_MODEL), const),
            resident((D_MODEL, FFN_HIDDEN)),
            resident((D_MODEL, FFN_HIDDEN)),
            resident((FFN_HIDDEN, D_MODEL)),
            pl.BlockSpec((1, D_MODEL), const),
        ],
        out_specs=pl.BlockSpec((tm, D_MODEL), row),
        out_shape=jax.ShapeDtypeStruct((n, D_MODEL), jnp.float32),
        compiler_params=pltpu.CompilerParams(
            dimension_semantics=("arbitrary",),
            vmem_limit_bytes=VMEM_LIMIT),
        name="out_ffn",
    )(x2d, yab, yc, yd, wout, gffn, wg, wu, wd, gfin)


def _pack_w_in(w):
    g = GROUP
    cuts = [0, 3 * g, 5 * g, 6 * g, 7 * g, 8 * g, 9 * g, 9 * g + IDX_DIM,
            9 * g + IDX_DIM + IDX_HEADS, 11 * g + IDX_DIM + IDX_HEADS]
    a, b, q, k, v, qi, ki, wi, d = [w[:, cuts[t]:cuts[t + 1]] for t in range(9)]
    z = lambda cols: jnp.zeros((w.shape[0], cols), w.dtype)
    rows = jnp.concatenate([a, b, k, ki, z(LANES - IDX_DIM), d], axis=1)
    trans = jnp.concatenate([q, v, qi, wi, z(WI_ROWS - IDX_HEADS)], axis=1).T
    return rows.astype(jnp.bfloat16), trans.astype(jnp.bfloat16)


def _rope_tables(seq):
    inv_freq = ROPE_THETA ** (-jnp.arange(0, HEAD_DIM, 2, dtype=jnp.float32) / HEAD_DIM)
    ang = jnp.arange(seq, dtype=jnp.float32)[:, None] * inv_freq[None, :]
    c, s = jnp.cos(ang), jnp.sin(ang)
    reps = LANES // HEAD_DIM
    cos = jnp.tile(jnp.concatenate([c, c], axis=1), (1, reps))
    sin = jnp.tile(jnp.concatenate([-s, s], axis=1), (1, reps))
    return cos, sin, c.T, s.T


def kernel(x, g_mix, w_in, w_conv_a, gmlp_ln_g, gmlp_ln_b, w_s, b_s, w_conf, b_conf,
           conf_ln_g, conf_ln_b, w_out, g_ffn, w_gate, w_up, w_down, g_final):
    bsz, seq, d = x.shape
    depth = w_in.shape[0]
    bf16 = jnp.bfloat16
    tables = _rope_tables(seq)
    w_out_b, w_gate_b, w_up_b, w_down_b = (
        w.astype(bf16) for w in (w_out, w_gate, w_up, w_down))
    xs = x
    for l in range(depth):
        w_rows, w_t = _pack_w_in(w_in[l])
        yab, yd, k4, ki, qt, vt, qit, wit = _mix_call(
            xs.reshape(bsz, seq, d), g_mix[l][None, :], w_rows, w_t, tables,
            w_conv_a[l], gmlp_ln_g[l][None, :], gmlp_ln_b[l][None, :], w_s[l],
            jnp.repeat(b_s[l].T, HEAD_DIM, axis=1), w_conf[l], b_conf[l][None, :],
            conf_ln_g[l][None, :], conf_ln_b[l][None, :])
        yc = _dsa_call(qit, wit, qt, ki, k4, vt, bsz, seq)
        xs = _out_ffn_call(
            xs.reshape(bsz * seq, d), yab, yc, yd, w_out_b, g_ffn[l][None, :],
            w_gate_b, w_up_b, w_down_b, g_final[None, :], layer=l, final=(l == depth - 1))
    return xs.reshape(bsz, seq, d)
```
